```python
import math
import jax
import jax.numpy as jnp
from jax import lax
import numpy as np

D_MODEL = 1024
BATCH = 2
SEQ = 8192
DEPTH = 4
DEC_BATCH = 32
DEC_SEQ = 1
PAST_LEN = 8192
PAGE_SIZE = 128

N_MIXERS = 4
GROUP_WIDTH = D_MODEL // N_MIXERS
HEAD_DIM = 64
SB_HEADS = GROUP_WIDTH // HEAD_DIM
SB_QBLOCK = 128
SSM_HEADS = GROUP_WIDTH // HEAD_DIM
SSM_STATE = 64
SSM_GROUPS = 2
SSM_CONV = 4
SSM_CHUNK = 64
SSM_CONV_DIM = GROUP_WIDTH + 2 * SSM_GROUPS * SSM_STATE
S5_GROUP_CH = 16
S5_GROUPS = GROUP_WIDTH // S5_GROUP_CH
S5_STATE = 64
HG_HEADS = GROUP_WIDTH // HEAD_DIM
HG_KDIM = 64
HG_CHUNK = 32
D_FF = 2816
FFN_CONV = 3
ALPHA = (2 * DEPTH) ** 0.25
BETA = (8 * DEPTH) ** -0.25
EPS = 1e-5
PROJ_SPLITS = (GROUP_WIDTH, GROUP_WIDTH, GROUP_WIDTH,
               GROUP_WIDTH, SSM_CONV_DIM, SSM_HEADS,
               GROUP_WIDTH,
               HG_HEADS * HG_KDIM, HG_HEADS * HG_KDIM,
               GROUP_WIDTH, GROUP_WIDTH)
D_IN = sum(PROJ_SPLITS)

kernel_name = "hybrid_sb_ssd_s5_hgrn2_decode_step"


def split_points():
    return [int(v) for v in np.cumsum(PROJ_SPLITS)[:-1]]


def layer_norm(x, g, b):
    xf = x.astype(jnp.float32)
    mu = jnp.mean(xf, axis=-1, keepdims=True)
    var = jnp.mean(jnp.square(xf - mu), axis=-1, keepdims=True)
    return ((xf - mu) * lax.rsqrt(var + EPS) * g + b).astype(x.dtype)


def rms_norm(x, g):
    xf = x.astype(jnp.float32)
    return xf * lax.rsqrt(jnp.mean(xf * xf, axis=-1, keepdims=True) + EPS) * g


def pad_time(a, pad):
    return jnp.pad(a, [(0, 0), (0, pad)] + [(0, 0)] * (a.ndim - 2))


def causal_dwconv(x, w, b, state):
    width = w.shape[0]
    L = x.shape[1]
    xp = jnp.concatenate([state.astype(x.dtype), x], axis=1)
    y = b
    for j in range(width):
        y = y + xp[:, j:j + L] * w[j]
    return y, xp[:, L:]


def gather_pages(pool, page_table):
    g = pool[page_table]
    return g.reshape(g.shape[0], g.shape[1] * g.shape[2], g.shape[3], g.shape[4])


def stick_breaking_attention(q, k, v, logit_bias):
    B, Lq, H, Dh = q.shape
    Lk = k.shape[1]
    q_off = Lk - Lq
    qb = SB_QBLOCK if Lq % SB_QBLOCK == 0 else Lq
    nb = Lq // qb
    qf = q.astype(jnp.float32) * (Dh ** -0.5)
    kf = k.astype(jnp.float32)
    vf = v.astype(jnp.float32)
    bias = logit_bias.astype(jnp.float32)[None, :, None, None]
    q_blocks = jnp.swapaxes(qf.reshape(B, nb, qb, H, Dh), 0, 1)
    key_pos = jnp.arange(Lk)

    def block(args):
        q_blk, i = args
        q_pos = q_off + i * qb + jnp.arange(qb)
        z = jnp.einsum("bqhd,bkhd->bhqk", q_blk, kf) + bias
        mask = key_pos[None, :] < q_pos[:, None]
        log_1mb = jnp.where(mask, jax.nn.log_sigmoid(-z), 0.0)
        rest = lax.cumsum(log_1mb, axis=3, reverse=True) - log_1mb
        w = jnp.where(mask, jnp.exp(jax.nn.log_sigmoid(z) + rest), 0.0)
        return jnp.einsum("bhqk,bkhd->bqhd", w, vf)

    out = lax.map(block, (q_blocks, jnp.arange(nb)))
    return jnp.swapaxes(out, 0, 1).reshape(B, Lq, H, Dh).astype(q.dtype)


def ssd_chunked(x, dt, a, bm, cm, h0):
    Bsz, L, H, P = x.shape
    C = SSM_CHUNK
    pad = (-L) % C
    x, dt, bm, cm = (pad_time(t, pad) for t in (x, dt, bm, cm))
    nc = (L + pad) // C
    rep = H // SSM_GROUPS
    bh = jnp.repeat(bm, rep, axis=2).reshape(Bsz, nc, C, H, SSM_STATE)
    ch = jnp.repeat(cm, rep, axis=2).reshape(Bsz, nc, C, H, SSM_STATE)
    xc = x.reshape(Bsz, nc, C, H, P)
    dtc = dt.reshape(Bsz, nc, C, H)
    cum = jnp.cumsum(dtc * a, axis=2)
    causal = jnp.tril(jnp.ones((C, C), dtype=bool))
    seg = cum[:, :, :, None, :] - cum[:, :, None, :, :]
    decay = jnp.exp(jnp.where(causal[None, None, :, :, None], seg, -jnp.inf))
    w = jnp.einsum("bcthn,bcshn->bctsh", ch, bh) * decay * dtc[:, :, None, :, :]
    y_intra = jnp.einsum("bctsh,bcshp->bcthp", w, xc)
    cum_last = cum[:, :, -1, :]
    chunk_state = jnp.einsum("bcsh,bcshn,bcshp->bchpn",
                             jnp.exp(cum_last[:, :, None, :] - cum) * dtc, bh, xc)

    def step(h, inp):
        cs, cd = inp
        return h * cd[:, :, None, None] + cs, h

    h_final, h_start = lax.scan(step, h0, (jnp.swapaxes(chunk_state, 0, 1),
                                           jnp.swapaxes(jnp.exp(cum_last), 0, 1)))
    h_start = jnp.swapaxes(h_start, 0, 1)
    y_inter = jnp.einsum("bcthn,bchpn->bcthp", ch * jnp.exp(cum)[..., None], h_start)
    y = (y_intra + y_inter).reshape(Bsz, nc * C, H, P)[:, :L]
    return y, h_final


def gla_chunked(q, k, v, log_f, h0):
    Bsz, L, H, K = q.shape
    V = v.shape[-1]
    C = HG_CHUNK
    pad = (-L) % C
    q, k, v, log_f = (pad_time(t, pad) for t in (q, k, v, log_f))
    nc = (L + pad) // C
    qc = q.reshape(Bsz, nc, C, H, K)
    kc = k.reshape(Bsz, nc, C, H, K)
    vc = v.reshape(Bsz, nc, C, H, V)
    b = jnp.cumsum(log_f.reshape(Bsz, nc, C, H, K), axis=2)
    ref = b[:, :, C // 2 - 1:C // 2]
    scores = jnp.einsum("bcthk,bcshk->bchts", qc * jnp.exp(b - ref), kc * jnp.exp(ref - b))
    causal = jnp.tril(jnp.ones((C, C), dtype=bool))
    y_intra = jnp.einsum("bchts,bcshv->bcthv", jnp.where(causal, scores, 0.0), vc)
    b_last = b[:, :, -1]
    chunk_state = jnp.einsum("bcshk,bcshv->bchkv", kc * jnp.exp(b_last[:, :, None] - b), vc)

    def step(h, inp):
        cs, cd = inp
        return h * cd[..., None] + cs, h

    h_final, h_start = lax.scan(step, h0, (jnp.swapaxes(chunk_state, 0, 1),
                                           jnp.swapaxes(jnp.exp(b_last), 0, 1)))
    h_start = jnp.swapaxes(h_start, 0, 1)
    y_inter = jnp.einsum("bcthk,bchkv->bcthv", qc * jnp.exp(b), h_start)
    y = (y_intra + y_inter).reshape(Bsz, nc * C, H, V)[:, :L]
    return y, h_final


def complex_affine_combine(e1, e2):
    a1r, a1i, b1r, b1i = e1
    a2r, a2i, b2r, b2i = e2
    return (a2r * a1r - a2i * a1i,
            a2r * a1i + a2i * a1r,
            a2r * b1r - a2i * b1i + b2r,
            a2r * b1i + a2i * b1r + b2i)


def mamba2_mixer(z, xbc, dt_raw, conv_s, h0, conv_w, conv_b, dt_bias, a_log, d_skip, norm_g):
    Bsz, L, _ = z.shape
    xbc, new_conv = causal_dwconv(xbc, conv_w, conv_b, conv_s)
    xbc = jax.nn.silu(xbc.astype(jnp.float32))
    xs, bm, cm = jnp.split(xbc, [GROUP_WIDTH, GROUP_WIDTH + SSM_GROUPS * SSM_STATE], axis=-1)
    xs = xs.reshape(Bsz, L, SSM_HEADS, HEAD_DIM)
    bm = bm.reshape(Bsz, L, SSM_GROUPS, SSM_STATE)
    cm = cm.reshape(Bsz, L, SSM_GROUPS, SSM_STATE)
    dt = jax.nn.softplus(dt_raw.astype(jnp.float32) + dt_bias)
    a = -jnp.exp(a_log.astype(jnp.float32))
    y, h_new = ssd_chunked(xs, dt, a, bm, cm, h0.astype(jnp.float32))
    y = y + d_skip[:, None] * xs
    gated = y.reshape(Bsz, L, GROUP_WIDTH) * jax.nn.silu(z.astype(jnp.float32))
    gated = rms_norm(gated.reshape(Bsz, L, SSM_GROUPS, GROUP_WIDTH // SSM_GROUPS),
                     norm_g.reshape(SSM_GROUPS, GROUP_WIDTH // SSM_GROUPS))
    return gated.reshape(Bsz, L, GROUP_WIDTH), new_conv, h_new


def s5_mixer(u, h0_re, h0_im, a_re, a_im, b_re, b_im, c_re, c_im, d, log_dt, w_glu, b_glu):
    Bsz, L, _ = u.shape
    uf = u.astype(jnp.float32).reshape(Bsz, L, S5_GROUPS, S5_GROUP_CH)
    dt = jnp.exp(log_dt)[:, None]
    mag = jnp.exp(a_re * dt)
    ab_re = mag * jnp.cos(a_im * dt)
    ab_im = mag * jnp.sin(a_im * dt)
    den = a_re * a_re + a_im * a_im
    coef_re = ((ab_re - 1.0) * a_re + ab_im * a_im) / den
    coef_im = (ab_im * a_re - (ab_re - 1.0) * a_im) / den
    bb_re = coef_re[..., None] * b_re - coef_im[..., None] * b_im
    bb_im = coef_re[..., None] * b_im + coef_im[..., None] * b_re
    bu_re = jnp.einsum("blgh,gnh->blgn", uf, bb_re)
    bu_im = jnp.einsum("blgh,gnh->blgn", uf, bb_im)
    h0_re = h0_re.astype(jnp.float32)
    h0_im = h0_im.astype(jnp.float32)
    bu_re = bu_re.at[:, 0].add(ab_re * h0_re - ab_im * h0_im)
    bu_im = bu_im.at[:, 0].add(ab_re * h0_im + ab_im * h0_re)
    shape = bu_re.shape
    _, _, x_re, x_im = lax.associative_scan(
        complex_affine_combine,
        (jnp.broadcast_to(ab_re, shape), jnp.broadcast_to(ab_im, shape), bu_re, bu_im), axis=1)
    y = jnp.einsum("blgn,ghn->blgh", x_re, c_re) - jnp.einsum("blgn,ghn->blgh", x_im, c_im)
    y = y + d.reshape(S5_GROUPS, S5_GROUP_CH) * uf
    y = jax.nn.gelu(y.reshape(Bsz, L, GROUP_WIDTH))
    out = y * jax.nn.sigmoid(jnp.einsum("blc,ce->ble", y, w_glu) + b_glu)
    return out, x_re[:, -1], x_im[:, -1]


def hgrn2_mixer(hq, hf, hi, hg, lb, norm_g, h0):
    Bsz, L, _ = hq.shape
    q = jax.nn.silu(hq.astype(jnp.float32)).reshape(Bsz, L, HG_HEADS, HG_KDIM)
    fr = hf.astype(jnp.float32)
    f = lb + (1.0 - lb) * jax.nn.sigmoid(fr)
    log_f = jnp.log(f).reshape(Bsz, L, HG_HEADS, HG_KDIM)
    k = ((1.0 - lb) * jax.nn.sigmoid(-fr)).reshape(Bsz, L, HG_HEADS, HG_KDIM)
    v = hi.astype(jnp.float32).reshape(Bsz, L, HG_HEADS, HEAD_DIM)
    y, h_new = gla_chunked(q, k, v, log_f, h0.astype(jnp.float32))
    o = rms_norm(y, norm_g.reshape(HG_HEADS, HEAD_DIM)) * jax.nn.silu(
        hg.astype(jnp.float32)).reshape(Bsz, L, HG_HEADS, HEAD_DIM)
    return o.reshape(Bsz, L, GROUP_WIDTH), h_new


def hgrn_lower_bounds(logits):
    pr = jax.nn.softmax(logits.astype(jnp.float32), axis=0)
    return jnp.cumsum(pr, axis=0) - pr[0:1]


def trunk_layer(x, lb, past_k, past_v, conv_s, ssm_s, s5_re_s, s5_im_s, hg_s, ffn_s, p):
    (ln1_g, ln1_b, ln2_g, ln2_b, w_in, w_out, sb_logit_bias, ssm_conv_w, ssm_conv_b,
     ssm_dt_bias, ssm_a_log, ssm_d, ssm_norm_g, s5_a_re, s5_a_im, s5_b_re, s5_b_im, s5_c_re,
     s5_c_im, s5_d, s5_log_dt, s5_w_glu, s5_b_glu, hg_norm_g, w_up, ffn_conv_w, ffn_conv_b,
     w_down) = p
    Bsz, L, _ = x.shape
    proj = jnp.einsum("bld,de->ble", x, w_in)
    (sb_q, sb_k, sb_v, ssm_z, ssm_xbc, ssm_dt, s5_u,
     hg_q, hg_f, hg_i, hg_g) = jnp.split(proj, split_points(), axis=-1)
    q = sb_q.reshape(Bsz, L, SB_HEADS, HEAD_DIM)
    k_new = sb_k.reshape(Bsz, L, SB_HEADS, HEAD_DIM)
    v_new = sb_v.reshape(Bsz, L, SB_HEADS, HEAD_DIM)
    if past_k is None:
        k_all, v_all = k_new, v_new
    else:
        k_all = jnp.concatenate([past_k.astype(k_new.dtype), k_new], axis=1)
        v_all = jnp.concatenate([past_v.astype(v_new.dtype), v_new], axis=1)
    o_a = stick_breaking_attention(q, k_all, v_all, sb_logit_bias).reshape(Bsz, L, GROUP_WIDTH)
    o_b, new_conv, new_ssm = mamba2_mixer(ssm_z, ssm_xbc, ssm_dt, conv_s, ssm_s, ssm_conv_w,
                                          ssm_conv_b, ssm_dt_bias, ssm_a_log, ssm_d, ssm_norm_g)
    o_c, new_re, new_im = s5_mixer(s5_u, s5_re_s, s5_im_s, s5_a_re, s5_a_im, s5_b_re, s5_b_im,
                                   s5_c_re, s5_c_im, s5_d, s5_log_dt, s5_w_glu, s5_b_glu)
    o_d, new_hg = hgrn2_mixer(hg_q, hg_f, hg_i, hg_g, lb, hg_norm_g, hg_s)
    mix = jnp.concatenate([o_a.astype(x.dtype), o_b.astype(x.dtype),
                           o_c.astype(x.dtype), o_d.astype(x.dtype)], axis=-1)
    mix = jnp.einsum("blc,cd->bld", mix, w_out)
    x = layer_norm(ALPHA * x + mix, ln1_g, ln1_b)
    up = jnp.einsum("bld,df->blf", x, w_up)
    u_half, g_half = jnp.split(up, 2, axis=-1)
    g_half, new_ffn = causal_dwconv(g_half, ffn_conv_w, ffn_conv_b, ffn_s)
    ffn = jnp.einsum("blf,fd->bld", jax.nn.gelu(g_half) * u_half, w_down)
    x = layer_norm(ALPHA * x + ffn, ln2_g, ln2_b)
    return x, (k_new, v_new, new_conv, new_ssm, new_re, new_im, new_hg, new_ffn)


def setup_inputs(seed: int = 0) -> dict:
    key = jax.random.key(seed)
    k = jax.random.split(key, 40)
    f32 = jnp.float32

    def nrm(i, shape, scale):
        return scale * jax.random.normal(k[i], shape, f32)

    def unif(i, shape, lo, hi):
        return jax.random.uniform(k[i], shape, f32, minval=lo, maxval=hi)

    n_pages = PAST_LEN // PAGE_SIZE
    n_phys = (5 * DEC_BATCH * n_pages) // 4
    page_table = jax.random.permutation(k[10], n_phys)[:DEC_BATCH * n_pages].reshape(
        DEC_BATCH, n_pages).astype(jnp.int32)
    dt0 = jnp.exp(unif(19, (DEPTH, SSM_HEADS), math.log(1e-3), math.log(1e-1)))
    ssm_dt_bias = dt0 + jnp.log(-jnp.expm1(-dt0))
    s5_a_re = -0.5 + nrm(23, (DEPTH, S5_GROUPS, S5_STATE), 0.01)
    s5_a_im = math.pi * jnp.arange(S5_STATE, dtype=f32) + nrm(24, (DEPTH, S5_GROUPS, S5_STATE), 0.01)
    sb_logit_bias = jnp.linspace(-4.0, -8.0, SB_HEADS, dtype=f32) + nrm(39, (DEPTH, SB_HEADS), 0.1)
    return {
        "x_prompt": nrm(0, (BATCH, SEQ, D_MODEL), 1.0),
        "x_sample": nrm(1, (DEC_BATCH, DEC_SEQ, D_MODEL), 1.0),
        "cache_k": nrm(2, (DEPTH, n_phys, PAGE_SIZE, SB_HEADS, HEAD_DIM), 1.0),
        "cache_v": nrm(3, (DEPTH, n_phys, PAGE_SIZE, SB_HEADS, HEAD_DIM), 1.0),
        "state_ssm_conv": nrm(4, (DEPTH, DEC_BATCH, SSM_CONV - 1, SSM_CONV_DIM), 1.0),
        "state_ssm": nrm(5, (DEPTH, DEC_BATCH, SSM_HEADS, HEAD_DIM, SSM_STATE), 0.5),
        "state_s5_re": nrm(6, (DEPTH, DEC_BATCH, S5_GROUPS, S5_STATE), 0.5),
        "state_s5_im": nrm(7, (DEPTH, DEC_BATCH, S5_GROUPS, S5_STATE), 0.5),
        "state_hgrn": nrm(8, (DEPTH, DEC_BATCH, HG_HEADS, HG_KDIM, HEAD_DIM), 0.5),
        "state_ffn_conv": nrm(9, (DEPTH, DEC_BATCH, FFN_CONV - 1, D_FF), 1.0),
        "page_table": page_table,
        "ln1_g": 1.0 + nrm(11, (DEPTH, D_MODEL), 0.05),
        "ln1_b": nrm(12, (DEPTH, D_MODEL), 0.02),
        "ln2_g": 1.0 + nrm(13, (DEPTH, D_MODEL), 0.05),
        "ln2_b": nrm(14, (DEPTH, D_MODEL), 0.02),
        "w_in": nrm(15, (DEPTH, D_MODEL, D_IN), D_MODEL ** -0.5),
        "w_out": nrm(16, (DEPTH, D_MODEL, D_MODEL), BETA * D_MODEL ** -0.5),
        "sb_logit_bias": sb_logit_bias,
        "ssm_conv_w": nrm(17, (DEPTH, SSM_CONV, SSM_CONV_DIM), SSM_CONV ** -0.5),
        "ssm_conv_b": nrm(18, (DEPTH, SSM_CONV_DIM), 0.02),
        "ssm_dt_bias": ssm_dt_bias,
        "ssm_a_log": jnp.log(unif(20, (DEPTH, SSM_HEADS), 1.0, 16.0)),
        "ssm_d": 1.0 + nrm(21, (DEPTH, SSM_HEADS), 0.05),
        "ssm_norm_g": 1.0 + nrm(22, (DEPTH, GROUP_WIDTH), 0.05),
        "s5_a_re": s5_a_re,
        "s5_a_im": s5_a_im,
        "s5_b_re": nrm(25, (DEPTH, S5_GROUPS, S5_STATE, S5_GROUP_CH), (2 * S5_GROUP_CH) ** -0.5),
        "s5_b_im": nrm(26, (DEPTH, S5_GROUPS, S5_STATE, S5_GROUP_CH), (2 * S5_GROUP_CH) ** -0.5),
        "s5_c_re": nrm(27, (DEPTH, S5_GROUPS, S5_GROUP_CH, S5_STATE), (2 * S5_STATE) ** -0.5),
        "s5_c_im": nrm(28, (DEPTH, S5_GROUPS, S5_GROUP_CH, S5_STATE), (2 * S5_STATE) ** -0.5),
        "s5_d": nrm(29, (DEPTH, GROUP_WIDTH), 1.0),
        "s5_log_dt": unif(30, (DEPTH, S5_GROUPS), math.log(1e-3), math.log(1e-1)),
        "s5_w_glu": nrm(31, (DEPTH, GROUP_WIDTH, GROUP_WIDTH), GROUP_WIDTH ** -0.5),
        "s5_b_glu": nrm(32, (DEPTH, GROUP_WIDTH), 0.02),
        "hg_lb_logits": nrm(33, (DEPTH, HG_HEADS * HG_KDIM), 0.3),
        "hg_norm_g": 1.0 + nrm(34, (DEPTH, GROUP_WIDTH), 0.05),
        "w_up": nrm(35, (DEPTH, D_MODEL, 2 * D_FF), D_MODEL ** -0.5),
        "ffn_conv_w": nrm(36, (DEPTH, FFN_CONV, D_FF), FFN_CONV ** -0.5),
        "ffn_conv_b": nrm(37, (DEPTH, D_FF), 0.02),
        "w_down": nrm(38, (DEPTH, D_FF, D_MODEL), BETA * D_FF ** -0.5),
    }


def reference(x_prompt, x_sample, cache_k, cache_v, state_ssm_conv, state_ssm, state_s5_re,
              state_s5_im, state_hgrn, state_ffn_conv, page_table, ln1_g, ln1_b, ln2_g, ln2_b,
              w_in, w_out, sb_logit_bias, ssm_conv_w, ssm_conv_b, ssm_dt_bias, ssm_a_log, ssm_d,
              ssm_norm_g, s5_a_re, s5_a_im, s5_b_re, s5_b_im, s5_c_re, s5_c_im, s5_d, s5_log_dt,
              s5_w_glu, s5_b_glu, hg_lb_logits, hg_norm_g, w_up, ffn_conv_w, ffn_conv_b, w_down):
    layer_weights = (ln1_g, ln1_b, ln2_g, ln2_b, w_in, w_out, sb_logit_bias, ssm_conv_w,
                     ssm_conv_b, ssm_dt_bias, ssm_a_log, ssm_d, ssm_norm_g, s5_a_re, s5_a_im,
                     s5_b_re, s5_b_im, s5_c_re, s5_c_im, s5_d, s5_log_dt, s5_w_glu, s5_b_glu,
                     hg_norm_g, w_up, ffn_conv_w, ffn_conv_b, w_down)
    lbs = hgrn_lower_bounds(hg_lb_logits)
    b_p = x_prompt.shape[0]
    z_conv = jnp.zeros((b_p, SSM_CONV - 1, SSM_CONV_DIM), x_prompt.dtype)
    z_ssm = jnp.zeros((b_p, SSM_HEADS, HEAD_DIM, SSM_STATE), jnp.float32)
    z_s5 = jnp.zeros((b_p, S5_GROUPS, S5_STATE), jnp.float32)
    z_hg = jnp.zeros((b_p, HG_HEADS, HG_KDIM, HEAD_DIM), jnp.float32)
    z_ffn = jnp.zeros((b_p, FFN_CONV - 1, D_FF), x_prompt.dtype)
    y_prompt = x_prompt
    y_sample = x_sample
    states_p = []
    states_s = []
    for l in range(DEPTH):
        p = tuple(w[l] for w in layer_weights)
        y_prompt, st_p = trunk_layer(y_prompt, lbs[l], None, None, z_conv, z_ssm, z_s5, z_s5,
                                     z_hg, z_ffn, p)
        y_sample, st_s = trunk_layer(y_sample, lbs[l], gather_pages(cache_k[l], page_table),
                                     gather_pages(cache_v[l], page_table), state_ssm_conv[l],
                                     state_ssm[l], state_s5_re[l], state_s5_im[l], state_hgrn[l],
                                     state_ffn_conv[l], p)
        states_p.append(st_p)
        states_s.append(st_s)
    sp = [jnp.stack(col, axis=0) for col in zip(*states_p)]
    ss = [jnp.stack(col, axis=0) for col in zip(*states_s)]
    return (y_prompt, y_sample, sp[0], sp[1], ss[0], ss[1], sp[2], ss[2], sp[3], ss[3],
            sp[4], ss[4], sp[5], ss[5], sp[6], ss[6], sp[7], ss[7])
```

```python
import functools
import math

import jax
import jax.numpy as jnp
from jax import lax
from jax.experimental import pallas as pl
from jax.experimental.pallas import tpu as pltpu

F32 = jnp.float32
BF16 = jnp.bfloat16

HEAD_DIM = 64
GROUP_WIDTH = 256
EPS = 1e-5
NEG_BIG = -1e30

VMEM_LIMIT_BYTES = 56 * 1024 * 1024


def _cparams(*sem):
    return pltpu.CompilerParams(dimension_semantics=sem, vmem_limit_bytes=VMEM_LIMIT_BYTES)


def _dot(a, b):
    return jnp.dot(a, b, preferred_element_type=F32)


def _dot_nt(a, b):
    return lax.dot_general(a, b, (((1,), (1,)), ((), ())), preferred_element_type=F32)


def _dot_tn(a, b):
    return lax.dot_general(a, b, (((0,), (0,)), ((), ())), preferred_element_type=F32)


def _split2(x):
    hi = x.astype(BF16)
    lo = (x - hi.astype(F32)).astype(BF16)
    return hi, lo


def _split3(x):
    hi = x.astype(BF16)
    r = x - hi.astype(F32)
    mid = r.astype(BF16)
    lo = (r - mid.astype(F32)).astype(BF16)
    return hi, mid, lo


def _mm01_left(m01, x, passes):
    parts = _split3(x) if passes == 3 else _split2(x)
    acc = _dot(m01, parts[0])
    for p in parts[1:]:
        acc = acc + _dot(m01, p)
    return acc


def _mm01_right(x, m01, passes):
    parts = _split3(x) if passes == 3 else _split2(x)
    acc = _dot(parts[0], m01)
    for p in parts[1:]:
        acc = acc + _dot(p, m01)
    return acc


def _sigmoid(x):
    return 1.0 / (1.0 + jnp.exp(-x))


def _silu(x):
    return x * _sigmoid(x)


def _softplus(x):
    return jnp.maximum(x, 0.0) + jnp.log(1.0 + jnp.exp(-jnp.abs(x)))


def _log_sigmoid_neg(z):
    return jnp.minimum(-z, 0.0) - jnp.log(1.0 + jnp.exp(-jnp.abs(z)))


def _gelu_tanh(x):
    c = math.sqrt(2.0 / math.pi)
    return 0.5 * x * (1.0 + jnp.tanh(c * (x + 0.044715 * (x * x * x))))


def _iota2(shape, dim):
    return lax.broadcasted_iota(jnp.int32, shape, dim)


PROJ_W = 3072


def _proj_kernel(x_ref, w_ref, q_ref, k_ref, v_ref, vb_ref, zx_ref, u_ref, hg_ref, *kt_refs, tk):
    x = x_ref[...].astype(BF16)

    def mm(lo, hi):
        return _dot(x, w_ref[:, lo:hi])

    q_ref[...] = mm(0, 256).astype(BF16)
    k = mm(256, 512)
    k_ref[...] = k
    v = mm(512, 768)
    v_ref[...] = v
    vb_ref[...] = v.astype(BF16)
    zx_ref[...] = mm(768, 1792)
    u_ref[...] = mm(1792, 2048)
    hg_ref[...] = mm(2048, 3072)
    if kt_refs:
        (kt_ref,) = kt_refs
        for c in range(k.shape[0] // tk):
            kt_ref[c] = k[c * tk:(c + 1) * tk, :].T.astype(BF16)


def _projection(x2d, w_p, *, tm, tk):
    n, d = x2d.shape
    grid = (n // tm,)
    row = lambda i: (i, 0)
    out_shape = [
        jax.ShapeDtypeStruct((n, 256), BF16),
        jax.ShapeDtypeStruct((n, 256), F32),
        jax.ShapeDtypeStruct((n, 256), F32),
        jax.ShapeDtypeStruct((n, 256), BF16),
        jax.ShapeDtypeStruct((n, 1024), F32),
        jax.ShapeDtypeStruct((n, 256), F32),
        jax.ShapeDtypeStruct((n, 1024), F32),
    ]
    out_specs = [pl.BlockSpec((tm, s.shape[1]), row) for s in out_shape]
    if tk:
        out_shape.append(jax.ShapeDtypeStruct((n // tk, 256, tk), BF16))
        out_specs.append(pl.BlockSpec((tm // tk, 256, tk), lambda i: (i, 0, 0)))
    return pl.pallas_call(
        functools.partial(_proj_kernel, tk=tk),
        grid=grid,
        in_specs=[pl.BlockSpec((tm, d), row), pl.BlockSpec((d, PROJ_W), lambda i: (0, 0))],
        out_specs=out_specs,
        out_shape=out_shape,
        compiler_params=_cparams("arbitrary"),
        name="projection",
    )(x2d, w_p)


CUMSUM_PASSES = 2


def _strict_lower(n):
    return jnp.where(_iota2((n, n), 0) > _iota2((n, n), 1), 1.0, 0.0).astype(BF16)


def _sb_block(q_h, kt_h, v_h, bias, u01, carry, diag_mask):
    z = _dot(q_h, kt_h) + bias
    lg = _log_sigmoid_neg(z)
    if diag_mask is not None:
        lg = jnp.where(diag_mask, lg, 0.0)
    rest = _mm01_right(lg, u01, CUMSUM_PASSES)
    w = jnp.exp(z + lg + rest + carry)
    if diag_mask is not None:
        w = jnp.where(diag_mask, w, 0.0)
    return _dot(w.astype(BF16), v_h), jnp.sum(lg, axis=1, keepdims=True)


def _attn_kernel(bias_ref, q_ref, kt_ref, v_ref, o_ref, *, t):
    hp = pl.program_id(1)
    i = pl.program_id(2)
    u01 = _strict_lower(t)
    diag_mask = _iota2((t, t), 1) < _iota2((t, t), 0)
    heads = []
    for hh in range(2):
        sl = slice(hh * HEAD_DIM, (hh + 1) * HEAD_DIM)
        heads.append((q_ref[:, sl], sl, bias_ref[2 * hp + hh]))

    carries = []
    for q_h, sl, bias in heads:
        start = pl.multiple_of(i * t, t)
        acc, tot = _sb_block(q_h, kt_ref[i, sl, :], v_ref[pl.ds(start, t), sl], bias, u01, 0.0, diag_mask)
        carries += [acc, tot]

    def body(it, carry):
        j = i - 1 - it
        start = pl.multiple_of(j * t, t)
        out = []
        for n, (q_h, sl, bias) in enumerate(heads):
            acc, tot = carry[2 * n], carry[2 * n + 1]
            d_acc, d_tot = _sb_block(q_h, kt_ref[j, sl, :], v_ref[pl.ds(start, t), sl], bias, u01, tot, None)
            out += [acc + d_acc, tot + d_tot]
        return tuple(out)

    carry = lax.fori_loop(0, i, body, tuple(carries))
    o_ref[...] = jnp.concatenate([carry[0], carry[2]], axis=1)


def _prompt_attention(q, kt, vb, bias, *, batch, seq, t):
    n = q.shape[0]
    nq = seq // t
    return pl.pallas_call(
        functools.partial(_attn_kernel, t=t),
        grid=(batch, 2, nq),
        in_specs=[
            pl.BlockSpec(memory_space=pltpu.SMEM),
            pl.BlockSpec((t, 128), lambda b, hp, i: (b * nq + i, hp)),
            pl.BlockSpec((nq, 128, t), lambda b, hp, i: (b, hp, 0)),
            pl.BlockSpec((seq, 128), lambda b, hp, i: (b, hp)),
        ],
        out_specs=pl.BlockSpec((t, 128), lambda b, hp, i: (b * nq + i, hp)),
        out_shape=jax.ShapeDtypeStruct((n, 256), F32),
        compiler_params=_cparams("arbitrary", "arbitrary", "arbitrary"),
        name="prompt_attention",
    )(bias, q, kt, vb)


def _lower_incl(n):
    return jnp.where(_iota2((n, n), 1) <= _iota2((n, n), 0), 1.0, 0.0).astype(BF16)


def _group_rms(x, gain, width):
    outs = []
    for g in range(x.shape[1] // width):
        xg = x[:, g * width:(g + 1) * width]
        ms = jnp.mean(xg * xg, axis=1, keepdims=True)
        outs.append(xg * lax.rsqrt(ms + EPS))
    return jnp.concatenate(outs, axis=1) * gain


def _ssd_kernel(zx_ref, cw_ref, cb_ref, dtb_ref, alog_ref, d_ref, ng_ref,
                o_ref, conv_ref, h_ref, tail_ref, hs_ref, *, t):
    i = pl.program_id(1)

    @pl.when(i == 0)
    def _():
        tail_ref[...] = jnp.zeros_like(tail_ref)
        hs_ref[...] = jnp.zeros_like(hs_ref)

    z = zx_ref[:, 0:256]
    xbc = zx_ref[:, 256:768]
    ext = jnp.concatenate([tail_ref[...], xbc], axis=0)
    conv = cb_ref[...]
    for j in range(4):
        conv = conv + ext[5 + j:5 + j + t] * cw_ref[j:j + 1, :]
    tail_ref[...] = xbc[t - 8:t]
    act = _silu(conv)
    xs, bm, cm = act[:, 0:256], act[:, 256:384], act[:, 384:512]
    dt = _softplus(zx_ref[:, 768:1024] + dtb_ref[...])
    da = dt * (-jnp.exp(alog_ref[...]))
    cum = _mm01_left(_lower_incl(t), da, 3)
    cum_t = cum.T
    xdt = xs * dt
    causal = _iota2((t, t), 1) <= _iota2((t, t), 0)
    bmb = bm.astype(BF16)
    cmb = cm.astype(BF16)
    gmat = [_dot_nt(cmb[:, g * 64:(g + 1) * 64], bmb[:, g * 64:(g + 1) * 64]) for g in range(2)]
    ys = []
    for h in range(4):
        hs = slice(h * 64, (h + 1) * 64)
        gs = slice((h // 2) * 64, (h // 2 + 1) * 64)
        cum_h = cum[:, hs]
        seg = cum[:, h * 64:h * 64 + 1] - cum_t[h * 64:h * 64 + 1, :]
        decay = jnp.exp(jnp.where(causal, seg, NEG_BIG))
        y_intra = _dot((gmat[h // 2] * decay).astype(BF16), xdt[:, hs].astype(BF16))
        last = cum_h[t - 1:t, :]
        h_prev = hs_ref[h]
        y_inter = _dot_nt((cm[:, gs] * jnp.exp(cum_h)).astype(BF16), h_prev.astype(BF16))
        s_c = _dot_tn((xdt[:, hs] * jnp.exp(last - cum_h)).astype(BF16), bmb[:, gs])
        hs_ref[h] = h_prev * jnp.exp(last) + s_c
        ys.append(y_intra + y_inter)
    y = jnp.concatenate(ys, axis=1) + d_ref[...] * xs
    o_ref[...] = _group_rms(y * _silu(z), ng_ref[...], 128)

    @pl.when(i == pl.num_programs(1) - 1)
    def _():
        conv_ref[0] = ext[t + 5:t + 8]
        h_ref[0] = hs_ref[...]


def _prompt_ssd(zx, cw, cb, dtb, alog, d_full, ng, *, batch, seq, t):
    n = zx.shape[0]
    nt = seq // t
    full = lambda a: pl.BlockSpec(a.shape, lambda b, i: (0,) * a.ndim)
    return pl.pallas_call(
        functools.partial(_ssd_kernel, t=t),
        grid=(batch, nt),
        in_specs=[pl.BlockSpec((t, 1024), lambda b, i: (b * nt + i, 0))] + [full(a) for a in (cw, cb, dtb, alog, d_full, ng)],
        out_specs=[
            pl.BlockSpec((t, 256), lambda b, i: (b * nt + i, 0)),
            pl.BlockSpec((1, 3, 512), lambda b, i: (b, 0, 0)),
            pl.BlockSpec((1, 4, 64, 64), lambda b, i: (b, 0, 0, 0)),
        ],
        out_shape=[
            jax.ShapeDtypeStruct((n, 256), F32),
            jax.ShapeDtypeStruct((batch, 3, 512), F32),
            jax.ShapeDtypeStruct((batch, 4, 64, 64), F32),
        ],
        scratch_shapes=[pltpu.VMEM((8, 512), F32), pltpu.VMEM((4, 64, 64), F32)],
        compiler_params=_cparams("arbitrary", "arbitrary"),
        name="prompt_ssd",
    )(zx, cw, cb, dtb, alog, d_full, ng)


HG_CHUNK = 32


def _head_mean_matrix():
    same = (_iota2((256, 256), 0) // 64) == (_iota2((256, 256), 1) // 64)
    return jnp.where(same, 1.0, 0.0).astype(BF16)


def _head_rms(y, gain):
    ms = _mm01_right(y * y, _head_mean_matrix(), 2) * (1.0 / 64.0)
    return y * lax.rsqrt(ms + EPS) * gain


def _hgrn_kernel(hg_ref, lb_ref, ng_ref, o_ref, h_ref, st_ref, *, t):
    i = pl.program_id(1)
    c = HG_CHUNK

    @pl.when(i == 0)
    def _():
        st_ref[...] = jnp.zeros_like(st_ref)

    lb = lb_ref[...]
    q = _silu(hg_ref[:, 0:256])
    fr = hg_ref[:, 256:512]
    log_f = jnp.log(lb + (1.0 - lb) * _sigmoid(fr))
    k = (1.0 - lb) * _sigmoid(-fr)
    v = hg_ref[:, 512:768]
    same_chunk = (_iota2((t, t), 0) // c) == (_iota2((t, t), 1) // c)
    lmat = jnp.where(same_chunk & (_iota2((t, t), 1) <= _iota2((t, t), 0)), 1.0, 0.0).astype(BF16)
    b_all = _mm01_left(lmat, log_f, 3)
    lane_head = _iota2((1, 256), 1) // 64
    head_lane = [lane_head == h for h in range(4)]
    stack_causal = _iota2((4 * c, c), 1) <= (_iota2((4 * c, c), 0) % c)
    block_diag = (_iota2((256, 256), 0) // 64) == (_iota2((256, 256), 1) // 64)
    ys = []
    for n in range(t // c):
        rows = slice(n * c, (n + 1) * c)
        b = b_all[rows]
        ref = b[c // 2 - 1:c // 2]
        last = b[c - 1:c]
        q_c, k_c, v_c = q[rows], k[rows], v[rows]
        qe = q_c * jnp.exp(b - ref)
        ke = (k_c * jnp.exp(ref - b)).astype(BF16)
        kl = (k_c * jnp.exp(last - b)).astype(BF16)
        qb = (q_c * jnp.exp(b)).astype(BF16)
        vb = v_c.astype(BF16)
        q_stack = jnp.concatenate([jnp.where(m, qe, 0.0) for m in head_lane], axis=0).astype(BF16)
        scores = jnp.where(stack_causal, _dot_nt(q_stack, ke), 0.0)
        y4 = _dot(scores.astype(BF16), vb)
        y_intra = jnp.where(head_lane[0], y4[0:c], 0.0)
        for h in range(1, 4):
            y_intra = y_intra + jnp.where(head_lane[h], y4[h * c:(h + 1) * c], 0.0)
        st = st_ref[...]
        y_inter = _dot_nt(qb, st.astype(BF16))
        st_ref[...] = st * jnp.exp(last) + jnp.where(block_diag, _dot_tn(vb, kl), 0.0)
        ys.append(y_intra + y_inter)
    y = jnp.concatenate(ys, axis=0)
    o_ref[...] = _head_rms(y, ng_ref[...]) * _silu(hg_ref[:, 768:1024])

    @pl.when(i == pl.num_programs(1) - 1)
    def _():
        st_t = st_ref[...].T
        for h in range(4):
            h_ref[0, h] = st_t[h * 64:(h + 1) * 64, h * 64:(h + 1) * 64]


def _prompt_hgrn(hg4, lb, ng, *, batch, seq, t):
    n = hg4.shape[0]
    nt = seq // t
    full = lambda a: pl.BlockSpec(a.shape, lambda b, i: (0,) * a.ndim)
    return pl.pallas_call(
        functools.partial(_hgrn_kernel, t=t),
        grid=(batch, nt),
        in_specs=[pl.BlockSpec((t, 1024), lambda b, i: (b * nt + i, 0)), full(lb), full(ng)],
        out_specs=[
            pl.BlockSpec((t, 256), lambda b, i: (b * nt + i, 0)),
            pl.BlockSpec((1, 4, 64, 64), lambda b, i: (b, 0, 0, 0)),
        ],
        out_shape=[jax.ShapeDtypeStruct((n, 256), F32), jax.ShapeDtypeStruct((batch, 4, 64, 64), F32)],
        scratch_shapes=[pltpu.VMEM((256, 256), F32)],
        compiler_params=_cparams("arbitrary", "arbitrary"),
        name="prompt_hgrn",
    )(hg4, lb, ng)


S5_ROW = 16
S5_GROUPS = 16
S5_STATE = 64


def _s5_discretize(a_re, a_im, b_re, b_im, log_dt):
    dt = jnp.exp(log_dt)[:, None]
    mag = jnp.exp(a_re * dt)
    ab_re = mag * jnp.cos(a_im * dt)
    ab_im = mag * jnp.sin(a_im * dt)
    den = a_re * a_re + a_im * a_im
    coef_re = ((ab_re - 1.0) * a_re + ab_im * a_im) / den
    coef_im = (ab_im * a_re - (ab_re - 1.0) * a_im) / den
    bb_re = coef_re[..., None] * b_re - coef_im[..., None] * b_im
    bb_im = coef_re[..., None] * b_im + coef_im[..., None] * b_re
    return a_re * dt, a_im * dt, bb_re, bb_im


def _lam_pow(log_mag, arg, m):
    mag = jnp.exp(m * log_mag)
    return mag * jnp.cos(m * arg), mag * jnp.sin(m * arg)


def _s5_prompt_matrices(log_mag, arg, bb_re, bb_im, c_re, c_im):
    r = S5_ROW
    hp = lax.Precision.HIGHEST
    taus = jnp.arange(r + 1, dtype=F32)[:, None, None]
    pw_re, pw_im = _lam_pow(log_mag[None], arg[None], taus)
    p_re = pw_re[..., None] * bb_re[None] - pw_im[..., None] * bb_im[None]
    p_im = pw_re[..., None] * bb_im[None] + pw_im[..., None] * bb_re[None]
    kern = (jnp.einsum("ghn,tgnk->gthk", c_re, p_re[:r], precision=hp)
            - jnp.einsum("ghn,tgnk->gthk", c_im, p_im[:r], precision=hp))
    s_idx = jnp.arange(r)[:, None]
    t_idx = jnp.arange(r)[None, :]
    tau = jnp.clip(t_idx - s_idx, 0, r - 1)
    m = kern[:, tau]
    m = jnp.where((t_idx >= s_idx)[None, :, :, None, None], m, 0.0)
    m_intra = jnp.transpose(m, (0, 1, 4, 2, 3)).reshape(S5_GROUPS, r * 16, r * 16)
    bst_re = jnp.transpose(p_re[:r][::-1], (1, 0, 3, 2)).reshape(S5_GROUPS, r * 16, S5_STATE)
    bst_im = jnp.transpose(p_im[:r][::-1], (1, 0, 3, 2)).reshape(S5_GROUPS, r * 16, S5_STATE)
    cl_re = c_re[None] * pw_re[1:, :, None, :] - c_im[None] * pw_im[1:, :, None, :]
    cl_im = c_re[None] * pw_im[1:, :, None, :] + c_im[None] * pw_re[1:, :, None, :]
    cin_re = jnp.transpose(cl_re, (1, 3, 0, 2)).reshape(S5_GROUPS, S5_STATE, r * 16)
    cin_im = -jnp.transpose(cl_im, (1, 3, 0, 2)).reshape(S5_GROUPS, S5_STATE, r * 16)
    return m_intra.astype(BF16), bst_re.astype(BF16), bst_im.astype(BF16), cin_re.astype(BF16), cin_im.astype(BF16)


def _shift_rows(x, d):
    return jnp.concatenate([jnp.zeros((d, x.shape[1]), x.dtype), x[:x.shape[0] - d]], axis=0)


def _s5_kernel(u_ref, m_ref, bre_ref, bim_ref, cre_ref, cim_ref, pre_ref, pim_ref,
               y_ref, xre_ref, xim_ref, *, rows):
    u = u_ref[0].astype(BF16)
    x_re = _dot(u, bre_ref[0])
    x_im = _dot(u, bim_ref[0])
    d, step = 1, 0
    while d < rows:
        l_re = pre_ref[0, step:step + 1, :]
        l_im = pim_ref[0, step:step + 1, :]
        s_re, s_im = _shift_rows(x_re, d), _shift_rows(x_im, d)
        x_re, x_im = x_re + l_re * s_re - l_im * s_im, x_im + l_re * s_im + l_im * s_re
        d, step = 2 * d, step + 1
    p_re, p_im = _shift_rows(x_re, 1), _shift_rows(x_im, 1)
    y_ref[0] = _dot(u, m_ref[0]) + _dot(p_re.astype(BF16), cre_ref[0]) + _dot(p_im.astype(BF16), cim_ref[0])
    xre_ref[0] = x_re[rows - 1:rows]
    xim_ref[0] = x_im[rows - 1:rows]


def _prompt_s5(ug, mats, pw_re, pw_im, *, batch, rows):
    g = S5_GROUPS
    per_g = lambda a: pl.BlockSpec((1,) + a.shape[1:], lambda b, j: (j,) + (0,) * (a.ndim - 1))
    per_bg = lambda shape: pl.BlockSpec((1,) + shape, lambda b, j: (b * g + j,) + (0,) * len(shape))
    return pl.pallas_call(
        functools.partial(_s5_kernel, rows=rows),
        grid=(batch, g),
        in_specs=[per_bg((rows, 256))] + [per_g(a) for a in (*mats, pw_re, pw_im)],
        out_specs=[per_bg((rows, 256)), per_bg((1, S5_STATE)), per_bg((1, S5_STATE))],
        out_shape=[
            jax.ShapeDtypeStruct((batch * g, rows, 256), F32),
            jax.ShapeDtypeStruct((batch * g, 1, S5_STATE), F32),
            jax.ShapeDtypeStruct((batch * g, 1, S5_STATE), F32),
        ],
        compiler_params=_cparams("arbitrary", "arbitrary"),
        name="prompt_s5",
    )(ug, *mats, pw_re, pw_im)


def _s5_prompt_scan(u2d, disc, c_re, c_im, *, batch, seq):
    log_mag, arg, bb_re, bb_im = disc
    g, r = S5_GROUPS, S5_ROW
    rows = seq // r
    mats = _s5_prompt_matrices(log_mag, arg, bb_re, bb_im, c_re, c_im)
    nsteps = max(1, (rows - 1).bit_length())
    strides = (r * (2 ** jnp.arange(16, dtype=F32)))[None, :, None]
    pw_re, pw_im = _lam_pow(log_mag[:, None, :], arg[:, None, :], strides)
    assert nsteps <= 16
    ug = u2d.reshape(batch, rows, r, g, 16).transpose(0, 3, 1, 2, 4).reshape(batch * g, rows, r * 16)
    y, x_re, x_im = _prompt_s5(ug, mats, pw_re, pw_im, batch=batch, rows=rows)
    y2d = y.reshape(batch, g, rows, r, 16).transpose(0, 2, 3, 1, 4).reshape(batch * seq, 256)
    return y2d, x_re.reshape(batch, g, S5_STATE), x_im.reshape(batch, g, S5_STATE)


D_FF = 2816
FFN_SPLIT = 2


def _layer_norm(x, g, b):
    mu = jnp.mean(x, axis=-1, keepdims=True)
    xc = x - mu
    var = jnp.mean(xc * xc, axis=-1, keepdims=True)
    return xc * lax.rsqrt(var + EPS) * g + b


def _post_kernel(*refs, t, decode, alpha):
    (x_ref, oa_ref, ob_ref, yc_ref, u_ref, od_ref, s5d_ref, wglu_ref, bglu_ref, wout_ref, ln1g_ref, ln1b_ref,
     wup_ref, fcw_ref, fcb_ref, wdown_ref, ln2g_ref, ln2b_ref) = refs[:18]
    if decode:
        st_ref, xo_ref, sto_ref = refs[18:]
    else:
        xo_ref, sto_ref, tail_ref = refs[18:]
        i = pl.program_id(1)

        @pl.when(i == 0)
        def _():
            tail_ref[...] = jnp.zeros_like(tail_ref)

    yc = _gelu_tanh(yc_ref[...] + s5d_ref[...] * u_ref[...])
    oc = yc * _sigmoid(_dot(yc.astype(BF16), wglu_ref[...]) + bglu_ref[...])
    mix = _dot(oa_ref[...].astype(BF16), wout_ref[0:256, :])
    mix = mix + _dot(ob_ref[...].astype(BF16), wout_ref[256:512, :])
    mix = mix + _dot(oc.astype(BF16), wout_ref[512:768, :])
    mix = mix + _dot(od_ref[...].astype(BF16), wout_ref[768:1024, :])
    x1 = _layer_norm(alpha * x_ref[...] + mix, ln1g_ref[...], ln1b_ref[...])
    x1b = x1.astype(BF16)
    ffn = None
    width = D_FF // FFN_SPLIT
    for c in range(FFN_SPLIT):
        lo, hi = c * width, (c + 1) * width
        uh = _dot(x1b, wup_ref[:, lo:hi])
        gh = _dot(x1b, wup_ref[:, D_FF + lo:D_FF + hi])
        if decode:
            s0 = st_ref[:, lo:hi]
            s1 = st_ref[:, D_FF + lo:D_FF + hi]
            gconv = fcb_ref[:, lo:hi] + fcw_ref[0:1, lo:hi] * s0 + fcw_ref[1:2, lo:hi] * s1 + fcw_ref[2:3, lo:hi] * gh
            sto_ref[:, lo:hi] = s1
            sto_ref[:, D_FF + lo:D_FF + hi] = gh
        else:
            ext = jnp.concatenate([tail_ref[:, lo:hi], gh], axis=0)
            gconv = fcb_ref[:, lo:hi]
            for j in range(3):
                gconv = gconv + ext[6 + j:6 + j + t] * fcw_ref[j:j + 1, lo:hi]
            tail_ref[:, lo:hi] = gh[t - 8:t]

            @pl.when(i == pl.num_programs(1) - 1)
            def _():
                sto_ref[0, :, lo:hi] = ext[t + 6:t + 8]

        hmid = (_gelu_tanh(gconv) * uh).astype(BF16)
        part = _dot(hmid, wdown_ref[lo:hi, :])
        ffn = part if ffn is None else ffn + part
    xo_ref[...] = _layer_norm(alpha * x1 + ffn, ln2g_ref[...], ln2b_ref[...])


def _post(x, oa, ob, yc, u, od, weights, *, alpha, batch=None, seq=None, t=None, ffn_state=None):
    decode = ffn_state is not None
    n = x.shape[0]
    if decode:
        t = n
        grid = (1,)
        row = lambda i: (0, 0)
        const = lambda a: pl.BlockSpec(a.shape, lambda i: (0,) * a.ndim, pipeline_mode=pl.Buffered(1))
        extra_in = [pl.BlockSpec(ffn_state.shape, row)]
        extra_args = [ffn_state]
        out_specs = [pl.BlockSpec((t, 1024), row), pl.BlockSpec((t, 2 * D_FF), row)]
        out_shape = [jax.ShapeDtypeStruct((n, 1024), F32), jax.ShapeDtypeStruct((n, 2 * D_FF), F32)]
        scratch = []
        sem = ("arbitrary",)
    else:
        nt = seq // t
        grid = (batch, nt)
        row = lambda b, i: (b * nt + i, 0)
        const = lambda a: pl.BlockSpec(a.shape, lambda b, i: (0,) * a.ndim, pipeline_mode=pl.Buffered(1))
        extra_in, extra_args = [], []
        out_specs = [pl.BlockSpec((t, 1024), row), pl.BlockSpec((1, 2, D_FF), lambda b, i: (b, 0, 0))]
        out_shape = [jax.ShapeDtypeStruct((n, 1024), F32), jax.ShapeDtypeStruct((batch, 2, D_FF), F32)]
        scratch = [pltpu.VMEM((8, D_FF), F32)]
        sem = ("arbitrary", "arbitrary")
    acts = (x, oa, ob, yc, u, od)
    return pl.pallas_call(
        functools.partial(_post_kernel, t=t, decode=decode, alpha=alpha),
        grid=grid,
        in_specs=[pl.BlockSpec((t, a.shape[1]), row) for a in acts] + [const(w) for w in weights] + extra_in,
        out_specs=out_specs,
        out_shape=out_shape,
        scratch_shapes=scratch,
        compiler_params=_cparams(*sem),
        name="post_decode" if decode else "post_prompt",
    )(*acts, *weights, *extra_args)


PAGE = 128
PAGES_PER_STEP = 8


def _decode_attn_kernel(pt_ref, q_ref, bias_ref, *refs, pp):
    del pt_ref
    k_refs, v_refs = refs[:pp], refs[pp:2 * pp]
    o_ref, c_ref, acc_ref = refs[2 * pp:]
    s = pl.program_id(1)

    @pl.when(s == 0)
    def _():
        c_ref[...] = jnp.zeros_like(c_ref)
        acc_ref[...] = jnp.zeros_like(acc_ref)

    own_head = _iota2((8, 256), 0) == _iota2((8, 256), 1) // HEAD_DIM
    qmat = jnp.where(own_head, q_ref[0], 0.0).astype(BF16)
    u01 = _strict_lower(PAGE)
    c = c_ref[...]
    acc = acc_ref[...]
    for r in reversed(range(pp)):
        z = _dot_nt(qmat, k_refs[r][0].astype(BF16)) + bias_ref[...]
        lg = _log_sigmoid_neg(z)
        rest = _mm01_right(lg, u01, CUMSUM_PASSES)
        w = jnp.exp(z + lg + rest + c)
        acc = acc + _dot(w.astype(BF16), v_refs[r][0].astype(BF16))
        c = c + jnp.sum(lg, axis=1, keepdims=True)
    c_ref[...] = c
    acc_ref[...] = acc

    @pl.when(s == pl.num_programs(1) - 1)
    def _():
        o_ref[0] = jnp.sum(jnp.where(own_head, acc, 0.0), axis=0, keepdims=True)


def _decode_attention(q, pool_k, pool_v, page_table, bias8, *, base):
    nseq, npages = page_table.shape
    pp = PAGES_PER_STEP
    nsteps = npages // pp

    def page_map(r):
        return lambda b, s, pt: (base + pt[b, (nsteps - 1 - s) * pp + r], 0, 0)

    page_specs = [pl.BlockSpec((1, PAGE, 256), page_map(r)) for r in range(pp)]
    out = pl.pallas_call(
        functools.partial(_decode_attn_kernel, pp=pp),
        grid_spec=pltpu.PrefetchScalarGridSpec(
            num_scalar_prefetch=1,
            grid=(nseq, nsteps),
            in_specs=[pl.BlockSpec((1, 1, 256), lambda b, s, pt: (b, 0, 0)),
                      pl.BlockSpec((8, PAGE), lambda b, s, pt: (0, 0))] + page_specs + page_specs,
            out_specs=pl.BlockSpec((1, 1, 256), lambda b, s, pt: (b, 0, 0)),
            scratch_shapes=[pltpu.VMEM((8, PAGE), F32), pltpu.VMEM((8, 256), F32)],
        ),
        out_shape=jax.ShapeDtypeStruct((nseq, 1, 256), F32),
        compiler_params=_cparams("arbitrary", "arbitrary"),
        name="decode_attention",
    )(page_table, q.reshape(nseq, 1, 256), bias8, *([pool_k] * pp), *([pool_v] * pp))
    return out.reshape(nseq, 256)


def _dot_f32(a, b):
    a0, a1, a2 = _split3(a)
    b0, b1, b2 = _split3(b)
    return (_dot(a0, b0) + (_dot(a0, b1) + _dot(a1, b0))
            + (_dot(a0, b2) + _dot(a2, b0) + _dot(a1, b1)))


_R_DEC, _R_XDT, _R_BM, _R_CM, _R_F, _R_K, _R_Q, _R_V, _R_END = 0, 256, 512, 640, 768, 1024, 1280, 1536, 1792


def _decode_mixers_kernel(zx_ref, u_ref, hg_ref, cs_ref, hssm_ref, x0re_ref, x0im_ref, hhg_ref,
                          cw_ref, cb_ref, dtb_ref, alog_ref, d_ref, ngs_ref, lb_ref, ngh_ref,
                          lre_ref, lim_ref, bre_ref, bim_ref, cre_ref, cim_ref,
                          ob_ref, cso_ref, hssmo_ref, yc_ref, xre_ref, xim_ref, od_ref, hhgo_ref,
                          rows_ref, yssm_ref, yhg_ref, *, nseq):
    z = zx_ref[:, 0:256]
    xbc = zx_ref[:, 256:768]
    s0, s1, s2 = cs_ref[:, 0:512], cs_ref[:, 512:1024], cs_ref[:, 1024:1536]
    conv = cb_ref[...] + cw_ref[0:1, :] * s0 + cw_ref[1:2, :] * s1 + cw_ref[2:3, :] * s2 + cw_ref[3:4, :] * xbc
    cso_ref[:, 0:512] = s1
    cso_ref[:, 512:1024] = s2
    cso_ref[:, 1024:1536] = xbc
    act = _silu(conv)
    xs = act[:, 0:256]
    dt = _softplus(zx_ref[:, 768:1024] + dtb_ref[...])
    rows_ref[:, _R_DEC:_R_XDT] = jnp.exp(dt * (-jnp.exp(alog_ref[...])))
    rows_ref[:, _R_XDT:_R_BM] = xs * dt
    rows_ref[:, _R_BM:_R_F] = act[:, 256:512]
    lb = lb_ref[...]
    fr = hg_ref[:, 256:512]
    rows_ref[:, _R_F:_R_K] = lb + (1.0 - lb) * _sigmoid(fr)
    rows_ref[:, _R_K:_R_Q] = (1.0 - lb) * _sigmoid(-fr)
    rows_ref[:, _R_Q:_R_V] = _silu(hg_ref[:, 0:256])
    rows_ref[:, _R_V:_R_END] = hg_ref[:, 512:768]

    eye = _iota2((256, 256), 0) == _iota2((256, 256), 1)
    row_id = _iota2((256, 64), 0)

    def to_col(r):
        return jnp.sum(jnp.where(eye, r, 0.0), axis=1, keepdims=True)

    def to_row(c):
        return jnp.sum(jnp.where(eye, c, 0.0), axis=0, keepdims=True)

    def per_seq(b, get):
        bm = get(_R_BM, _R_CM)
        cm = get(_R_CM, _R_F)
        bm_rows = jnp.where(row_id < 128, bm[:, 0:64], bm[:, 64:128])
        cm_rows = jnp.where(row_id < 128, cm[:, 0:64], cm[:, 64:128])
        hn = hssm_ref[b] * to_col(get(_R_DEC, _R_XDT)) + to_col(get(_R_XDT, _R_BM)) * bm_rows
        hssmo_ref[b] = hn
        y_ssm = to_row(jnp.sum(hn * cm_rows, axis=1, keepdims=True))
        v = get(_R_V, _R_END)
        v_rows = jnp.where(row_id < 64, v[:, 0:64],
                           jnp.where(row_id < 128, v[:, 64:128],
                                     jnp.where(row_id < 192, v[:, 128:192], v[:, 192:256])))
        gn = hhg_ref[b] * to_col(get(_R_F, _R_K)) + to_col(get(_R_K, _R_Q)) * v_rows
        hhgo_ref[b] = gn
        qg = to_col(get(_R_Q, _R_V)) * gn
        y_hg = jnp.concatenate(
            [jnp.sum(qg[h * 64:(h + 1) * 64], axis=0, keepdims=True) for h in range(4)], axis=1)
        return y_ssm, y_hg

    def per_octet(o, carry):
        base = pl.multiple_of(o * 8, 8)
        blk = rows_ref[pl.ds(base, 8), :]
        ys = [per_seq(base + r, lambda lo, hi, r=r: blk[r:r + 1, lo:hi]) for r in range(8)]
        yssm_ref[pl.ds(base, 8), :] = jnp.concatenate([y[0] for y in ys], axis=0)
        yhg_ref[pl.ds(base, 8), :] = jnp.concatenate([y[1] for y in ys], axis=0)
        return carry

    lax.fori_loop(0, nseq // 8, per_octet, 0)

    y = yssm_ref[...] + d_ref[...] * xs
    ob_ref[...] = _group_rms(y * _silu(z), ngs_ref[...], 128)
    od_ref[...] = _head_rms(yhg_ref[...], ngh_ref[...]) * _silu(hg_ref[:, 768:1024])
    u = u_ref[...]
    x0r, x0i = x0re_ref[...], x0im_ref[...]
    lr, li = lre_ref[...], lim_ref[...]
    xr = lr * x0r - li * x0i + _dot_f32(u, bre_ref[...])
    xi = lr * x0i + li * x0r + _dot_f32(u, bim_ref[...])
    xre_ref[...] = xr
    xim_ref[...] = xi
    yc_ref[...] = _dot_f32(xr, cre_ref[...]) - _dot_f32(xi, cim_ref[...])


def _decode_mixers(zx, u, hg4, conv_state, h_ssm, x0_re, x0_im, h_hg, params):
    nseq = zx.shape[0]
    args = (zx, u, hg4, conv_state, h_ssm, x0_re, x0_im, h_hg, *params)
    full = lambda a: pl.BlockSpec(a.shape, lambda i: (0,) * a.ndim)
    out_shape = [
        jax.ShapeDtypeStruct((nseq, 256), F32),
        jax.ShapeDtypeStruct((nseq, 1536), F32),
        jax.ShapeDtypeStruct((nseq, 256, 64), F32),
        jax.ShapeDtypeStruct((nseq, 256), F32),
        jax.ShapeDtypeStruct((nseq, 1024), F32),
        jax.ShapeDtypeStruct((nseq, 1024), F32),
        jax.ShapeDtypeStruct((nseq, 256), F32),
        jax.ShapeDtypeStruct((nseq, 256, 64), F32),
    ]
    return pl.pallas_call(
        functools.partial(_decode_mixers_kernel, nseq=nseq),
        grid=(1,),
        in_specs=[full(a) for a in args],
        out_specs=[full(s) for s in out_shape],
        out_shape=out_shape,
        scratch_shapes=[pltpu.VMEM((nseq, _R_END), F32), pltpu.VMEM((nseq, 256), F32), pltpu.VMEM((nseq, 256), F32)],
        compiler_params=_cparams("arbitrary"),
        name="decode_mixers",
    )(*args)


def _s5_decode_matrices(log_mag, arg, bb_re, bb_im, c_re, c_im):
    g, n = S5_GROUPS, S5_STATE
    eye = jnp.eye(g, dtype=F32)
    lam_re, lam_im = _lam_pow(log_mag, arg, 1.0)
    expand_b = lambda bb: jnp.einsum("gnk,gj->gkjn", bb, eye).reshape(g * 16, g * n)
    expand_c = lambda cc: jnp.einsum("ghn,gj->gnjh", cc, eye).reshape(g * n, g * 16)
    return (lam_re.reshape(1, g * n), lam_im.reshape(1, g * n),
            expand_b(bb_re), expand_b(bb_im), expand_c(c_re), expand_c(c_im))


ATTN_BLOCK = 256
PROJ_TILE = 512
SSD_TILE = 256
HGRN_TILE = 256
POST_TILE = 256


def _rearranged_w_in(w_in):
    q, k, v, z, xbc, dt, u, hq, hf, hi, hg = jnp.split(
        w_in, [256, 512, 768, 1024, 1536, 1540, 1796, 2052, 2308, 2564], axis=1)
    dt_full = jnp.repeat(dt, HEAD_DIM, axis=1)
    return jnp.concatenate([q * (HEAD_DIM ** -0.5), k, v, z, xbc, dt_full, u, hq, hf, hi, hg], axis=1).astype(BF16)


def kernel(x_prompt, x_sample, cache_k, cache_v, state_ssm_conv, state_ssm, state_s5_re, state_s5_im, state_hgrn, state_ffn_conv, page_table, ln1_g, ln1_b, ln2_g, ln2_b, w_in, w_out, sb_logit_bias, ssm_conv_w, ssm_conv_b, ssm_dt_bias, ssm_a_log, ssm_d, ssm_norm_g, s5_a_re, s5_a_im, s5_b_re, s5_b_im, s5_c_re, s5_c_im, s5_d, s5_log_dt, s5_w_glu, s5_b_glu, hg_lb_logits, hg_norm_g, w_up, ffn_conv_w, ffn_conv_b, w_down):
    depth = w_in.shape[0]
    bp, seq, dm = x_prompt.shape
    ns = x_sample.shape[0]
    n_phys = cache_k.shape[1]
    alpha = (2 * depth) ** 0.25
    row = lambda a: a.reshape(1, -1)
    rep = lambda a: jnp.repeat(a, HEAD_DIM).reshape(1, -1)

    pr = jax.nn.softmax(hg_lb_logits.astype(F32), axis=0)
    lbs = jnp.cumsum(pr, axis=0) - pr[0:1]
    pool_k = cache_k.reshape(depth * n_phys, PAGE, 256)
    pool_v = cache_v.reshape(depth * n_phys, PAGE, 256)

    xp = x_prompt.reshape(bp * seq, dm)
    xs = x_sample.reshape(ns, dm)
    outs_p, outs_s = [], []
    for l in range(depth):
        w_p = _rearranged_w_in(w_in[l])
        disc = _s5_discretize(s5_a_re[l], s5_a_im[l], s5_b_re[l], s5_b_im[l], s5_log_dt[l])
        ssd_w = (ssm_conv_w[l], row(ssm_conv_b[l]), rep(ssm_dt_bias[l]), rep(ssm_a_log[l]), rep(ssm_d[l]),
                 row(ssm_norm_g[l]))
        post_w = (row(s5_d[l]), s5_w_glu[l].astype(BF16), row(s5_b_glu[l]), w_out[l].astype(BF16),
                  row(ln1_g[l]), row(ln1_b[l]), w_up[l].astype(BF16), ffn_conv_w[l], row(ffn_conv_b[l]),
                  w_down[l].astype(BF16), row(ln2_g[l]), row(ln2_b[l]))
        lb = row(lbs[l])
        ngh = row(hg_norm_g[l])

        q, k, v, vb, zx, u, hg4, kt = _projection(xp, w_p, tm=PROJ_TILE, tk=ATTN_BLOCK)
        oa = _prompt_attention(q, kt, vb, sb_logit_bias[l], batch=bp, seq=seq, t=ATTN_BLOCK)
        ob, conv_p, ssm_p = _prompt_ssd(zx, *ssd_w, batch=bp, seq=seq, t=SSD_TILE)
        yc, re_p, im_p = _s5_prompt_scan(u, disc, s5_c_re[l], s5_c_im[l], batch=bp, seq=seq)
        od, hg_p = _prompt_hgrn(hg4, lb, ngh, batch=bp, seq=seq, t=HGRN_TILE)
        xp, ffn_p = _post(xp, oa, ob, yc, u, od, post_w, alpha=alpha, batch=bp, seq=seq, t=POST_TILE)
        outs_p.append((k.reshape(bp, seq, 4, HEAD_DIM), v.reshape(bp, seq, 4, HEAD_DIM), conv_p, ssm_p,
                       re_p, im_p, hg_p, ffn_p))

        q, k, v, _, zx, u, hg4 = _projection(xs, w_p, tm=ns, tk=0)
        bias8 = jnp.broadcast_to(jnp.pad(sb_logit_bias[l], (0, 4))[:, None], (8, PAGE))
        oa = _decode_attention(q.astype(F32), pool_k, pool_v, page_table, bias8, base=l * n_phys)
        dec_w = (*ssd_w, lb, ngh, *_s5_decode_matrices(*disc, s5_c_re[l], s5_c_im[l]))
        ob, conv_s, ssm_s, yc, re_s, im_s, od, hg_s = _decode_mixers(
            zx, u, hg4, state_ssm_conv[l].reshape(ns, 1536), state_ssm[l].reshape(ns, 256, 64),
            state_s5_re[l].reshape(ns, 1024), state_s5_im[l].reshape(ns, 1024),
            state_hgrn[l].reshape(ns, 256, 64), dec_w)
        xs, ffn_s = _post(xs, oa, ob, yc, u, od, post_w, alpha=alpha,
                          ffn_state=state_ffn_conv[l].reshape(ns, 2 * D_FF))
        outs_s.append((k.reshape(ns, 1, 4, HEAD_DIM), v.reshape(ns, 1, 4, HEAD_DIM),
                       conv_s.reshape(ns, 3, 512), ssm_s.reshape(ns, 4, 64, 64),
                       re_s.reshape(ns, S5_GROUPS, S5_STATE), im_s.reshape(ns, S5_GROUPS, S5_STATE),
                       hg_s.reshape(ns, 4, 64, 64), ffn_s.reshape(ns, 2, D_FF)))

    sp = [jnp.stack(col, axis=0) for col in zip(*outs_p)]
    ss = [jnp.stack(col, axis=0) for col in zip(*outs_s)]
    return (xp.reshape(bp, seq, dm), xs.reshape(ns, 1, dm), sp[0], sp[1], ss[0], ss[1], sp[2], ss[2],
            sp[3], ss[3], sp[4], ss[4], sp[5], ss[5], sp[6], ss[6], sp[7], ss[7])
```

```python
import functools
import math

import jax
import jax.numpy as jnp
from jax import lax
from jax.experimental import pallas as pl
from jax.experimental.pallas import tpu as pltpu

F32 = jnp.float32
BF16 = jnp.bfloat16

HEAD_DIM = 64
GROUP_WIDTH = 256
EPS = 1e-5
NEG_BIG = -1e30

VMEM_LIMIT_BYTES = 56 * 1024 * 1024


def _cparams(*sem):
    return pltpu.CompilerParams(dimension_semantics=sem, vmem_limit_bytes=VMEM_LIMIT_BYTES)


def _dot(a, b):
    return jnp.dot(a, b, preferred_element_type=F32)


def _dot_nt(a, b):
    return lax.dot_general(a, b, (((1,), (1,)), ((), ())), preferred_element_type=F32)


def _dot_tn(a, b):
    return lax.dot_general(a, b, (((0,), (0,)), ((), ())), preferred_element_type=F32)


def _split2(x):
    hi = x.astype(BF16)
    lo = (x - hi.astype(F32)).astype(BF16)
    return hi, lo


def _split3(x):
    hi = x.astype(BF16)
    r = x - hi.astype(F32)
    mid = r.astype(BF16)
    lo = (r - mid.astype(F32)).astype(BF16)
    return hi, mid, lo


def _mm01_left(m01, x, passes):
    parts = _split3(x) if passes == 3 else _split2(x)
    acc = _dot(m01, parts[0])
    for p in parts[1:]:
        acc = acc + _dot(m01, p)
    return acc


def _mm01_right(x, m01, passes):
    parts = _split3(x) if passes == 3 else _split2(x)
    acc = _dot(parts[0], m01)
    for p in parts[1:]:
        acc = acc + _dot(p, m01)
    return acc


def _sigmoid(x):
    return 1.0 / (1.0 + jnp.exp(-x))


def _silu(x):
    return x * _sigmoid(x)


def _softplus(x):
    return jnp.maximum(x, 0.0) + jnp.log(1.0 + jnp.exp(-jnp.abs(x)))


def _log_sigmoid_neg(z):
    return jnp.minimum(-z, 0.0) - jnp.log(1.0 + jnp.exp(-jnp.abs(z)))


def _gelu_tanh(x):
    c = math.sqrt(2.0 / math.pi)
    return 0.5 * x * (1.0 + jnp.tanh(c * (x + 0.044715 * (x * x * x))))


def _iota2(shape, dim):
    return lax.broadcasted_iota(jnp.int32, shape, dim)


PROJ_W = 3072


def _proj_kernel(x_ref, w_ref, *refs, tk):
    x = x_ref[...].astype(BF16)

    def mm(lo, hi):
        return _dot(x, w_ref[:, lo:hi])

    q = mm(0, 256)
    k = mm(256, 512)
    v = mm(512, 768)
    if tk:
        _, _, q_ref, vb_ref, zx_ref, u_ref, hg_ref, kt_ref, kall_ref, vall_ref = refs
        q_ref[...] = q.astype(BF16)
        vb_ref[...] = v.astype(BF16)
        k_t = k.T
        kall_ref[0, 0] = k_t
        vall_ref[0, 0] = v.T
        for c in range(k.shape[0] // tk):
            kt_ref[c] = k_t[:, c * tk:(c + 1) * tk].astype(BF16)
    else:
        q_ref, k_ref, v_ref, zx_ref, u_ref, hg_ref = refs
        q_ref[...] = q
        k_ref[...] = k
        v_ref[...] = v
    zx_ref[...] = mm(768, 1792)
    u_ref[...] = mm(1792, 2048)
    hg_ref[...] = mm(2048, 3072)


def _projection(x2d, w_p, *, tm, tk=0, seq=None, layer=None, k_all=None, v_all=None):
    n, d = x2d.shape
    row = lambda i: (i, 0)
    rows = lambda width, dtype: (jax.ShapeDtypeStruct((n, width), dtype), pl.BlockSpec((tm, width), row))
    in_specs = [pl.BlockSpec((tm, d), row), pl.BlockSpec((d, PROJ_W), lambda i: (0, 0))]
    args = [x2d, w_p]
    aliases = {}
    if tk:
        nt = seq // tm
        slab = pl.BlockSpec((1, 1, 256, tm), lambda i: (layer, i // nt, 0, i % nt))
        outs = [rows(256, BF16), rows(256, BF16), rows(1024, F32), rows(256, F32), rows(1024, F32),
                (jax.ShapeDtypeStruct((n // tk, 256, tk), BF16), pl.BlockSpec((tm // tk, 256, tk), lambda i: (i, 0, 0))),
                (jax.ShapeDtypeStruct(k_all.shape, F32), slab), (jax.ShapeDtypeStruct(v_all.shape, F32), slab)]
        in_specs += [pl.BlockSpec(memory_space=pl.ANY), pl.BlockSpec(memory_space=pl.ANY)]
        args += [k_all, v_all]
        aliases = {2: 6, 3: 7}
    else:
        outs = [rows(256, F32), rows(256, F32), rows(256, F32), rows(1024, F32), rows(256, F32), rows(1024, F32)]
    return pl.pallas_call(
        functools.partial(_proj_kernel, tk=tk),
        grid=(n // tm,),
        in_specs=in_specs,
        out_specs=[o[1] for o in outs],
        out_shape=[o[0] for o in outs],
        input_output_aliases=aliases,
        compiler_params=_cparams("arbitrary"),
        name="projection",
    )(*args)


LOG2E = 1.4426950408889634


def _suffix_ones(n):
    return jnp.where(_iota2((n, n), 0) >= _iota2((n, n), 1), 1.0, 0.0).astype(BF16)


def _softplus2(z):
    return jnp.maximum(z, 0.0) + jnp.log2(1.0 + jnp.exp2(-jnp.abs(z)))


def _sb_block(qs, kt_pair, v_pair, bias_col, ones_sfx, carry, mask):
    z = _dot(qs, kt_pair) + bias_col
    sp = _softplus2(z)
    if mask is not None:
        sp = jnp.where(mask, sp, 0.0)
    rest = _dot(sp.astype(BF16), ones_sfx)
    w = jnp.exp2(z - rest - carry)
    if mask is not None:
        w = jnp.where(mask, w, 0.0)
    return _dot(w.astype(BF16), v_pair), rest[:, 0:1]


def _attn_kernel(bias_ref, q_ref, kt_ref, v_ref, o_ref, qs_ref, acc_ref, *, tq, tk):
    hp = pl.program_id(1)
    i = pl.program_id(2)
    ones_sfx = _suffix_ones(tk)
    r = tq // tk
    lane_head = _iota2((tq, 128), 1) // HEAD_DIM
    q = q_ref[...]
    qs_ref[0:tq, :] = jnp.where(lane_head == 0, q, jnp.zeros_like(q))
    qs_ref[tq:2 * tq, :] = jnp.where(lane_head == 1, q, jnp.zeros_like(q))
    bias0 = bias_ref[2 * hp] * LOG2E
    bias1 = bias_ref[2 * hp + 1] * LOG2E

    def kv(j):
        start = pl.multiple_of(j * tk, tk)
        return kt_ref[j], v_ref[pl.ds(start, tk), :]

    acc = jnp.zeros((2 * tq, 128), F32)
    tot = jnp.zeros((2 * tq, 1), F32)
    for d in reversed(range(r)):
        lo = d * tk
        nd = tq - lo
        qs_d = jnp.concatenate([qs_ref[lo:tq, :], qs_ref[tq + lo:2 * tq, :]], axis=0)
        tot_d = jnp.concatenate([tot[lo:tq], tot[tq + lo:2 * tq]], axis=0)
        bias_d = jnp.where(_iota2((2 * nd, 1), 0) < nd, bias0, bias1)
        mask = _iota2((2 * nd, tk), 1) < _iota2((2 * nd, tk), 0) % nd
        kt_p, v_p = kv(i * r + d)
        da, dt = _sb_block(qs_d, kt_p, v_p, bias_d, ones_sfx, tot_d, mask)
        pieces_a = [da[:nd], da[nd:]]
        pieces_t = [dt[:nd], dt[nd:]]
        if lo:
            pieces_a = [jnp.zeros((lo, 128), F32), da[:nd], jnp.zeros((lo, 128), F32), da[nd:]]
            pieces_t = [jnp.zeros((lo, 1), F32), dt[:nd], jnp.zeros((lo, 1), F32), dt[nd:]]
        acc = acc + jnp.concatenate(pieces_a, axis=0)
        tot = tot + jnp.concatenate(pieces_t, axis=0)
    acc_ref[...] = acc
    bias_col = jnp.where(_iota2((2 * tq, 1), 0) < tq, bias0, bias1)

    def body(it, tot):
        kt_p, v_p = kv(i * r - 1 - it)
        da, dt = _sb_block(qs_ref[...], kt_p, v_p, bias_col, ones_sfx, tot, None)
        acc_ref[...] += da
        return tot + dt

    lax.fori_loop(0, i * r, body, tot)
    o_ref[...] = jnp.where(lane_head == 0, acc_ref[0:tq, :], acc_ref[tq:2 * tq, :])


def _prompt_attention(q, kt, vb, bias, *, batch, seq, tq, tk):
    n = q.shape[0]
    nq = seq // tq
    return pl.pallas_call(
        functools.partial(_attn_kernel, tq=tq, tk=tk),
        grid=(batch, 2, nq),
        in_specs=[
            pl.BlockSpec(memory_space=pltpu.SMEM),
            pl.BlockSpec((tq, 128), lambda b, hp, i: (b * nq + i, hp)),
            pl.BlockSpec((seq // tk, 128, tk), lambda b, hp, i: (b, hp, 0)),
            pl.BlockSpec((seq, 128), lambda b, hp, i: (b, hp)),
        ],
        out_specs=pl.BlockSpec((tq, 128), lambda b, hp, i: (b * nq + i, hp)),
        out_shape=jax.ShapeDtypeStruct((n, 256), F32),
        scratch_shapes=[pltpu.VMEM((2 * tq, 128), BF16), pltpu.VMEM((2 * tq, 128), F32)],
        compiler_params=_cparams("arbitrary", "arbitrary", "arbitrary"),
        name="prompt_attention",
    )(bias, q, kt, vb)


def _lower_incl(n):
    return jnp.where(_iota2((n, n), 1) <= _iota2((n, n), 0), 1.0, 0.0).astype(BF16)


def _group_rms(x, gain, width):
    outs = []
    for g in range(x.shape[1] // width):
        xg = x[:, g * width:(g + 1) * width]
        ms = jnp.mean(xg * xg, axis=1, keepdims=True)
        outs.append(xg * lax.rsqrt(ms + EPS))
    return jnp.concatenate(outs, axis=1) * gain


def _ssd_kernel(zx_ref, cw_ref, cb_ref, dtb_ref, alog_ref, d_ref, ng_ref,
                o_ref, conv_ref, h_ref, tail_ref, hs_ref, *, t):
    i = pl.program_id(1)

    @pl.when(i == 0)
    def _():
        tail_ref[...] = jnp.zeros_like(tail_ref)
        hs_ref[...] = jnp.zeros_like(hs_ref)

    z = zx_ref[:, 0:256]
    xbc = zx_ref[:, 256:768]
    ext = jnp.concatenate([tail_ref[...], xbc], axis=0)
    conv = cb_ref[...]
    for j in range(4):
        conv = conv + ext[5 + j:5 + j + t] * cw_ref[j:j + 1, :]
    tail_ref[...] = xbc[t - 8:t]
    act = _silu(conv)
    xs, bm, cm = act[:, 0:256], act[:, 256:384], act[:, 384:512]
    dt = _softplus(zx_ref[:, 768:1024] + dtb_ref[...])
    da = dt * (-jnp.exp(alog_ref[...]))
    cum = _mm01_left(_lower_incl(t), da, 3)
    cum_t = cum.T
    xdt = xs * dt
    causal = _iota2((t, t), 1) <= _iota2((t, t), 0)
    bmb = bm.astype(BF16)
    cmb = cm.astype(BF16)
    gmat = [_dot_nt(cmb[:, g * 64:(g + 1) * 64], bmb[:, g * 64:(g + 1) * 64]) for g in range(2)]
    ys = []
    for h in range(4):
        hs = slice(h * 64, (h + 1) * 64)
        gs = slice((h // 2) * 64, (h // 2 + 1) * 64)
        cum_h = cum[:, hs]
        seg = cum[:, h * 64:h * 64 + 1] - cum_t[h * 64:h * 64 + 1, :]
        decay = jnp.exp(jnp.where(causal, seg, NEG_BIG))
        y_intra = _dot((gmat[h // 2] * decay).astype(BF16), xdt[:, hs].astype(BF16))
        last = cum_h[t - 1:t, :]
        h_prev = hs_ref[h]
        y_inter = _dot_nt((cm[:, gs] * jnp.exp(cum_h)).astype(BF16), h_prev.astype(BF16))
        s_c = _dot_tn((xdt[:, hs] * jnp.exp(last - cum_h)).astype(BF16), bmb[:, gs])
        hs_ref[h] = h_prev * jnp.exp(last) + s_c
        ys.append(y_intra + y_inter)
    y = jnp.concatenate(ys, axis=1) + d_ref[...] * xs
    o_ref[...] = _group_rms(y * _silu(z), ng_ref[...], 128)

    @pl.when(i == pl.num_programs(1) - 1)
    def _():
        conv_ref[0] = ext[t + 5:t + 8]
        h_ref[0] = hs_ref[...]


def _prompt_ssd(zx, cw, cb, dtb, alog, d_full, ng, *, batch, seq, t):
    n = zx.shape[0]
    nt = seq // t
    full = lambda a: pl.BlockSpec(a.shape, lambda b, i: (0,) * a.ndim)
    return pl.pallas_call(
        functools.partial(_ssd_kernel, t=t),
        grid=(batch, nt),
        in_specs=[pl.BlockSpec((t, 1024), lambda b, i: (b * nt + i, 0))] + [full(a) for a in (cw, cb, dtb, alog, d_full, ng)],
        out_specs=[
            pl.BlockSpec((t, 256), lambda b, i: (b * nt + i, 0)),
            pl.BlockSpec((1, 3, 512), lambda b, i: (b, 0, 0)),
            pl.BlockSpec((1, 4, 64, 64), lambda b, i: (b, 0, 0, 0)),
        ],
        out_shape=[
            jax.ShapeDtypeStruct((n, 256), F32),
            jax.ShapeDtypeStruct((batch, 3, 512), F32),
            jax.ShapeDtypeStruct((batch, 4, 64, 64), F32),
        ],
        scratch_shapes=[pltpu.VMEM((8, 512), F32), pltpu.VMEM((4, 64, 64), F32)],
        compiler_params=_cparams("arbitrary", "arbitrary"),
        name="prompt_ssd",
    )(zx, cw, cb, dtb, alog, d_full, ng)


HG_CHUNK = 32


def _head_mean_matrix():
    same = (_iota2((256, 256), 0) // 64) == (_iota2((256, 256), 1) // 64)
    return jnp.where(same, 1.0, 0.0).astype(BF16)


def _head_rms(y, gain):
    ms = _mm01_right(y * y, _head_mean_matrix(), 2) * (1.0 / 64.0)
    return y * lax.rsqrt(ms + EPS) * gain


def _hgrn_kernel(hg_ref, lb_ref, ng_ref, o_ref, h_ref, st_ref, *, t):
    i = pl.program_id(1)
    c = HG_CHUNK

    @pl.when(i == 0)
    def _():
        st_ref[...] = jnp.zeros_like(st_ref)

    lb = lb_ref[...]
    q = _silu(hg_ref[:, 0:256])
    fr = hg_ref[:, 256:512]
    log_f = jnp.log(lb + (1.0 - lb) * _sigmoid(fr))
    k = (1.0 - lb) * _sigmoid(-fr)
    v = hg_ref[:, 512:768]
    same_chunk = (_iota2((t, t), 0) // c) == (_iota2((t, t), 1) // c)
    lmat = jnp.where(same_chunk & (_iota2((t, t), 1) <= _iota2((t, t), 0)), 1.0, 0.0).astype(BF16)
    b_all = _mm01_left(lmat, log_f, 3)
    lane_head = _iota2((1, 256), 1) // 64
    head_lane = [lane_head == h for h in range(4)]
    stack_causal = _iota2((4 * c, c), 1) <= (_iota2((4 * c, c), 0) % c)
    block_diag = (_iota2((256, 256), 0) // 64) == (_iota2((256, 256), 1) // 64)
    ys = []
    for n in range(t // c):
        rows = slice(n * c, (n + 1) * c)
        b = b_all[rows]
        ref = b[c // 2 - 1:c // 2]
        last = b[c - 1:c]
        q_c, k_c, v_c = q[rows], k[rows], v[rows]
        qe = q_c * jnp.exp(b - ref)
        ke = (k_c * jnp.exp(ref - b)).astype(BF16)
        kl = (k_c * jnp.exp(last - b)).astype(BF16)
        qb = (q_c * jnp.exp(b)).astype(BF16)
        vb = v_c.astype(BF16)
        q_stack = jnp.concatenate([jnp.where(m, qe, 0.0) for m in head_lane], axis=0).astype(BF16)
        scores = jnp.where(stack_causal, _dot_nt(q_stack, ke), 0.0)
        y4 = _dot(scores.astype(BF16), vb)
        y_intra = jnp.where(head_lane[0], y4[0:c], 0.0)
        for h in range(1, 4):
            y_intra = y_intra + jnp.where(head_lane[h], y4[h * c:(h + 1) * c], 0.0)
        st = st_ref[...]
        y_inter = _dot_nt(qb, st.astype(BF16))
        st_ref[...] = st * jnp.exp(last) + jnp.where(block_diag, _dot_tn(vb, kl), 0.0)
        ys.append(y_intra + y_inter)
    y = jnp.concatenate(ys, axis=0)
    o_ref[...] = _head_rms(y, ng_ref[...]) * _silu(hg_ref[:, 768:1024])

    @pl.when(i == pl.num_programs(1) - 1)
    def _():
        st_t = st_ref[...].T
        for h in range(4):
            h_ref[0, h] = st_t[h * 64:(h + 1) * 64, h * 64:(h + 1) * 64]


def _prompt_hgrn(hg4, lb, ng, *, batch, seq, t):
    n = hg4.shape[0]
    nt = seq // t
    full = lambda a: pl.BlockSpec(a.shape, lambda b, i: (0,) * a.ndim)
    return pl.pallas_call(
        functools.partial(_hgrn_kernel, t=t),
        grid=(batch, nt),
        in_specs=[pl.BlockSpec((t, 1024), lambda b, i: (b * nt + i, 0)), full(lb), full(ng)],
        out_specs=[
            pl.BlockSpec((t, 256), lambda b, i: (b * nt + i, 0)),
            pl.BlockSpec((1, 4, 64, 64), lambda b, i: (b, 0, 0, 0)),
        ],
        out_shape=[jax.ShapeDtypeStruct((n, 256), F32), jax.ShapeDtypeStruct((batch, 4, 64, 64), F32)],
        scratch_shapes=[pltpu.VMEM((256, 256), F32)],
        compiler_params=_cparams("arbitrary", "arbitrary"),
        name="prompt_hgrn",
    )(hg4, lb, ng)


S5_ROW = 16
S5_GROUPS = 16
S5_STATE = 64


def _s5_discretize(a_re, a_im, b_re, b_im, log_dt):
    dt = jnp.exp(log_dt)[:, None]
    mag = jnp.exp(a_re * dt)
    ab_re = mag * jnp.cos(a_im * dt)
    ab_im = mag * jnp.sin(a_im * dt)
    den = a_re * a_re + a_im * a_im
    coef_re = ((ab_re - 1.0) * a_re + ab_im * a_im) / den
    coef_im = (ab_im * a_re - (ab_re - 1.0) * a_im) / den
    bb_re = coef_re[..., None] * b_re - coef_im[..., None] * b_im
    bb_im = coef_re[..., None] * b_im + coef_im[..., None] * b_re
    return a_re * dt, a_im * dt, bb_re, bb_im


def _lam_pow(log_mag, arg, m):
    mag = jnp.exp(m * log_mag)
    return mag * jnp.cos(m * arg), mag * jnp.sin(m * arg)


def _s5_prompt_matrices(log_mag, arg, bb_re, bb_im, c_re, c_im):
    r = S5_ROW
    hp = lax.Precision.HIGHEST
    taus = jnp.arange(r + 1, dtype=F32)[:, None, None]
    pw_re, pw_im = _lam_pow(log_mag[None], arg[None], taus)
    p_re = pw_re[..., None] * bb_re[None] - pw_im[..., None] * bb_im[None]
    p_im = pw_re[..., None] * bb_im[None] + pw_im[..., None] * bb_re[None]
    kern = (jnp.einsum("ghn,tgnk->gthk", c_re, p_re[:r], precision=hp)
            - jnp.einsum("ghn,tgnk->gthk", c_im, p_im[:r], precision=hp))
    s_idx = jnp.arange(r)[:, None]
    t_idx = jnp.arange(r)[None, :]
    tau = jnp.clip(t_idx - s_idx, 0, r - 1)
    m = kern[:, tau]
    m = jnp.where((t_idx >= s_idx)[None, :, :, None, None], m, 0.0)
    m_intra = jnp.transpose(m, (0, 1, 4, 2, 3)).reshape(S5_GROUPS, r * 16, r * 16)
    bst_re = jnp.transpose(p_re[:r][::-1], (1, 0, 3, 2)).reshape(S5_GROUPS, r * 16, S5_STATE)
    bst_im = jnp.transpose(p_im[:r][::-1], (1, 0, 3, 2)).reshape(S5_GROUPS, r * 16, S5_STATE)
    cl_re = c_re[None] * pw_re[1:, :, None, :] - c_im[None] * pw_im[1:, :, None, :]
    cl_im = c_re[None] * pw_im[1:, :, None, :] + c_im[None] * pw_re[1:, :, None, :]
    cin_re = jnp.transpose(cl_re, (1, 3, 0, 2)).reshape(S5_GROUPS, S5_STATE, r * 16)
    cin_im = -jnp.transpose(cl_im, (1, 3, 0, 2)).reshape(S5_GROUPS, S5_STATE, r * 16)
    return m_intra.astype(BF16), bst_re.astype(BF16), bst_im.astype(BF16), cin_re.astype(BF16), cin_im.astype(BF16)


def _shift_rows(x, d):
    return jnp.concatenate([jnp.zeros((d, x.shape[1]), x.dtype), x[:x.shape[0] - d]], axis=0)


def _s5_kernel(u_ref, m_ref, bre_ref, bim_ref, cre_ref, cim_ref, pre_ref, pim_ref,
               y_ref, xre_ref, xim_ref, *, rows):
    u = u_ref[0].astype(BF16)
    x_re = _dot(u, bre_ref[0])
    x_im = _dot(u, bim_ref[0])
    d, step = 1, 0
    while d < rows:
        l_re = pre_ref[0, step:step + 1, :]
        l_im = pim_ref[0, step:step + 1, :]
        s_re, s_im = _shift_rows(x_re, d), _shift_rows(x_im, d)
        x_re, x_im = x_re + l_re * s_re - l_im * s_im, x_im + l_re * s_im + l_im * s_re
        d, step = 2 * d, step + 1
    p_re, p_im = _shift_rows(x_re, 1), _shift_rows(x_im, 1)
    y_ref[0] = _dot(u, m_ref[0]) + _dot(p_re.astype(BF16), cre_ref[0]) + _dot(p_im.astype(BF16), cim_ref[0])
    xre_ref[0] = x_re[rows - 1:rows]
    xim_ref[0] = x_im[rows - 1:rows]


def _prompt_s5(ug, mats, pw_re, pw_im, *, batch, rows):
    g = S5_GROUPS
    per_g = lambda a: pl.BlockSpec((1,) + a.shape[1:], lambda b, j: (j,) + (0,) * (a.ndim - 1))
    per_bg = lambda shape: pl.BlockSpec((1,) + shape, lambda b, j: (b * g + j,) + (0,) * len(shape))
    return pl.pallas_call(
        functools.partial(_s5_kernel, rows=rows),
        grid=(batch, g),
        in_specs=[per_bg((rows, 256))] + [per_g(a) for a in (*mats, pw_re, pw_im)],
        out_specs=[per_bg((rows, 256)), per_bg((1, S5_STATE)), per_bg((1, S5_STATE))],
        out_shape=[
            jax.ShapeDtypeStruct((batch * g, rows, 256), F32),
            jax.ShapeDtypeStruct((batch * g, 1, S5_STATE), F32),
            jax.ShapeDtypeStruct((batch * g, 1, S5_STATE), F32),
        ],
        compiler_params=_cparams("arbitrary", "arbitrary"),
        name="prompt_s5",
    )(ug, *mats, pw_re, pw_im)


def _s5_prompt_scan(u2d, disc, c_re, c_im, *, batch, seq):
    log_mag, arg, bb_re, bb_im = disc
    g, r = S5_GROUPS, S5_ROW
    rows = seq // r
    mats = _s5_prompt_matrices(log_mag, arg, bb_re, bb_im, c_re, c_im)
    nsteps = max(1, (rows - 1).bit_length())
    strides = (r * (2 ** jnp.arange(16, dtype=F32)))[None, :, None]
    pw_re, pw_im = _lam_pow(log_mag[:, None, :], arg[:, None, :], strides)
    assert nsteps <= 16
    ug = u2d.reshape(batch, rows, r, g, 16).transpose(0, 3, 1, 2, 4).reshape(batch * g, rows, r * 16)
    y, x_re, x_im = _prompt_s5(ug, mats, pw_re, pw_im, batch=batch, rows=rows)
    y2d = y.reshape(batch, g, rows, r, 16).transpose(0, 2, 3, 1, 4).reshape(batch * seq, 256)
    return y2d, x_re.reshape(batch, g, S5_STATE), x_im.reshape(batch, g, S5_STATE)


D_FF = 2816
FFN_SPLIT = 2


def _layer_norm(x, g, b):
    mu = jnp.mean(x, axis=-1, keepdims=True)
    xc = x - mu
    var = jnp.mean(xc * xc, axis=-1, keepdims=True)
    return xc * lax.rsqrt(var + EPS) * g + b


def _post_kernel(*refs, t, decode, alpha):
    (x_ref, oa_ref, ob_ref, yc_ref, u_ref, od_ref, s5d_ref, wglu_ref, bglu_ref, wout_ref, ln1g_ref, ln1b_ref,
     wup_ref, fcw_ref, fcb_ref, wdown_ref, ln2g_ref, ln2b_ref) = refs[:18]
    if decode:
        st_ref, xo_ref, sto_ref = refs[18:]
    else:
        xo_ref, sto_ref, tail_ref = refs[18:]
        i = pl.program_id(1)

        @pl.when(i == 0)
        def _():
            tail_ref[...] = jnp.zeros_like(tail_ref)

    yc = _gelu_tanh(yc_ref[...] + s5d_ref[...] * u_ref[...])
    oc = yc * _sigmoid(_dot(yc.astype(BF16), wglu_ref[...]) + bglu_ref[...])
    mix = _dot(oa_ref[...].astype(BF16), wout_ref[0:256, :])
    mix = mix + _dot(ob_ref[...].astype(BF16), wout_ref[256:512, :])
    mix = mix + _dot(oc.astype(BF16), wout_ref[512:768, :])
    mix = mix + _dot(od_ref[...].astype(BF16), wout_ref[768:1024, :])
    x1 = _layer_norm(alpha * x_ref[...] + mix, ln1g_ref[...], ln1b_ref[...])
    x1b = x1.astype(BF16)
    ffn = None
    width = D_FF // FFN_SPLIT
    for c in range(FFN_SPLIT):
        lo, hi = c * width, (c + 1) * width
        uh = _dot(x1b, wup_ref[:, lo:hi])
        gh = _dot(x1b, wup_ref[:, D_FF + lo:D_FF + hi])
        if decode:
            s0 = st_ref[:, lo:hi]
            s1 = st_ref[:, D_FF + lo:D_FF + hi]
            gconv = fcb_ref[:, lo:hi] + fcw_ref[0:1, lo:hi] * s0 + fcw_ref[1:2, lo:hi] * s1 + fcw_ref[2:3, lo:hi] * gh
            sto_ref[:, lo:hi] = s1
            sto_ref[:, D_FF + lo:D_FF + hi] = gh
        else:
            ext = jnp.concatenate([tail_ref[:, lo:hi], gh], axis=0)
            gconv = fcb_ref[:, lo:hi]
            for j in range(3):
                gconv = gconv + ext[6 + j:6 + j + t] * fcw_ref[j:j + 1, lo:hi]
            tail_ref[:, lo:hi] = gh[t - 8:t]

            @pl.when(i == pl.num_programs(1) - 1)
            def _():
                sto_ref[0, :, lo:hi] = ext[t + 6:t + 8]

        hmid = (_gelu_tanh(gconv) * uh).astype(BF16)
        part = _dot(hmid, wdown_ref[lo:hi, :])
        ffn = part if ffn is None else ffn + part
    xo_ref[...] = _layer_norm(alpha * x1 + ffn, ln2g_ref[...], ln2b_ref[...])


def _post(x, oa, ob, yc, u, od, weights, *, alpha, batch=None, seq=None, t=None, ffn_state=None):
    decode = ffn_state is not None
    n = x.shape[0]
    if decode:
        t = n
        grid = (1,)
        row = lambda i: (0, 0)
        const = lambda a: pl.BlockSpec(a.shape, lambda i: (0,) * a.ndim, pipeline_mode=pl.Buffered(1))
        extra_in = [pl.BlockSpec(ffn_state.shape, row)]
        extra_args = [ffn_state]
        out_specs = [pl.BlockSpec((t, 1024), row), pl.BlockSpec((t, 2 * D_FF), row)]
        out_shape = [jax.ShapeDtypeStruct((n, 1024), F32), jax.ShapeDtypeStruct((n, 2 * D_FF), F32)]
        scratch = []
        sem = ("arbitrary",)
    else:
        nt = seq // t
        grid = (batch, nt)
        row = lambda b, i: (b * nt + i, 0)
        const = lambda a: pl.BlockSpec(a.shape, lambda b, i: (0,) * a.ndim, pipeline_mode=pl.Buffered(1))
        extra_in, extra_args = [], []
        out_specs = [pl.BlockSpec((t, 1024), row), pl.BlockSpec((1, 2, D_FF), lambda b, i: (b, 0, 0))]
        out_shape = [jax.ShapeDtypeStruct((n, 1024), F32), jax.ShapeDtypeStruct((batch, 2, D_FF), F32)]
        scratch = [pltpu.VMEM((8, D_FF), F32)]
        sem = ("arbitrary", "arbitrary")
    acts = (x, oa, ob, yc, u, od)
    return pl.pallas_call(
        functools.partial(_post_kernel, t=t, decode=decode, alpha=alpha),
        grid=grid,
        in_specs=[pl.BlockSpec((t, a.shape[1]), row) for a in acts] + [const(w) for w in weights] + extra_in,
        out_specs=out_specs,
        out_shape=out_shape,
        scratch_shapes=scratch,
        compiler_params=_cparams(*sem),
        name="post_decode" if decode else "post_prompt",
    )(*acts, *weights, *extra_args)


PAGE = 128
PAGES_PER_STEP = 16


def _decode_attn_kernel(pt_ref, q_ref, bias_ref, *refs, pp):
    del pt_ref
    k_refs, v_refs = refs[:pp], refs[pp:2 * pp]
    o_ref, qb_ref, c_ref, acc_ref = refs[2 * pp:]
    s = pl.program_id(1)
    eye = _iota2((256, 256), 0) == _iota2((256, 256), 1)

    @pl.when(s == 0)
    def _():
        q_col = jnp.sum(jnp.where(eye, q_ref[0], 0.0), axis=1, keepdims=True)
        qb_ref[...] = jnp.broadcast_to(q_col, (256, PAGE))
        c_ref[...] = jnp.zeros_like(c_ref)
        acc_ref[...] = jnp.zeros_like(acc_ref)

    ones_sfx = _suffix_ones(PAGE)
    qb = qb_ref[...]
    c = c_ref[...]
    acc = acc_ref[...]
    for r in reversed(range(pp)):
        prod = k_refs[r][0] * qb
        z = jnp.concatenate(
            [jnp.sum(prod[h * HEAD_DIM:(h + 1) * HEAD_DIM], axis=0, keepdims=True) for h in range(4)]
            + [jnp.zeros((4, PAGE), F32)], axis=0) + bias_ref[...]
        sp = _softplus2(z)
        rest = _dot(sp.astype(BF16), ones_sfx)
        w = jnp.exp2(z - rest - c)
        v_t = v_refs[r][0]
        acc = acc + jnp.concatenate(
            [v_t[h * HEAD_DIM:(h + 1) * HEAD_DIM] * w[h:h + 1, :] for h in range(4)], axis=0)
        c = c + rest[:, 0:1]
    c_ref[...] = c
    acc_ref[...] = acc

    @pl.when(s == pl.num_programs(1) - 1)
    def _():
        o_col = jnp.sum(acc, axis=1, keepdims=True)
        o_ref[0] = jnp.sum(jnp.where(eye, o_col, 0.0), axis=0, keepdims=True)


def _decode_attention(q, pool_kt, pool_vt, page_table, bias8, *, base):
    nseq, npages = page_table.shape
    pp = PAGES_PER_STEP
    nsteps = npages // pp

    def page_map(r):
        return lambda b, s, pt: (base + pt[b, (nsteps - 1 - s) * pp + r], 0, 0)

    page_specs = [pl.BlockSpec((1, 256, PAGE), page_map(r)) for r in range(pp)]
    out = pl.pallas_call(
        functools.partial(_decode_attn_kernel, pp=pp),
        grid_spec=pltpu.PrefetchScalarGridSpec(
            num_scalar_prefetch=1,
            grid=(nseq, nsteps),
            in_specs=[pl.BlockSpec((1, 1, 256), lambda b, s, pt: (b, 0, 0)),
                      pl.BlockSpec((8, PAGE), lambda b, s, pt: (0, 0))] + page_specs + page_specs,
            out_specs=pl.BlockSpec((1, 1, 256), lambda b, s, pt: (b, 0, 0)),
            scratch_shapes=[pltpu.VMEM((256, PAGE), F32), pltpu.VMEM((8, PAGE), F32), pltpu.VMEM((256, PAGE), F32)],
        ),
        out_shape=jax.ShapeDtypeStruct((nseq, 1, 256), F32),
        compiler_params=_cparams("arbitrary", "arbitrary"),
        name="decode_attention",
    )(page_table, q.reshape(nseq, 1, 256), bias8, *([pool_kt] * pp), *([pool_vt] * pp))
    return out.reshape(nseq, 256)


def _dot_f32(a, b):
    a0, a1, a2 = _split3(a)
    b0, b1, b2 = _split3(b)
    return (_dot(a0, b0) + (_dot(a0, b1) + _dot(a1, b0))
            + (_dot(a0, b2) + _dot(a2, b0) + _dot(a1, b1)))


_R_DEC, _R_XDT, _R_BM, _R_CM, _R_F, _R_K, _R_Q, _R_V, _R_END = 0, 256, 512, 640, 768, 1024, 1280, 1536, 1792


def _decode_mixers_kernel(zx_ref, u_ref, hg_ref, cs_ref, hssm_ref, x0re_ref, x0im_ref, hhg_ref,
                          cw_ref, cb_ref, dtb_ref, alog_ref, d_ref, ngs_ref, lb_ref, ngh_ref,
                          lre_ref, lim_ref, bre_ref, bim_ref, cre_ref, cim_ref,
                          ob_ref, cso_ref, hssmo_ref, yc_ref, xre_ref, xim_ref, od_ref, hhgo_ref,
                          rows_ref, yssm_ref, yhg_ref, *, nseq):
    z = zx_ref[:, 0:256]
    xbc = zx_ref[:, 256:768]
    s0, s1, s2 = cs_ref[:, 0:512], cs_ref[:, 512:1024], cs_ref[:, 1024:1536]
    conv = cb_ref[...] + cw_ref[0:1, :] * s0 + cw_ref[1:2, :] * s1 + cw_ref[2:3, :] * s2 + cw_ref[3:4, :] * xbc
    cso_ref[:, 0:512] = s1
    cso_ref[:, 512:1024] = s2
    cso_ref[:, 1024:1536] = xbc
    act = _silu(conv)
    xs = act[:, 0:256]
    dt = _softplus(zx_ref[:, 768:1024] + dtb_ref[...])
    rows_ref[:, _R_DEC:_R_XDT] = jnp.exp(dt * (-jnp.exp(alog_ref[...])))
    rows_ref[:, _R_XDT:_R_BM] = xs * dt
    rows_ref[:, _R_BM:_R_F] = act[:, 256:512]
    lb = lb_ref[...]
    fr = hg_ref[:, 256:512]
    rows_ref[:, _R_F:_R_K] = lb + (1.0 - lb) * _sigmoid(fr)
    rows_ref[:, _R_K:_R_Q] = (1.0 - lb) * _sigmoid(-fr)
    rows_ref[:, _R_Q:_R_V] = _silu(hg_ref[:, 0:256])
    rows_ref[:, _R_V:_R_END] = hg_ref[:, 512:768]

    eye = _iota2((256, 256), 0) == _iota2((256, 256), 1)
    row_id = _iota2((256, 64), 0)

    def to_col(r):
        return jnp.sum(jnp.where(eye, r, 0.0), axis=1, keepdims=True)

    def to_row(c):
        return jnp.sum(jnp.where(eye, c, 0.0), axis=0, keepdims=True)

    def per_seq(b, get):
        bm = get(_R_BM, _R_CM)
        cm = get(_R_CM, _R_F)
        bm_rows = jnp.where(row_id < 128, bm[:, 0:64], bm[:, 64:128])
        cm_rows = jnp.where(row_id < 128, cm[:, 0:64], cm[:, 64:128])
        hn = hssm_ref[b] * to_col(get(_R_DEC, _R_XDT)) + to_col(get(_R_XDT, _R_BM)) * bm_rows
        hssmo_ref[b] = hn
        y_ssm = to_row(jnp.sum(hn * cm_rows, axis=1, keepdims=True))
        v = get(_R_V, _R_END)
        v_rows = jnp.where(row_id < 64, v[:, 0:64],
                           jnp.where(row_id < 128, v[:, 64:128],
                                     jnp.where(row_id < 192, v[:, 128:192], v[:, 192:256])))
        gn = hhg_ref[b] * to_col(get(_R_F, _R_K)) + to_col(get(_R_K, _R_Q)) * v_rows
        hhgo_ref[b] = gn
        qg = to_col(get(_R_Q, _R_V)) * gn
        y_hg = jnp.concatenate(
            [jnp.sum(qg[h * 64:(h + 1) * 64], axis=0, keepdims=True) for h in range(4)], axis=1)
        return y_ssm, y_hg

    def per_octet(o, carry):
        base = pl.multiple_of(o * 8, 8)
        blk = rows_ref[pl.ds(base, 8), :]
        ys = [per_seq(base + r, lambda lo, hi, r=r: blk[r:r + 1, lo:hi]) for r in range(8)]
        yssm_ref[pl.ds(base, 8), :] = jnp.concatenate([y[0] for y in ys], axis=0)
        yhg_ref[pl.ds(base, 8), :] = jnp.concatenate([y[1] for y in ys], axis=0)
        return carry

    lax.fori_loop(0, nseq // 8, per_octet, 0)

    y = yssm_ref[...] + d_ref[...] * xs
    ob_ref[...] = _group_rms(y * _silu(z), ngs_ref[...], 128)
    od_ref[...] = _head_rms(yhg_ref[...], ngh_ref[...]) * _silu(hg_ref[:, 768:1024])
    u = u_ref[...]
    x0r, x0i = x0re_ref[...], x0im_ref[...]
    lr, li = lre_ref[...], lim_ref[...]
    xr = lr * x0r - li * x0i + _dot_f32(u, bre_ref[...])
    xi = lr * x0i + li * x0r + _dot_f32(u, bim_ref[...])
    xre_ref[...] = xr
    xim_ref[...] = xi
    yc_ref[...] = _dot_f32(xr, cre_ref[...]) - _dot_f32(xi, cim_ref[...])


def _decode_mixers(zx, u, hg4, conv_state, h_ssm, x0_re, x0_im, h_hg, params):
    nseq = zx.shape[0]
    args = (zx, u, hg4, conv_state, h_ssm, x0_re, x0_im, h_hg, *params)
    full = lambda a: pl.BlockSpec(a.shape, lambda i: (0,) * a.ndim)
    out_shape = [
        jax.ShapeDtypeStruct((nseq, 256), F32),
        jax.ShapeDtypeStruct((nseq, 1536), F32),
        jax.ShapeDtypeStruct((nseq, 256, 64), F32),
        jax.ShapeDtypeStruct((nseq, 256), F32),
        jax.ShapeDtypeStruct((nseq, 1024), F32),
        jax.ShapeDtypeStruct((nseq, 1024), F32),
        jax.ShapeDtypeStruct((nseq, 256), F32),
        jax.ShapeDtypeStruct((nseq, 256, 64), F32),
    ]
    return pl.pallas_call(
        functools.partial(_decode_mixers_kernel, nseq=nseq),
        grid=(1,),
        in_specs=[full(a) for a in args],
        out_specs=[full(s) for s in out_shape],
        out_shape=out_shape,
        scratch_shapes=[pltpu.VMEM((nseq, _R_END), F32), pltpu.VMEM((nseq, 256), F32), pltpu.VMEM((nseq, 256), F32)],
        compiler_params=_cparams("arbitrary"),
        name="decode_mixers",
    )(*args)


def _s5_decode_matrices(log_mag, arg, bb_re, bb_im, c_re, c_im):
    g, n = S5_GROUPS, S5_STATE
    eye = jnp.eye(g, dtype=F32)
    lam_re, lam_im = _lam_pow(log_mag, arg, 1.0)
    expand_b = lambda bb: jnp.einsum("gnk,gj->gkjn", bb, eye).reshape(g * 16, g * n)
    expand_c = lambda cc: jnp.einsum("ghn,gj->gnjh", cc, eye).reshape(g * n, g * 16)
    return (lam_re.reshape(1, g * n), lam_im.reshape(1, g * n),
            expand_b(bb_re), expand_b(bb_im), expand_c(c_re), expand_c(c_im))


ATTN_TQ = 1024
ATTN_TK = 256
PROJ_TILE = 512
SSD_TILE = 256
HGRN_TILE = 256
POST_TILE = 256


def _rearranged_w_in(w_in):
    q, k, v, z, xbc, dt, u, hq, hf, hi, hg = jnp.split(
        w_in, [256, 512, 768, 1024, 1536, 1540, 1796, 2052, 2308, 2564], axis=1)
    dt_full = jnp.repeat(dt, HEAD_DIM, axis=1)
    q_scale = LOG2E * HEAD_DIM ** -0.5
    return jnp.concatenate([q * q_scale, k, v, z, xbc, dt_full, u, hq, hf, hi, hg], axis=1).astype(BF16)


def kernel(x_prompt, x_sample, cache_k, cache_v, state_ssm_conv, state_ssm, state_s5_re, state_s5_im, state_hgrn, state_ffn_conv, page_table, ln1_g, ln1_b, ln2_g, ln2_b, w_in, w_out, sb_logit_bias, ssm_conv_w, ssm_conv_b, ssm_dt_bias, ssm_a_log, ssm_d, ssm_norm_g, s5_a_re, s5_a_im, s5_b_re, s5_b_im, s5_c_re, s5_c_im, s5_d, s5_log_dt, s5_w_glu, s5_b_glu, hg_lb_logits, hg_norm_g, w_up, ffn_conv_w, ffn_conv_b, w_down):
    depth = w_in.shape[0]
    bp, seq, dm = x_prompt.shape
    ns = x_sample.shape[0]
    n_phys = cache_k.shape[1]
    alpha = (2 * depth) ** 0.25
    row = lambda a: a.reshape(1, -1)
    rep = lambda a: jnp.repeat(a, HEAD_DIM).reshape(1, -1)

    pr = jax.nn.softmax(hg_lb_logits.astype(F32), axis=0)
    lbs = jnp.cumsum(pr, axis=0) - pr[0:1]
    pool_kt = cache_k.transpose(0, 1, 3, 4, 2).reshape(depth * n_phys, 256, PAGE)
    pool_vt = cache_v.transpose(0, 1, 3, 4, 2).reshape(depth * n_phys, 256, PAGE)

    xp = x_prompt.reshape(bp * seq, dm)
    xs = x_sample.reshape(ns, dm)
    k_all = jnp.zeros((depth, bp, 256, seq), F32)
    v_all = jnp.zeros((depth, bp, 256, seq), F32)
    outs_p, outs_s = [], []
    for l in range(depth):
        w_p = _rearranged_w_in(w_in[l])
        disc = _s5_discretize(s5_a_re[l], s5_a_im[l], s5_b_re[l], s5_b_im[l], s5_log_dt[l])
        ssd_w = (ssm_conv_w[l], row(ssm_conv_b[l]), rep(ssm_dt_bias[l]), rep(ssm_a_log[l]), rep(ssm_d[l]),
                 row(ssm_norm_g[l]))
        post_w = (row(s5_d[l]), s5_w_glu[l].astype(BF16), row(s5_b_glu[l]), w_out[l].astype(BF16),
                  row(ln1_g[l]), row(ln1_b[l]), w_up[l].astype(BF16), ffn_conv_w[l], row(ffn_conv_b[l]),
                  w_down[l].astype(BF16), row(ln2_g[l]), row(ln2_b[l]))
        lb = row(lbs[l])
        ngh = row(hg_norm_g[l])

        q, vb, zx, u, hg4, kt, k_all, v_all = _projection(
            xp, w_p, tm=PROJ_TILE, tk=ATTN_TK, seq=seq, layer=l, k_all=k_all, v_all=v_all)
        oa = _prompt_attention(q, kt, vb, sb_logit_bias[l], batch=bp, seq=seq, tq=ATTN_TQ, tk=ATTN_TK)
        ob, conv_p, ssm_p = _prompt_ssd(zx, *ssd_w, batch=bp, seq=seq, t=SSD_TILE)
        yc, re_p, im_p = _s5_prompt_scan(u, disc, s5_c_re[l], s5_c_im[l], batch=bp, seq=seq)
        od, hg_p = _prompt_hgrn(hg4, lb, ngh, batch=bp, seq=seq, t=HGRN_TILE)
        xp, ffn_p = _post(xp, oa, ob, yc, u, od, post_w, alpha=alpha, batch=bp, seq=seq, t=POST_TILE)
        outs_p.append((conv_p, ssm_p, re_p, im_p, hg_p, ffn_p))

        q, k, v, zx, u, hg4 = _projection(xs, w_p, tm=ns)
        bias8 = jnp.broadcast_to(jnp.pad(sb_logit_bias[l] * LOG2E, (0, 4))[:, None], (8, PAGE))
        oa = _decode_attention(q, pool_kt, pool_vt, page_table, bias8, base=l * n_phys)
        dec_w = (*ssd_w, lb, ngh, *_s5_decode_matrices(*disc, s5_c_re[l], s5_c_im[l]))
        ob, conv_s, ssm_s, yc, re_s, im_s, od, hg_s = _decode_mixers(
            zx, u, hg4, state_ssm_conv[l].reshape(ns, 1536), state_ssm[l].reshape(ns, 256, 64),
            state_s5_re[l].reshape(ns, 1024), state_s5_im[l].reshape(ns, 1024),
            state_hgrn[l].reshape(ns, 256, 64), dec_w)
        xs, ffn_s = _post(xs, oa, ob, yc, u, od, post_w, alpha=alpha,
                          ffn_state=state_ffn_conv[l].reshape(ns, 2 * D_FF))
        outs_s.append((k.reshape(ns, 1, 4, HEAD_DIM), v.reshape(ns, 1, 4, HEAD_DIM),
                       conv_s.reshape(ns, 3, 512), ssm_s.reshape(ns, 4, 64, 64),
                       re_s.reshape(ns, S5_GROUPS, S5_STATE), im_s.reshape(ns, S5_GROUPS, S5_STATE),
                       hg_s.reshape(ns, 4, 64, 64), ffn_s.reshape(ns, 2, D_FF)))

    sp = [jnp.stack(col, axis=0) for col in zip(*outs_p)]
    ss = [jnp.stack(col, axis=0) for col in zip(*outs_s)]
    k_prompt = k_all.reshape(depth, bp, 4, HEAD_DIM, seq).transpose(0, 1, 4, 2, 3)
    v_prompt = v_all.reshape(depth, bp, 4, HEAD_DIM, seq).transpose(0, 1, 4, 2, 3)
    return (xp.reshape(bp, seq, dm), xs.reshape(ns, 1, dm), k_prompt, v_prompt, ss[0], ss[1], sp[0], ss[2],
            sp[1], ss[3], sp[2], ss[4], sp[3], ss[5], sp[4], ss[6], sp[5], ss[7])
```

```python
import functools
import math

import jax
import jax.numpy as jnp
from jax import lax
from jax.experimental import pallas as pl
from jax.experimental.pallas import tpu as pltpu

F32 = jnp.float32
BF16 = jnp.bfloat16

HEAD_DIM = 64
GROUP_WIDTH = 256
EPS = 1e-5
NEG_BIG = -1e30

VMEM_LIMIT_BYTES = 56 * 1024 * 1024


def _cparams(*sem):
    return pltpu.CompilerParams(dimension_semantics=sem, vmem_limit_bytes=VMEM_LIMIT_BYTES)


def _dot(a, b):
    return jnp.dot(a, b, preferred_element_type=F32)


def _dot_nt(a, b):
    return lax.dot_general(a, b, (((1,), (1,)), ((), ())), preferred_element_type=F32)


def _dot_tn(a, b):
    return lax.dot_general(a, b, (((0,), (0,)), ((), ())), preferred_element_type=F32)


def _split2(x):
    hi = x.astype(BF16)
    lo = (x - hi.astype(F32)).astype(BF16)
    return hi, lo


def _split3(x):
    hi = x.astype(BF16)
    r = x - hi.astype(F32)
    mid = r.astype(BF16)
    lo = (r - mid.astype(F32)).astype(BF16)
    return hi, mid, lo


def _mm01_left(m01, x, passes):
    parts = _split3(x) if passes == 3 else _split2(x)
    acc = _dot(m01, parts[0])
    for p in parts[1:]:
        acc = acc + _dot(m01, p)
    return acc


def _mm01_right(x, m01, passes):
    parts = _split3(x) if passes == 3 else _split2(x)
    acc = _dot(parts[0], m01)
    for p in parts[1:]:
        acc = acc + _dot(p, m01)
    return acc


def _sigmoid(x):
    return 1.0 / (1.0 + jnp.exp(-x))


def _silu(x):
    return x * _sigmoid(x)


def _softplus(x):
    return jnp.maximum(x, 0.0) + jnp.log(1.0 + jnp.exp(-jnp.abs(x)))


def _log_sigmoid_neg(z):
    return jnp.minimum(-z, 0.0) - jnp.log(1.0 + jnp.exp(-jnp.abs(z)))


def _gelu_tanh(x):
    c = math.sqrt(2.0 / math.pi)
    return 0.5 * x * (1.0 + jnp.tanh(c * (x + 0.044715 * (x * x * x))))


def _iota2(shape, dim):
    return lax.broadcasted_iota(jnp.int32, shape, dim)


PROJ_W = 3072


def _proj_kernel(x_ref, w_ref, *refs, tk):
    x = x_ref[...].astype(BF16)

    def mm(lo, hi):
        return _dot(x, w_ref[:, lo:hi])

    q = mm(0, 256)
    k = mm(256, 512)
    v = mm(512, 768)
    u = mm(1792, 2048)
    if tk:
        _, _, q_ref, vb_ref, zx_ref, u_ref, hg_ref, kt_ref, kall_ref, vall_ref, ub_ref = refs
        ub_ref[...] = u.astype(BF16)
        q_ref[...] = q.astype(BF16)
        vb_ref[...] = v.astype(BF16)
        k_t = k.T
        kall_ref[0, 0] = k_t
        vall_ref[0, 0] = v.T
        for c in range(k.shape[0] // tk):
            kt_ref[c] = k_t[:, c * tk:(c + 1) * tk].astype(BF16)
    else:
        q_ref, k_ref, v_ref, zx_ref, u_ref, hg_ref = refs
        q_ref[...] = q
        k_ref[...] = k
        v_ref[...] = v
    zx_ref[...] = mm(768, 1792)
    u_ref[...] = u
    hg_ref[...] = mm(2048, 3072)


def _projection(x2d, w_p, *, tm, tk=0, seq=None, layer=None, k_all=None, v_all=None):
    n, d = x2d.shape
    row = lambda i: (i, 0)
    rows = lambda width, dtype: (jax.ShapeDtypeStruct((n, width), dtype), pl.BlockSpec((tm, width), row))
    in_specs = [pl.BlockSpec((tm, d), row), pl.BlockSpec((d, PROJ_W), lambda i: (0, 0))]
    args = [x2d, w_p]
    aliases = {}
    if tk:
        nt = seq // tm
        slab = pl.BlockSpec((1, 1, 256, tm), lambda i: (layer, i // nt, 0, i % nt))
        outs = [rows(256, BF16), rows(256, BF16), rows(1024, F32), rows(256, F32), rows(1024, F32),
                (jax.ShapeDtypeStruct((n // tk, 256, tk), BF16), pl.BlockSpec((tm // tk, 256, tk), lambda i: (i, 0, 0))),
                (jax.ShapeDtypeStruct(k_all.shape, F32), slab), (jax.ShapeDtypeStruct(v_all.shape, F32), slab),
                rows(256, BF16)]
        in_specs += [pl.BlockSpec(memory_space=pl.ANY), pl.BlockSpec(memory_space=pl.ANY)]
        args += [k_all, v_all]
        aliases = {2: 6, 3: 7}
    else:
        outs = [rows(256, F32), rows(256, F32), rows(256, F32), rows(1024, F32), rows(256, F32), rows(1024, F32)]
    return pl.pallas_call(
        functools.partial(_proj_kernel, tk=tk),
        grid=(n // tm,),
        in_specs=in_specs,
        out_specs=[o[1] for o in outs],
        out_shape=[o[0] for o in outs],
        input_output_aliases=aliases,
        compiler_params=_cparams("arbitrary"),
        name="projection",
    )(*args)


LOG2E = 1.4426950408889634


def _suffix_ones(n):
    return jnp.where(_iota2((n, n), 0) >= _iota2((n, n), 1), 1.0, 0.0).astype(BF16)


def _neg_abs(z):
    return pltpu.bitcast(pltpu.bitcast(z, jnp.uint32) | jnp.uint32(0x80000000), F32)


def _softplus2(z):
    return jnp.maximum(z, 0.0) + jnp.log2(1.0 + jnp.exp2(_neg_abs(z)))


def _sb_block(qs, kt_pair, v_pair, bias_col, ones_sfx, carry, mask):
    z = _dot(qs, kt_pair) + bias_col
    sp = _softplus2(z)
    if mask is not None:
        sp = jnp.where(mask, sp, 0.0)
    rest = _dot(sp.astype(BF16), ones_sfx)
    w = jnp.exp2(z - rest - carry)
    if mask is not None:
        w = jnp.where(mask, w, 0.0)
    return _dot(w.astype(BF16), v_pair), rest[:, 0:1]


def _attn_kernel(bias_ref, q_ref, kt_ref, v_ref, o_ref, qs_ref, acc_ref, *, tq, tk):
    hp = pl.program_id(1)
    i = pl.program_id(2)
    ones_sfx = _suffix_ones(tk)
    r = tq // tk
    lane_head = _iota2((tq, 128), 1) // HEAD_DIM
    q = q_ref[...]
    qs_ref[0:tq, :] = jnp.where(lane_head == 0, q, jnp.zeros_like(q))
    qs_ref[tq:2 * tq, :] = jnp.where(lane_head == 1, q, jnp.zeros_like(q))
    bias0 = bias_ref[2 * hp] * LOG2E
    bias1 = bias_ref[2 * hp + 1] * LOG2E

    def kv(j):
        start = pl.multiple_of(j * tk, tk)
        return kt_ref[j], v_ref[pl.ds(start, tk), :]

    acc = jnp.zeros((2 * tq, 128), F32)
    tot = jnp.zeros((2 * tq, 1), F32)
    for d in reversed(range(r)):
        lo = d * tk
        nd = tq - lo
        qs_d = jnp.concatenate([qs_ref[lo:tq, :], qs_ref[tq + lo:2 * tq, :]], axis=0)
        tot_d = jnp.concatenate([tot[lo:tq], tot[tq + lo:2 * tq]], axis=0)
        bias_d = jnp.where(_iota2((2 * nd, 1), 0) < nd, bias0, bias1)
        mask = _iota2((2 * nd, tk), 1) < _iota2((2 * nd, tk), 0) % nd
        kt_p, v_p = kv(i * r + d)
        da, dt = _sb_block(qs_d, kt_p, v_p, bias_d, ones_sfx, tot_d, mask)
        pieces_a = [da[:nd], da[nd:]]
        pieces_t = [dt[:nd], dt[nd:]]
        if lo:
            pieces_a = [jnp.zeros((lo, 128), F32), da[:nd], jnp.zeros((lo, 128), F32), da[nd:]]
            pieces_t = [jnp.zeros((lo, 1), F32), dt[:nd], jnp.zeros((lo, 1), F32), dt[nd:]]
        acc = acc + jnp.concatenate(pieces_a, axis=0)
        tot = tot + jnp.concatenate(pieces_t, axis=0)
    acc_ref[...] = acc
    bias_col = jnp.where(_iota2((2 * tq, 1), 0) < tq, bias0, bias1)

    def body(it, tot):
        kt_p, v_p = kv(i * r - 1 - it)
        da, dt = _sb_block(qs_ref[...], kt_p, v_p, bias_col, ones_sfx, tot, None)
        acc_ref[...] += da
        return tot + dt

    lax.fori_loop(0, i * r, body, tot)
    o_ref[...] = jnp.where(lane_head == 0, acc_ref[0:tq, :], acc_ref[tq:2 * tq, :])


def _prompt_attention(q, kt, vb, bias, *, batch, seq, tq, tk):
    n = q.shape[0]
    nq = seq // tq
    return pl.pallas_call(
        functools.partial(_attn_kernel, tq=tq, tk=tk),
        grid=(batch, 2, nq),
        in_specs=[
            pl.BlockSpec(memory_space=pltpu.SMEM),
            pl.BlockSpec((tq, 128), lambda b, hp, i: (b * nq + i, hp)),
            pl.BlockSpec((seq // tk, 128, tk), lambda b, hp, i: (b, hp, 0)),
            pl.BlockSpec((seq, 128), lambda b, hp, i: (b, hp)),
        ],
        out_specs=pl.BlockSpec((tq, 128), lambda b, hp, i: (b * nq + i, hp)),
        out_shape=jax.ShapeDtypeStruct((n, 256), F32),
        scratch_shapes=[pltpu.VMEM((2 * tq, 128), BF16), pltpu.VMEM((2 * tq, 128), F32)],
        compiler_params=_cparams("arbitrary", "arbitrary", "arbitrary"),
        name="prompt_attention",
    )(bias, q, kt, vb)


def _lower_incl(n):
    return jnp.where(_iota2((n, n), 1) <= _iota2((n, n), 0), 1.0, 0.0).astype(BF16)


def _group_rms(x, gain, width):
    outs = []
    for g in range(x.shape[1] // width):
        xg = x[:, g * width:(g + 1) * width]
        ms = jnp.mean(xg * xg, axis=1, keepdims=True)
        outs.append(xg * lax.rsqrt(ms + EPS))
    return jnp.concatenate(outs, axis=1) * gain


def _ssd_kernel(zx_ref, cw_ref, cb_ref, dtb_ref, alog_ref, d_ref, ng_ref,
                o_ref, conv_ref, h_ref, tail_ref, hs_ref, *, t):
    i = pl.program_id(1)

    @pl.when(i == 0)
    def _():
        tail_ref[...] = jnp.zeros_like(tail_ref)
        hs_ref[...] = jnp.zeros_like(hs_ref)

    z = zx_ref[:, 0:256]
    xbc = zx_ref[:, 256:768]
    ext = jnp.concatenate([tail_ref[...], xbc], axis=0)
    conv = cb_ref[...]
    for j in range(4):
        conv = conv + ext[5 + j:5 + j + t] * cw_ref[j:j + 1, :]
    tail_ref[...] = xbc[t - 8:t]
    act = _silu(conv)
    xs, bm, cm = act[:, 0:256], act[:, 256:384], act[:, 384:512]
    dt = _softplus(zx_ref[:, 768:1024] + dtb_ref[...])
    da = dt * (-jnp.exp(alog_ref[...]))
    cum = _mm01_left(_lower_incl(t), da, 3)
    cum_t = cum.T
    xdt = xs * dt
    causal = _iota2((t, t), 1) <= _iota2((t, t), 0)
    bmb = bm.astype(BF16)
    cmb = cm.astype(BF16)
    gmat = [_dot_nt(cmb[:, g * 64:(g + 1) * 64], bmb[:, g * 64:(g + 1) * 64]) for g in range(2)]
    ys = []
    for h in range(4):
        hs = slice(h * 64, (h + 1) * 64)
        gs = slice((h // 2) * 64, (h // 2 + 1) * 64)
        cum_h = cum[:, hs]
        seg = cum[:, h * 64:h * 64 + 1] - cum_t[h * 64:h * 64 + 1, :]
        decay = jnp.exp(jnp.where(causal, seg, NEG_BIG))
        y_intra = _dot((gmat[h // 2] * decay).astype(BF16), xdt[:, hs].astype(BF16))
        last = cum_h[t - 1:t, :]
        h_prev = hs_ref[h]
        y_inter = _dot_nt((cm[:, gs] * jnp.exp(cum_h)).astype(BF16), h_prev.astype(BF16))
        s_c = _dot_tn((xdt[:, hs] * jnp.exp(last - cum_h)).astype(BF16), bmb[:, gs])
        hs_ref[h] = h_prev * jnp.exp(last) + s_c
        ys.append(y_intra + y_inter)
    y = jnp.concatenate(ys, axis=1) + d_ref[...] * xs
    o_ref[...] = _group_rms(y * _silu(z), ng_ref[...], 128)

    @pl.when(i == pl.num_programs(1) - 1)
    def _():
        conv_ref[0] = ext[t + 5:t + 8]
        h_ref[0] = hs_ref[...]


def _prompt_ssd(zx, cw, cb, dtb, alog, d_full, ng, *, batch, seq, t):
    n = zx.shape[0]
    nt = seq // t
    full = lambda a: pl.BlockSpec(a.shape, lambda b, i: (0,) * a.ndim)
    return pl.pallas_call(
        functools.partial(_ssd_kernel, t=t),
        grid=(batch, nt),
        in_specs=[pl.BlockSpec((t, 1024), lambda b, i: (b * nt + i, 0))] + [full(a) for a in (cw, cb, dtb, alog, d_full, ng)],
        out_specs=[
            pl.BlockSpec((t, 256), lambda b, i: (b * nt + i, 0)),
            pl.BlockSpec((1, 3, 512), lambda b, i: (b, 0, 0)),
            pl.BlockSpec((1, 4, 64, 64), lambda b, i: (b, 0, 0, 0)),
        ],
        out_shape=[
            jax.ShapeDtypeStruct((n, 256), F32),
            jax.ShapeDtypeStruct((batch, 3, 512), F32),
            jax.ShapeDtypeStruct((batch, 4, 64, 64), F32),
        ],
        scratch_shapes=[pltpu.VMEM((8, 512), F32), pltpu.VMEM((4, 64, 64), F32)],
        compiler_params=_cparams("arbitrary", "arbitrary"),
        name="prompt_ssd",
    )(zx, cw, cb, dtb, alog, d_full, ng)


HG_CHUNK = 32


def _head_mean_matrix():
    same = (_iota2((256, 256), 0) // 64) == (_iota2((256, 256), 1) // 64)
    return jnp.where(same, 1.0, 0.0).astype(BF16)


def _head_rms(y, gain):
    ms = _mm01_right(y * y, _head_mean_matrix(), 2) * (1.0 / 64.0)
    return y * lax.rsqrt(ms + EPS) * gain


def _hgrn_kernel(hg_ref, lb_ref, ng_ref, o_ref, h_ref, st_ref, *, t):
    i = pl.program_id(1)
    c = HG_CHUNK

    @pl.when(i == 0)
    def _():
        st_ref[...] = jnp.zeros_like(st_ref)

    lb = lb_ref[...]
    q = _silu(hg_ref[:, 0:256])
    fr = hg_ref[:, 256:512]
    log_f = jnp.log(lb + (1.0 - lb) * _sigmoid(fr))
    k = (1.0 - lb) * _sigmoid(-fr)
    v = hg_ref[:, 512:768]
    same_chunk = (_iota2((t, t), 0) // c) == (_iota2((t, t), 1) // c)
    lmat = jnp.where(same_chunk & (_iota2((t, t), 1) <= _iota2((t, t), 0)), 1.0, 0.0).astype(BF16)
    b_all = _mm01_left(lmat, log_f, 3)
    lane_head = _iota2((1, 256), 1) // 64
    head_lane = [lane_head == h for h in range(4)]
    stack_causal = _iota2((4 * c, c), 1) <= (_iota2((4 * c, c), 0) % c)
    block_diag = (_iota2((256, 256), 0) // 64) == (_iota2((256, 256), 1) // 64)
    ys = []
    for n in range(t // c):
        rows = slice(n * c, (n + 1) * c)
        b = b_all[rows]
        ref = b[c // 2 - 1:c // 2]
        last = b[c - 1:c]
        q_c, k_c, v_c = q[rows], k[rows], v[rows]
        qe = q_c * jnp.exp(b - ref)
        ke = (k_c * jnp.exp(ref - b)).astype(BF16)
        kl = (k_c * jnp.exp(last - b)).astype(BF16)
        qb = (q_c * jnp.exp(b)).astype(BF16)
        vb = v_c.astype(BF16)
        q_stack = jnp.concatenate([jnp.where(m, qe, 0.0) for m in head_lane], axis=0).astype(BF16)
        scores = jnp.where(stack_causal, _dot_nt(q_stack, ke), 0.0)
        y4 = _dot(scores.astype(BF16), vb)
        y_intra = jnp.where(head_lane[0], y4[0:c], 0.0)
        for h in range(1, 4):
            y_intra = y_intra + jnp.where(head_lane[h], y4[h * c:(h + 1) * c], 0.0)
        st = st_ref[...]
        y_inter = _dot_nt(qb, st.astype(BF16))
        st_ref[...] = st * jnp.exp(last) + jnp.where(block_diag, _dot_tn(vb, kl), 0.0)
        ys.append(y_intra + y_inter)
    y = jnp.concatenate(ys, axis=0)
    o_ref[...] = _head_rms(y, ng_ref[...]) * _silu(hg_ref[:, 768:1024])

    @pl.when(i == pl.num_programs(1) - 1)
    def _():
        st_t = st_ref[...].T
        for h in range(4):
            h_ref[0, h] = st_t[h * 64:(h + 1) * 64, h * 64:(h + 1) * 64]


def _prompt_hgrn(hg4, lb, ng, *, batch, seq, t):
    n = hg4.shape[0]
    nt = seq // t
    full = lambda a: pl.BlockSpec(a.shape, lambda b, i: (0,) * a.ndim)
    return pl.pallas_call(
        functools.partial(_hgrn_kernel, t=t),
        grid=(batch, nt),
        in_specs=[pl.BlockSpec((t, 1024), lambda b, i: (b * nt + i, 0)), full(lb), full(ng)],
        out_specs=[
            pl.BlockSpec((t, 256), lambda b, i: (b * nt + i, 0)),
            pl.BlockSpec((1, 4, 64, 64), lambda b, i: (b, 0, 0, 0)),
        ],
        out_shape=[jax.ShapeDtypeStruct((n, 256), F32), jax.ShapeDtypeStruct((batch, 4, 64, 64), F32)],
        scratch_shapes=[pltpu.VMEM((256, 256), F32)],
        compiler_params=_cparams("arbitrary", "arbitrary"),
        name="prompt_hgrn",
    )(hg4, lb, ng)


S5_ROW = 16
S5_GROUPS = 16
S5_STATE = 64


def _s5_discretize(a_re, a_im, b_re, b_im, log_dt):
    dt = jnp.exp(log_dt)[:, None]
    mag = jnp.exp(a_re * dt)
    ab_re = mag * jnp.cos(a_im * dt)
    ab_im = mag * jnp.sin(a_im * dt)
    den = a_re * a_re + a_im * a_im
    coef_re = ((ab_re - 1.0) * a_re + ab_im * a_im) / den
    coef_im = (ab_im * a_re - (ab_re - 1.0) * a_im) / den
    bb_re = coef_re[..., None] * b_re - coef_im[..., None] * b_im
    bb_im = coef_re[..., None] * b_im + coef_im[..., None] * b_re
    return a_re * dt, a_im * dt, bb_re, bb_im


def _lam_pow(log_mag, arg, m):
    mag = jnp.exp(m * log_mag)
    return mag * jnp.cos(m * arg), mag * jnp.sin(m * arg)


def _s5_prompt_matrices(log_mag, arg, bb_re, bb_im, c_re, c_im):
    r = S5_ROW
    hp = lax.Precision.HIGHEST
    taus = jnp.arange(r + 1, dtype=F32)[:, None, None]
    pw_re, pw_im = _lam_pow(log_mag[None], arg[None], taus)
    p_re = pw_re[..., None] * bb_re[None] - pw_im[..., None] * bb_im[None]
    p_im = pw_re[..., None] * bb_im[None] + pw_im[..., None] * bb_re[None]
    kern = (jnp.einsum("ghn,tgnk->gthk", c_re, p_re[:r], precision=hp)
            - jnp.einsum("ghn,tgnk->gthk", c_im, p_im[:r], precision=hp))
    s_idx = jnp.arange(r)[:, None]
    t_idx = jnp.arange(r)[None, :]
    tau = jnp.clip(t_idx - s_idx, 0, r - 1)
    m = kern[:, tau]
    m = jnp.where((t_idx >= s_idx)[None, :, :, None, None], m, 0.0)
    m_intra = jnp.transpose(m, (0, 1, 4, 2, 3)).reshape(S5_GROUPS, r * 16, r * 16)
    bst_re = jnp.transpose(p_re[:r][::-1], (1, 0, 3, 2)).reshape(S5_GROUPS, r * 16, S5_STATE)
    bst_im = jnp.transpose(p_im[:r][::-1], (1, 0, 3, 2)).reshape(S5_GROUPS, r * 16, S5_STATE)
    cl_re = c_re[None] * pw_re[1:, :, None, :] - c_im[None] * pw_im[1:, :, None, :]
    cl_im = c_re[None] * pw_im[1:, :, None, :] + c_im[None] * pw_re[1:, :, None, :]
    cin_re = jnp.transpose(cl_re, (1, 3, 0, 2)).reshape(S5_GROUPS, S5_STATE, r * 16)
    cin_im = -jnp.transpose(cl_im, (1, 3, 0, 2)).reshape(S5_GROUPS, S5_STATE, r * 16)
    return m_intra.astype(BF16), bst_re.astype(BF16), bst_im.astype(BF16), cin_re.astype(BF16), cin_im.astype(BF16)


def _shift_rows(x, d):
    return jnp.concatenate([jnp.zeros((d, x.shape[1]), x.dtype), x[:x.shape[0] - d]], axis=0)


def _s5_kernel(u_ref, m_ref, bre_ref, bim_ref, cre_ref, cim_ref, pre_ref, pim_ref,
               y_ref, xre_ref, xim_ref, *, rows):
    u = u_ref[0]
    x_re = _dot(u, bre_ref[0])
    x_im = _dot(u, bim_ref[0])
    d, step = 1, 0
    while d < rows:
        l_re = pre_ref[0, step:step + 1, :]
        l_im = pim_ref[0, step:step + 1, :]
        s_re, s_im = _shift_rows(x_re, d), _shift_rows(x_im, d)
        x_re, x_im = x_re + l_re * s_re - l_im * s_im, x_im + l_re * s_im + l_im * s_re
        d, step = 2 * d, step + 1
    p_re, p_im = _shift_rows(x_re, 1), _shift_rows(x_im, 1)
    y = _dot(u, m_ref[0]) + _dot(p_re.astype(BF16), cre_ref[0]) + _dot(p_im.astype(BF16), cim_ref[0])
    y_ref[0] = y.astype(BF16)
    xre_ref[0] = x_re[rows - 1:rows]
    xim_ref[0] = x_im[rows - 1:rows]


def _prompt_s5(ug, mats, pw_re, pw_im, *, batch, rows):
    g = S5_GROUPS
    per_g = lambda a: pl.BlockSpec((1,) + a.shape[1:], lambda b, j: (j,) + (0,) * (a.ndim - 1))
    per_bg = lambda shape: pl.BlockSpec((1,) + shape, lambda b, j: (b * g + j,) + (0,) * len(shape))
    return pl.pallas_call(
        functools.partial(_s5_kernel, rows=rows),
        grid=(batch, g),
        in_specs=[per_bg((rows, 256))] + [per_g(a) for a in (*mats, pw_re, pw_im)],
        out_specs=[per_bg((rows, 256)), per_bg((1, S5_STATE)), per_bg((1, S5_STATE))],
        out_shape=[
            jax.ShapeDtypeStruct((batch * g, rows, 256), BF16),
            jax.ShapeDtypeStruct((batch * g, 1, S5_STATE), F32),
            jax.ShapeDtypeStruct((batch * g, 1, S5_STATE), F32),
        ],
        compiler_params=_cparams("arbitrary", "arbitrary"),
        name="prompt_s5",
    )(ug, *mats, pw_re, pw_im)


def _s5_prompt_scan(u2d, disc, c_re, c_im, *, batch, seq):
    log_mag, arg, bb_re, bb_im = disc
    g, r = S5_GROUPS, S5_ROW
    rows = seq // r
    mats = _s5_prompt_matrices(log_mag, arg, bb_re, bb_im, c_re, c_im)
    nsteps = max(1, (rows - 1).bit_length())
    strides = (r * (2 ** jnp.arange(16, dtype=F32)))[None, :, None]
    pw_re, pw_im = _lam_pow(log_mag[:, None, :], arg[:, None, :], strides)
    assert nsteps <= 16
    ug = u2d.reshape(batch, rows, r, g, 16).transpose(0, 3, 1, 2, 4).reshape(batch * g, rows, r * 16)
    y, x_re, x_im = _prompt_s5(ug, mats, pw_re, pw_im, batch=batch, rows=rows)
    y2d = y.reshape(batch, g, rows, r, 16).transpose(0, 2, 3, 1, 4).reshape(batch * seq, 256)
    return y2d, x_re.reshape(batch, g, S5_STATE), x_im.reshape(batch, g, S5_STATE)


D_FF = 2816
FFN_SPLIT = 2


def _layer_norm(x, g, b):
    mu = jnp.mean(x, axis=-1, keepdims=True)
    xc = x - mu
    var = jnp.mean(xc * xc, axis=-1, keepdims=True)
    return xc * lax.rsqrt(var + EPS) * g + b


def _post_kernel(*refs, t, decode, alpha):
    (x_ref, oa_ref, ob_ref, yc_ref, u_ref, od_ref, s5d_ref, wglu_ref, bglu_ref, wout_ref, ln1g_ref, ln1b_ref,
     wup_ref, fcw_ref, fcb_ref, wdown_ref, ln2g_ref, ln2b_ref) = refs[:18]
    if decode:
        st_ref, xo_ref, sto_ref = refs[18:]
    else:
        xo_ref, sto_ref, tail_ref = refs[18:]
        i = pl.program_id(1)

        @pl.when(i == 0)
        def _():
            tail_ref[...] = jnp.zeros_like(tail_ref)

    yc = _gelu_tanh(yc_ref[...] + s5d_ref[...] * u_ref[...])
    oc = yc * _sigmoid(_dot(yc.astype(BF16), wglu_ref[...]) + bglu_ref[...])
    mix = _dot(oa_ref[...].astype(BF16), wout_ref[0:256, :])
    mix = mix + _dot(ob_ref[...].astype(BF16), wout_ref[256:512, :])
    mix = mix + _dot(oc.astype(BF16), wout_ref[512:768, :])
    mix = mix + _dot(od_ref[...].astype(BF16), wout_ref[768:1024, :])
    x1 = _layer_norm(alpha * x_ref[...] + mix, ln1g_ref[...], ln1b_ref[...])
    x1b = x1.astype(BF16)
    ffn = None
    width = D_FF // FFN_SPLIT
    for c in range(FFN_SPLIT):
        lo, hi = c * width, (c + 1) * width
        uh = _dot(x1b, wup_ref[:, lo:hi])
        gh = _dot(x1b, wup_ref[:, D_FF + lo:D_FF + hi])
        if decode:
            s0 = st_ref[:, lo:hi]
            s1 = st_ref[:, D_FF + lo:D_FF + hi]
            gconv = fcb_ref[:, lo:hi] + fcw_ref[0:1, lo:hi] * s0 + fcw_ref[1:2, lo:hi] * s1 + fcw_ref[2:3, lo:hi] * gh
            sto_ref[:, lo:hi] = s1
            sto_ref[:, D_FF + lo:D_FF + hi] = gh
        else:
            ext = jnp.concatenate([tail_ref[:, lo:hi], gh], axis=0)
            gconv = fcb_ref[:, lo:hi]
            for j in range(3):
                gconv = gconv + ext[6 + j:6 + j + t] * fcw_ref[j:j + 1, lo:hi]
            tail_ref[:, lo:hi] = gh[t - 8:t]

            @pl.when(i == pl.num_programs(1) - 1)
            def _():
                sto_ref[0, :, lo:hi] = ext[t + 6:t + 8]

        hmid = (_gelu_tanh(gconv) * uh).astype(BF16)
        part = _dot(hmid, wdown_ref[lo:hi, :])
        ffn = part if ffn is None else ffn + part
    xo_ref[...] = _layer_norm(alpha * x1 + ffn, ln2g_ref[...], ln2b_ref[...])


def _post(x, oa, ob, yc, u, od, weights, *, alpha, batch=None, seq=None, t=None, ffn_state=None):
    decode = ffn_state is not None
    n = x.shape[0]
    if decode:
        t = n
        grid = (1,)
        row = lambda i: (0, 0)
        const = lambda a: pl.BlockSpec(a.shape, lambda i: (0,) * a.ndim, pipeline_mode=pl.Buffered(1))
        extra_in = [pl.BlockSpec(ffn_state.shape, row)]
        extra_args = [ffn_state]
        out_specs = [pl.BlockSpec((t, 1024), row), pl.BlockSpec((t, 2 * D_FF), row)]
        out_shape = [jax.ShapeDtypeStruct((n, 1024), F32), jax.ShapeDtypeStruct((n, 2 * D_FF), F32)]
        scratch = []
        sem = ("arbitrary",)
    else:
        nt = seq // t
        grid = (batch, nt)
        row = lambda b, i: (b * nt + i, 0)
        const = lambda a: pl.BlockSpec(a.shape, lambda b, i: (0,) * a.ndim, pipeline_mode=pl.Buffered(1))
        extra_in, extra_args = [], []
        out_specs = [pl.BlockSpec((t, 1024), row), pl.BlockSpec((1, 2, D_FF), lambda b, i: (b, 0, 0))]
        out_shape = [jax.ShapeDtypeStruct((n, 1024), F32), jax.ShapeDtypeStruct((batch, 2, D_FF), F32)]
        scratch = [pltpu.VMEM((8, D_FF), F32)]
        sem = ("arbitrary", "arbitrary")
    acts = (x, oa, ob, yc, u, od)
    return pl.pallas_call(
        functools.partial(_post_kernel, t=t, decode=decode, alpha=alpha),
        grid=grid,
        in_specs=[pl.BlockSpec((t, a.shape[1]), row) for a in acts] + [const(w) for w in weights] + extra_in,
        out_specs=out_specs,
        out_shape=out_shape,
        scratch_shapes=scratch,
        compiler_params=_cparams(*sem),
        name="post_decode" if decode else "post_prompt",
    )(*acts, *weights, *extra_args)


PAGE = 128
PAGES_PER_STEP = 32


def _decode_attn_kernel(pt_ref, q_ref, bias_ref, *refs, pp):
    del pt_ref
    k_refs, v_refs = refs[:pp], refs[pp:2 * pp]
    o_ref, qb_ref, c_ref, acc_ref = refs[2 * pp:]
    s = pl.program_id(1)
    eye = _iota2((256, 256), 0) == _iota2((256, 256), 1)

    @pl.when(s == 0)
    def _():
        q_col = jnp.sum(jnp.where(eye, q_ref[0], 0.0), axis=1, keepdims=True)
        qb_ref[...] = jnp.broadcast_to(q_col, (256, PAGE))
        c_ref[...] = jnp.zeros_like(c_ref)
        acc_ref[...] = jnp.zeros_like(acc_ref)

    ones_sfx = _suffix_ones(PAGE)
    qb = qb_ref[...]
    c = c_ref[...]
    acc = acc_ref[...]
    for r in reversed(range(pp)):
        prod = k_refs[r][0] * qb
        z = jnp.concatenate(
            [jnp.sum(prod[h * HEAD_DIM:(h + 1) * HEAD_DIM], axis=0, keepdims=True) for h in range(4)]
            + [jnp.zeros((4, PAGE), F32)], axis=0) + bias_ref[...]
        sp = _softplus2(z)
        rest = _dot(sp.astype(BF16), ones_sfx)
        w = jnp.exp2(z - rest - c)
        v_t = v_refs[r][0]
        acc = acc + jnp.concatenate(
            [v_t[h * HEAD_DIM:(h + 1) * HEAD_DIM] * w[h:h + 1, :] for h in range(4)], axis=0)
        c = c + rest[:, 0:1]
    c_ref[...] = c
    acc_ref[...] = acc

    @pl.when(s == pl.num_programs(1) - 1)
    def _():
        o_col = jnp.sum(acc, axis=1, keepdims=True)
        o_ref[0] = jnp.sum(jnp.where(eye, o_col, 0.0), axis=0, keepdims=True)


def _decode_attention(q, pool_kt, pool_vt, page_table, bias8, *, base):
    nseq, npages = page_table.shape
    pp = PAGES_PER_STEP
    nsteps = npages // pp

    def page_map(r):
        return lambda b, s, pt: (base + pt[b, (nsteps - 1 - s) * pp + r], 0, 0)

    page_specs = [pl.BlockSpec((1, 256, PAGE), page_map(r)) for r in range(pp)]
    out = pl.pallas_call(
        functools.partial(_decode_attn_kernel, pp=pp),
        grid_spec=pltpu.PrefetchScalarGridSpec(
            num_scalar_prefetch=1,
            grid=(nseq, nsteps),
            in_specs=[pl.BlockSpec((1, 1, 256), lambda b, s, pt: (b, 0, 0)),
                      pl.BlockSpec((8, PAGE), lambda b, s, pt: (0, 0))] + page_specs + page_specs,
            out_specs=pl.BlockSpec((1, 1, 256), lambda b, s, pt: (b, 0, 0)),
            scratch_shapes=[pltpu.VMEM((256, PAGE), F32), pltpu.VMEM((8, PAGE), F32), pltpu.VMEM((256, PAGE), F32)],
        ),
        out_shape=jax.ShapeDtypeStruct((nseq, 1, 256), F32),
        compiler_params=_cparams("arbitrary", "arbitrary"),
        name="decode_attention",
    )(page_table, q.reshape(nseq, 1, 256), bias8, *([pool_kt] * pp), *([pool_vt] * pp))
    return out.reshape(nseq, 256)


def _dot_f32(a, b):
    a0, a1, a2 = _split3(a)
    b0, b1, b2 = _split3(b)
    return (_dot(a0, b0) + (_dot(a0, b1) + _dot(a1, b0))
            + (_dot(a0, b2) + _dot(a2, b0) + _dot(a1, b1)))


_R_DEC, _R_XDT, _R_BM, _R_CM, _R_F, _R_K, _R_Q, _R_V, _R_END = 0, 256, 512, 640, 768, 1024, 1280, 1536, 1792


def _decode_mixers_kernel(zx_ref, u_ref, hg_ref, cs_ref, hssm_ref, x0re_ref, x0im_ref, hhg_ref,
                          cw_ref, cb_ref, dtb_ref, alog_ref, d_ref, ngs_ref, lb_ref, ngh_ref,
                          lre_ref, lim_ref, bre_ref, bim_ref, cre_ref, cim_ref,
                          ob_ref, cso_ref, hssmo_ref, yc_ref, xre_ref, xim_ref, od_ref, hhgo_ref,
                          rows_ref, yssm_ref, yhg_ref, *, nseq):
    z = zx_ref[:, 0:256]
    xbc = zx_ref[:, 256:768]
    s0, s1, s2 = cs_ref[:, 0:512], cs_ref[:, 512:1024], cs_ref[:, 1024:1536]
    conv = cb_ref[...] + cw_ref[0:1, :] * s0 + cw_ref[1:2, :] * s1 + cw_ref[2:3, :] * s2 + cw_ref[3:4, :] * xbc
    cso_ref[:, 0:512] = s1
    cso_ref[:, 512:1024] = s2
    cso_ref[:, 1024:1536] = xbc
    act = _silu(conv)
    xs = act[:, 0:256]
    dt = _softplus(zx_ref[:, 768:1024] + dtb_ref[...])
    rows_ref[:, _R_DEC:_R_XDT] = jnp.exp(dt * (-jnp.exp(alog_ref[...])))
    rows_ref[:, _R_XDT:_R_BM] = xs * dt
    rows_ref[:, _R_BM:_R_F] = act[:, 256:512]
    lb = lb_ref[...]
    fr = hg_ref[:, 256:512]
    rows_ref[:, _R_F:_R_K] = lb + (1.0 - lb) * _sigmoid(fr)
    rows_ref[:, _R_K:_R_Q] = (1.0 - lb) * _sigmoid(-fr)
    rows_ref[:, _R_Q:_R_V] = _silu(hg_ref[:, 0:256])
    rows_ref[:, _R_V:_R_END] = hg_ref[:, 512:768]

    eye = _iota2((256, 256), 0) == _iota2((256, 256), 1)
    row_id = _iota2((256, 64), 0)

    def to_col(r):
        return jnp.sum(jnp.where(eye, r, 0.0), axis=1, keepdims=True)

    def to_row(c):
        return jnp.sum(jnp.where(eye, c, 0.0), axis=0, keepdims=True)

    def per_seq(b, get):
        bm = get(_R_BM, _R_CM)
        cm = get(_R_CM, _R_F)
        bm_rows = jnp.where(row_id < 128, bm[:, 0:64], bm[:, 64:128])
        cm_rows = jnp.where(row_id < 128, cm[:, 0:64], cm[:, 64:128])
        hn = hssm_ref[b] * to_col(get(_R_DEC, _R_XDT)) + to_col(get(_R_XDT, _R_BM)) * bm_rows
        hssmo_ref[b] = hn
        y_ssm = to_row(jnp.sum(hn * cm_rows, axis=1, keepdims=True))
        v = get(_R_V, _R_END)
        v_rows = jnp.where(row_id < 64, v[:, 0:64],
                           jnp.where(row_id < 128, v[:, 64:128],
                                     jnp.where(row_id < 192, v[:, 128:192], v[:, 192:256])))
        gn = hhg_ref[b] * to_col(get(_R_F, _R_K)) + to_col(get(_R_K, _R_Q)) * v_rows
        hhgo_ref[b] = gn
        qg = to_col(get(_R_Q, _R_V)) * gn
        y_hg = jnp.concatenate(
            [jnp.sum(qg[h * 64:(h + 1) * 64], axis=0, keepdims=True) for h in range(4)], axis=1)
        return y_ssm, y_hg

    def per_octet(o, carry):
        base = pl.multiple_of(o * 8, 8)
        blk = rows_ref[pl.ds(base, 8), :]
        ys = [per_seq(base + r, lambda lo, hi, r=r: blk[r:r + 1, lo:hi]) for r in range(8)]
        yssm_ref[pl.ds(base, 8), :] = jnp.concatenate([y[0] for y in ys], axis=0)
        yhg_ref[pl.ds(base, 8), :] = jnp.concatenate([y[1] for y in ys], axis=0)
        return carry

    lax.fori_loop(0, nseq // 8, per_octet, 0)

    y = yssm_ref[...] + d_ref[...] * xs
    ob_ref[...] = _group_rms(y * _silu(z), ngs_ref[...], 128)
    od_ref[...] = _head_rms(yhg_ref[...], ngh_ref[...]) * _silu(hg_ref[:, 768:1024])
    u = u_ref[...]
    x0r, x0i = x0re_ref[...], x0im_ref[...]
    lr, li = lre_ref[...], lim_ref[...]
    xr = lr * x0r - li * x0i + _dot_f32(u, bre_ref[...])
    xi = lr * x0i + li * x0r + _dot_f32(u, bim_ref[...])
    xre_ref[...] = xr
    xim_ref[...] = xi
    yc_ref[...] = _dot_f32(xr, cre_ref[...]) - _dot_f32(xi, cim_ref[...])


def _decode_mixers(zx, u, hg4, conv_state, h_ssm, x0_re, x0_im, h_hg, params):
    nseq = zx.shape[0]
    args = (zx, u, hg4, conv_state, h_ssm, x0_re, x0_im, h_hg, *params)
    full = lambda a: pl.BlockSpec(a.shape, lambda i: (0,) * a.ndim)
    out_shape = [
        jax.ShapeDtypeStruct((nseq, 256), F32),
        jax.ShapeDtypeStruct((nseq, 1536), F32),
        jax.ShapeDtypeStruct((nseq, 256, 64), F32),
        jax.ShapeDtypeStruct((nseq, 256), F32),
        jax.ShapeDtypeStruct((nseq, 1024), F32),
        jax.ShapeDtypeStruct((nseq, 1024), F32),
        jax.ShapeDtypeStruct((nseq, 256), F32),
        jax.ShapeDtypeStruct((nseq, 256, 64), F32),
    ]
    return pl.pallas_call(
        functools.partial(_decode_mixers_kernel, nseq=nseq),
        grid=(1,),
        in_specs=[full(a) for a in args],
        out_specs=[full(s) for s in out_shape],
        out_shape=out_shape,
        scratch_shapes=[pltpu.VMEM((nseq, _R_END), F32), pltpu.VMEM((nseq, 256), F32), pltpu.VMEM((nseq, 256), F32)],
        compiler_params=_cparams("arbitrary"),
        name="decode_mixers",
    )(*args)


def _s5_decode_matrices(log_mag, arg, bb_re, bb_im, c_re, c_im):
    g, n = S5_GROUPS, S5_STATE
    eye = jnp.eye(g, dtype=F32)
    lam_re, lam_im = _lam_pow(log_mag, arg, 1.0)
    expand_b = lambda bb: jnp.einsum("gnk,gj->gkjn", bb, eye).reshape(g * 16, g * n)
    expand_c = lambda cc: jnp.einsum("ghn,gj->gnjh", cc, eye).reshape(g * n, g * 16)
    return (lam_re.reshape(1, g * n), lam_im.reshape(1, g * n),
            expand_b(bb_re), expand_b(bb_im), expand_c(c_re), expand_c(c_im))


ATTN_TQ = 1024
ATTN_TK = 256
PROJ_TILE = 512
SSD_TILE = 256
HGRN_TILE = 256
POST_TILE = 512


def _rearranged_w_in(w_in):
    q, k, v, z, xbc, dt, u, hq, hf, hi, hg = jnp.split(
        w_in, [256, 512, 768, 1024, 1536, 1540, 1796, 2052, 2308, 2564], axis=1)
    dt_full = jnp.repeat(dt, HEAD_DIM, axis=1)
    q_scale = LOG2E * HEAD_DIM ** -0.5
    return jnp.concatenate([q * q_scale, k, v, z, xbc, dt_full, u, hq, hf, hi, hg], axis=1).astype(BF16)


def kernel(x_prompt, x_sample, cache_k, cache_v, state_ssm_conv, state_ssm, state_s5_re, state_s5_im, state_hgrn, state_ffn_conv, page_table, ln1_g, ln1_b, ln2_g, ln2_b, w_in, w_out, sb_logit_bias, ssm_conv_w, ssm_conv_b, ssm_dt_bias, ssm_a_log, ssm_d, ssm_norm_g, s5_a_re, s5_a_im, s5_b_re, s5_b_im, s5_c_re, s5_c_im, s5_d, s5_log_dt, s5_w_glu, s5_b_glu, hg_lb_logits, hg_norm_g, w_up, ffn_conv_w, ffn_conv_b, w_down):
    depth = w_in.shape[0]
    bp, seq, dm = x_prompt.shape
    ns = x_sample.shape[0]
    n_phys = cache_k.shape[1]
    alpha = (2 * depth) ** 0.25
    row = lambda a: a.reshape(1, -1)
    rep = lambda a: jnp.repeat(a, HEAD_DIM).reshape(1, -1)

    pr = jax.nn.softmax(hg_lb_logits.astype(F32), axis=0)
    lbs = jnp.cumsum(pr, axis=0) - pr[0:1]
    pool_kt = cache_k.transpose(0, 1, 3, 4, 2).reshape(depth * n_phys, 256, PAGE)
    pool_vt = cache_v.transpose(0, 1, 3, 4, 2).reshape(depth * n_phys, 256, PAGE)

    xp = x_prompt.reshape(bp * seq, dm)
    xs = x_sample.reshape(ns, dm)
    k_all = jnp.zeros((depth, bp, 256, seq), F32)
    v_all = jnp.zeros((depth, bp, 256, seq), F32)
    outs_p, outs_s = [], []
    for l in range(depth):
        w_p = _rearranged_w_in(w_in[l])
        disc = _s5_discretize(s5_a_re[l], s5_a_im[l], s5_b_re[l], s5_b_im[l], s5_log_dt[l])
        ssd_w = (ssm_conv_w[l], row(ssm_conv_b[l]), rep(ssm_dt_bias[l]), rep(ssm_a_log[l]), rep(ssm_d[l]),
                 row(ssm_norm_g[l]))
        post_w = (row(s5_d[l]), s5_w_glu[l].astype(BF16), row(s5_b_glu[l]), w_out[l].astype(BF16),
                  row(ln1_g[l]), row(ln1_b[l]), w_up[l].astype(BF16), ffn_conv_w[l], row(ffn_conv_b[l]),
                  w_down[l].astype(BF16), row(ln2_g[l]), row(ln2_b[l]))
        lb = row(lbs[l])
        ngh = row(hg_norm_g[l])

        q, vb, zx, u, hg4, kt, k_all, v_all, ub = _projection(
            xp, w_p, tm=PROJ_TILE, tk=ATTN_TK, seq=seq, layer=l, k_all=k_all, v_all=v_all)
        oa = _prompt_attention(q, kt, vb, sb_logit_bias[l], batch=bp, seq=seq, tq=ATTN_TQ, tk=ATTN_TK)
        ob, conv_p, ssm_p = _prompt_ssd(zx, *ssd_w, batch=bp, seq=seq, t=SSD_TILE)
        yc, re_p, im_p = _s5_prompt_scan(ub, disc, s5_c_re[l], s5_c_im[l], batch=bp, seq=seq)
        od, hg_p = _prompt_hgrn(hg4, lb, ngh, batch=bp, seq=seq, t=HGRN_TILE)
        xp, ffn_p = _post(xp, oa, ob, yc, u, od, post_w, alpha=alpha, batch=bp, seq=seq, t=POST_TILE)
        outs_p.append((conv_p, ssm_p, re_p, im_p, hg_p, ffn_p))

        q, k, v, zx, u, hg4 = _projection(xs, w_p, tm=ns)
        bias8 = jnp.broadcast_to(jnp.pad(sb_logit_bias[l] * LOG2E, (0, 4))[:, None], (8, PAGE))
        oa = _decode_attention(q, pool_kt, pool_vt, page_table, bias8, base=l * n_phys)
        dec_w = (*ssd_w, lb, ngh, *_s5_decode_matrices(*disc, s5_c_re[l], s5_c_im[l]))
        ob, conv_s, ssm_s, yc, re_s, im_s, od, hg_s = _decode_mixers(
            zx, u, hg4, state_ssm_conv[l].reshape(ns, 1536), state_ssm[l].reshape(ns, 256, 64),
            state_s5_re[l].reshape(ns, 1024), state_s5_im[l].reshape(ns, 1024),
            state_hgrn[l].reshape(ns, 256, 64), dec_w)
        xs, ffn_s = _post(xs, oa, ob, yc, u, od, post_w, alpha=alpha,
                          ffn_state=state_ffn_conv[l].reshape(ns, 2 * D_FF))
        outs_s.append((k.reshape(ns, 1, 4, HEAD_DIM), v.reshape(ns, 1, 4, HEAD_DIM),
                       conv_s.reshape(ns, 3, 512), ssm_s.reshape(ns, 4, 64, 64),
                       re_s.reshape(ns, S5_GROUPS, S5_STATE), im_s.reshape(ns, S5_GROUPS, S5_STATE),
                       hg_s.reshape(ns, 4, 64, 64), ffn_s.reshape(ns, 2, D_FF)))

    sp = [jnp.stack(col, axis=0) for col in zip(*outs_p)]
    ss = [jnp.stack(col, axis=0) for col in zip(*outs_s)]
    k_prompt = k_all.reshape(depth, bp, 4, HEAD_DIM, seq).transpose(0, 1, 4, 2, 3)
    v_prompt = v_all.reshape(depth, bp, 4, HEAD_DIM, seq).transpose(0, 1, 4, 2, 3)
    return (xp.reshape(bp, seq, dm), xs.reshape(ns, 1, dm), k_prompt, v_prompt, ss[0], ss[1], sp[0], ss[2],
            sp[1], ss[3], sp[2], ss[4], sp[3], ss[5], sp[4], ss[6], sp[5], ss[7])
```

```python
import functools
import math

import jax
import jax.numpy as jnp
from jax import lax
from jax.experimental import pallas as pl
from jax.experimental.pallas import tpu as pltpu

F32 = jnp.float32
BF16 = jnp.bfloat16

HEAD_DIM = 64
GROUP_WIDTH = 256
EPS = 1e-5
NEG_BIG = -1e30

VMEM_LIMIT_BYTES = 56 * 1024 * 1024


def _cparams(*sem):
    return pltpu.CompilerParams(dimension_semantics=sem, vmem_limit_bytes=VMEM_LIMIT_BYTES)


def _dot(a, b):
    return jnp.dot(a, b, preferred_element_type=F32)


def _dot_nt(a, b):
    return lax.dot_general(a, b, (((1,), (1,)), ((), ())), preferred_element_type=F32)


def _dot_tn(a, b):
    return lax.dot_general(a, b, (((0,), (0,)), ((), ())), preferred_element_type=F32)


def _split2(x):
    hi = x.astype(BF16)
    lo = (x - hi.astype(F32)).astype(BF16)
    return hi, lo


def _split3(x):
    hi = x.astype(BF16)
    r = x - hi.astype(F32)
    mid = r.astype(BF16)
    lo = (r - mid.astype(F32)).astype(BF16)
    return hi, mid, lo


def _mm01_left(m01, x, passes):
    parts = _split3(x) if passes == 3 else _split2(x)
    acc = _dot(m01, parts[0])
    for p in parts[1:]:
        acc = acc + _dot(m01, p)
    return acc


def _mm01_right(x, m01, passes):
    parts = _split3(x) if passes == 3 else _split2(x)
    acc = _dot(parts[0], m01)
    for p in parts[1:]:
        acc = acc + _dot(p, m01)
    return acc


def _sigmoid(x):
    return 1.0 / (1.0 + jnp.exp(-x))


def _silu(x):
    return x * _sigmoid(x)


def _softplus(x):
    return jnp.maximum(x, 0.0) + jnp.log(1.0 + jnp.exp(-jnp.abs(x)))


def _log_sigmoid_neg(z):
    return jnp.minimum(-z, 0.0) - jnp.log(1.0 + jnp.exp(-jnp.abs(z)))


def _gelu_tanh(x):
    c = math.sqrt(2.0 / math.pi)
    return 0.5 * x * (1.0 + jnp.tanh(c * (x + 0.044715 * (x * x * x))))


def _iota2(shape, dim):
    return lax.broadcasted_iota(jnp.int32, shape, dim)


PROJ_W = 3072


def _proj_kernel(x_ref, w_ref, *refs, tk):
    x = x_ref[...].astype(BF16)

    def mm(lo, hi):
        return _dot(x, w_ref[:, lo:hi])

    q = mm(0, 256)
    k = mm(256, 512)
    v = mm(512, 768)
    u = mm(1792, 2048)
    if tk:
        _, _, q_ref, vb_ref, zx_ref, u_ref, hg_ref, kt_ref, kall_ref, vall_ref = refs
        u_ref[0] = u[:, 0:128]
        u_ref[1] = u[:, 128:256]
        q_ref[...] = q.astype(BF16)
        vb_ref[...] = v.astype(BF16)
        k_t = k.T
        kall_ref[0, 0] = k_t
        vall_ref[0, 0] = v.T
        for c in range(k.shape[0] // tk):
            kt_ref[c] = k_t[:, c * tk:(c + 1) * tk].astype(BF16)
    else:
        q_ref, k_ref, v_ref, zx_ref, u_ref, hg_ref = refs
        q_ref[...] = q
        k_ref[...] = k
        v_ref[...] = v
        u_ref[...] = u
    zx_ref[...] = mm(768, 1792)
    hg_ref[...] = mm(2048, 3072)


def _projection(x2d, w_p, *, tm, tk=0, seq=None, layer=None, k_all=None, v_all=None):
    n, d = x2d.shape
    row = lambda i: (i, 0)
    rows = lambda width, dtype: (jax.ShapeDtypeStruct((n, width), dtype), pl.BlockSpec((tm, width), row))
    in_specs = [pl.BlockSpec((tm, d), row), pl.BlockSpec((d, PROJ_W), lambda i: (0, 0))]
    args = [x2d, w_p]
    aliases = {}
    if tk:
        nt = seq // tm
        slab = pl.BlockSpec((1, 1, 256, tm), lambda i: (layer, i // nt, 0, i % nt))
        halves = (jax.ShapeDtypeStruct((2, n, 128), F32), pl.BlockSpec((2, tm, 128), lambda i: (0, i, 0)))
        outs = [rows(256, BF16), rows(256, BF16), rows(1024, F32), halves, rows(1024, F32),
                (jax.ShapeDtypeStruct((n // tk, 256, tk), BF16), pl.BlockSpec((tm // tk, 256, tk), lambda i: (i, 0, 0))),
                (jax.ShapeDtypeStruct(k_all.shape, F32), slab), (jax.ShapeDtypeStruct(v_all.shape, F32), slab)]
        in_specs += [pl.BlockSpec(memory_space=pl.ANY), pl.BlockSpec(memory_space=pl.ANY)]
        args += [k_all, v_all]
        aliases = {2: 6, 3: 7}
    else:
        outs = [rows(256, F32), rows(256, F32), rows(256, F32), rows(1024, F32), rows(256, F32), rows(1024, F32)]
    return pl.pallas_call(
        functools.partial(_proj_kernel, tk=tk),
        grid=(n // tm,),
        in_specs=in_specs,
        out_specs=[o[1] for o in outs],
        out_shape=[o[0] for o in outs],
        input_output_aliases=aliases,
        compiler_params=_cparams("arbitrary"),
        name="projection",
    )(*args)


LOG2E = 1.4426950408889634


def _suffix_ones(n):
    return jnp.where(_iota2((n, n), 0) >= _iota2((n, n), 1), 1.0, 0.0).astype(BF16)


def _softplus2(z):
    return jnp.maximum(z, 0.0) + jnp.log2(1.0 + jnp.exp2(-jnp.abs(z)))


def _sb_block(qs, kt_pair, v_pair, bias_col, ones_sfx, carry, mask):
    z = _dot(qs, kt_pair) + bias_col
    sp = _softplus2(z)
    if mask is not None:
        sp = jnp.where(mask, sp, 0.0)
    rest = _dot(sp.astype(BF16), ones_sfx)
    w = jnp.exp2(z - rest - carry)
    if mask is not None:
        w = jnp.where(mask, w, 0.0)
    return _dot(w.astype(BF16), v_pair), rest[:, 0:1]


def _attn_kernel(bias_ref, q_ref, kt_ref, v_ref, o_ref, qs_ref, acc_ref, *, tq, tk):
    hp = pl.program_id(1)
    i = pl.program_id(2)
    ones_sfx = _suffix_ones(tk)
    r = tq // tk
    lane_head = _iota2((tq, 128), 1) // HEAD_DIM
    q = q_ref[...]
    qs_ref[0:tq, :] = jnp.where(lane_head == 0, q, jnp.zeros_like(q))
    qs_ref[tq:2 * tq, :] = jnp.where(lane_head == 1, q, jnp.zeros_like(q))
    bias0 = bias_ref[2 * hp] * LOG2E
    bias1 = bias_ref[2 * hp + 1] * LOG2E

    def kv(j):
        start = pl.multiple_of(j * tk, tk)
        return kt_ref[j], v_ref[pl.ds(start, tk), :]

    acc = jnp.zeros((2 * tq, 128), F32)
    tot = jnp.zeros((2 * tq, 1), F32)
    for d in reversed(range(r)):
        lo = d * tk
        nd = tq - lo
        qs_d = jnp.concatenate([qs_ref[lo:tq, :], qs_ref[tq + lo:2 * tq, :]], axis=0)
        tot_d = jnp.concatenate([tot[lo:tq], tot[tq + lo:2 * tq]], axis=0)
        bias_d = jnp.where(_iota2((2 * nd, 1), 0) < nd, bias0, bias1)
        mask = _iota2((2 * nd, tk), 1) < _iota2((2 * nd, tk), 0) % nd
        kt_p, v_p = kv(i * r + d)
        da, dt = _sb_block(qs_d, kt_p, v_p, bias_d, ones_sfx, tot_d, mask)
        pieces_a = [da[:nd], da[nd:]]
        pieces_t = [dt[:nd], dt[nd:]]
        if lo:
            pieces_a = [jnp.zeros((lo, 128), F32), da[:nd], jnp.zeros((lo, 128), F32), da[nd:]]
            pieces_t = [jnp.zeros((lo, 1), F32), dt[:nd], jnp.zeros((lo, 1), F32), dt[nd:]]
        acc = acc + jnp.concatenate(pieces_a, axis=0)
        tot = tot + jnp.concatenate(pieces_t, axis=0)
    acc_ref[...] = acc
    bias_col = jnp.where(_iota2((2 * tq, 1), 0) < tq, bias0, bias1)

    def body(it, tot):
        for k in range(r):
            kt_p, v_p = kv((i - it) * r - 1 - k)
            da, dt = _sb_block(qs_ref[...], kt_p, v_p, bias_col, ones_sfx, tot, None)
            acc_ref[...] += da
            tot = tot + dt
        return tot

    lax.fori_loop(0, i, body, tot)
    o_ref[...] = jnp.where(lane_head == 0, acc_ref[0:tq, :], acc_ref[tq:2 * tq, :])


def _prompt_attention(q, kt, vb, bias, *, batch, seq, tq, tk):
    n = q.shape[0]
    nq = seq // tq
    return pl.pallas_call(
        functools.partial(_attn_kernel, tq=tq, tk=tk),
        grid=(batch, 2, nq),
        in_specs=[
            pl.BlockSpec(memory_space=pltpu.SMEM),
            pl.BlockSpec((tq, 128), lambda b, hp, i: (b * nq + i, hp)),
            pl.BlockSpec((seq // tk, 128, tk), lambda b, hp, i: (b, hp, 0)),
            pl.BlockSpec((seq, 128), lambda b, hp, i: (b, hp)),
        ],
        out_specs=pl.BlockSpec((tq, 128), lambda b, hp, i: (b * nq + i, hp)),
        out_shape=jax.ShapeDtypeStruct((n, 256), F32),
        scratch_shapes=[pltpu.VMEM((2 * tq, 128), BF16), pltpu.VMEM((2 * tq, 128), F32)],
        compiler_params=_cparams("arbitrary", "arbitrary", "arbitrary"),
        name="prompt_attention",
    )(bias, q, kt, vb)


def _lower_incl(n):
    return jnp.where(_iota2((n, n), 1) <= _iota2((n, n), 0), 1.0, 0.0).astype(BF16)


def _group_rms(x, gain, width):
    outs = []
    for g in range(x.shape[1] // width):
        xg = x[:, g * width:(g + 1) * width]
        ms = jnp.mean(xg * xg, axis=1, keepdims=True)
        outs.append(xg * lax.rsqrt(ms + EPS))
    return jnp.concatenate(outs, axis=1) * gain


def _ssd_kernel(zx_ref, cw_ref, cb_ref, dtb_ref, alog_ref, d_ref, ng_ref,
                o_ref, conv_ref, h_ref, tail_ref, hs_ref, *, t):
    i = pl.program_id(1)

    @pl.when(i == 0)
    def _():
        tail_ref[...] = jnp.zeros_like(tail_ref)
        hs_ref[...] = jnp.zeros_like(hs_ref)

    z = zx_ref[:, 0:256]
    xbc = zx_ref[:, 256:768]
    ext = jnp.concatenate([tail_ref[...], xbc], axis=0)
    conv = cb_ref[...]
    for j in range(4):
        conv = conv + ext[5 + j:5 + j + t] * cw_ref[j:j + 1, :]
    tail_ref[...] = xbc[t - 8:t]
    act = _silu(conv)
    xs, bm, cm = act[:, 0:256], act[:, 256:384], act[:, 384:512]
    dt = _softplus(zx_ref[:, 768:1024] + dtb_ref[...])
    da = dt * (-jnp.exp(alog_ref[...]))
    cum = _mm01_left(_lower_incl(t), da, 3)
    cum_t = cum.T
    xdt = xs * dt
    causal = _iota2((t, t), 1) <= _iota2((t, t), 0)
    bmb = bm.astype(BF16)
    cmb = cm.astype(BF16)
    gmat = [_dot_nt(cmb[:, g * 64:(g + 1) * 64], bmb[:, g * 64:(g + 1) * 64]) for g in range(2)]
    ys = []
    for h in range(4):
        hs = slice(h * 64, (h + 1) * 64)
        gs = slice((h // 2) * 64, (h // 2 + 1) * 64)
        cum_h = cum[:, hs]
        seg = cum[:, h * 64:h * 64 + 1] - cum_t[h * 64:h * 64 + 1, :]
        decay = jnp.exp(jnp.where(causal, seg, NEG_BIG))
        y_intra = _dot((gmat[h // 2] * decay).astype(BF16), xdt[:, hs].astype(BF16))
        last = cum_h[t - 1:t, :]
        h_prev = hs_ref[h]
        y_inter = _dot_nt((cm[:, gs] * jnp.exp(cum_h)).astype(BF16), h_prev.astype(BF16))
        s_c = _dot_tn((xdt[:, hs] * jnp.exp(last - cum_h)).astype(BF16), bmb[:, gs])
        hs_ref[h] = h_prev * jnp.exp(last) + s_c
        ys.append(y_intra + y_inter)
    y = jnp.concatenate(ys, axis=1) + d_ref[...] * xs
    o_ref[...] = _group_rms(y * _silu(z), ng_ref[...], 128)

    @pl.when(i == pl.num_programs(1) - 1)
    def _():
        conv_ref[0] = ext[t + 5:t + 8]
        h_ref[0] = hs_ref[...]


def _prompt_ssd(zx, cw, cb, dtb, alog, d_full, ng, *, batch, seq, t):
    n = zx.shape[0]
    nt = seq // t
    full = lambda a: pl.BlockSpec(a.shape, lambda b, i: (0,) * a.ndim)
    return pl.pallas_call(
        functools.partial(_ssd_kernel, t=t),
        grid=(batch, nt),
        in_specs=[pl.BlockSpec((t, 1024), lambda b, i: (b * nt + i, 0))] + [full(a) for a in (cw, cb, dtb, alog, d_full, ng)],
        out_specs=[
            pl.BlockSpec((t, 256), lambda b, i: (b * nt + i, 0)),
            pl.BlockSpec((1, 3, 512), lambda b, i: (b, 0, 0)),
            pl.BlockSpec((1, 4, 64, 64), lambda b, i: (b, 0, 0, 0)),
        ],
        out_shape=[
            jax.ShapeDtypeStruct((n, 256), F32),
            jax.ShapeDtypeStruct((batch, 3, 512), F32),
            jax.ShapeDtypeStruct((batch, 4, 64, 64), F32),
        ],
        scratch_shapes=[pltpu.VMEM((8, 512), F32), pltpu.VMEM((4, 64, 64), F32)],
        compiler_params=_cparams("arbitrary", "arbitrary"),
        name="prompt_ssd",
    )(zx, cw, cb, dtb, alog, d_full, ng)


HG_CHUNK = 32


def _head_mean_matrix():
    same = (_iota2((256, 256), 0) // 64) == (_iota2((256, 256), 1) // 64)
    return jnp.where(same, 1.0, 0.0).astype(BF16)


def _head_rms(y, gain):
    ms = _mm01_right(y * y, _head_mean_matrix(), 2) * (1.0 / 64.0)
    return y * lax.rsqrt(ms + EPS) * gain


def _hgrn_kernel(hg_ref, lb_ref, ng_ref, o_ref, h_ref, st_ref, *, t):
    i = pl.program_id(1)
    c = HG_CHUNK

    @pl.when(i == 0)
    def _():
        st_ref[...] = jnp.zeros_like(st_ref)

    lb = lb_ref[...]
    q = _silu(hg_ref[:, 0:256])
    fr = hg_ref[:, 256:512]
    log_f = jnp.log(lb + (1.0 - lb) * _sigmoid(fr))
    k = (1.0 - lb) * _sigmoid(-fr)
    v = hg_ref[:, 512:768]
    same_chunk = (_iota2((t, t), 0) // c) == (_iota2((t, t), 1) // c)
    lmat = jnp.where(same_chunk & (_iota2((t, t), 1) <= _iota2((t, t), 0)), 1.0, 0.0).astype(BF16)
    b_all = _mm01_left(lmat, log_f, 3)
    lane_head = _iota2((1, 256), 1) // 64
    head_lane = [lane_head == h for h in range(4)]
    stack_causal = _iota2((4 * c, c), 1) <= (_iota2((4 * c, c), 0) % c)
    block_diag = (_iota2((256, 256), 0) // 64) == (_iota2((256, 256), 1) // 64)
    ys = []
    for n in range(t // c):
        rows = slice(n * c, (n + 1) * c)
        b = b_all[rows]
        ref = b[c // 2 - 1:c // 2]
        last = b[c - 1:c]
        q_c, k_c, v_c = q[rows], k[rows], v[rows]
        qe = q_c * jnp.exp(b - ref)
        ke = (k_c * jnp.exp(ref - b)).astype(BF16)
        kl = (k_c * jnp.exp(last - b)).astype(BF16)
        qb = (q_c * jnp.exp(b)).astype(BF16)
        vb = v_c.astype(BF16)
        q_stack = jnp.concatenate([jnp.where(m, qe, 0.0) for m in head_lane], axis=0).astype(BF16)
        scores = jnp.where(stack_causal, _dot_nt(q_stack, ke), 0.0)
        y4 = _dot(scores.astype(BF16), vb)
        y_intra = jnp.where(head_lane[0], y4[0:c], 0.0)
        for h in range(1, 4):
            y_intra = y_intra + jnp.where(head_lane[h], y4[h * c:(h + 1) * c], 0.0)
        st = st_ref[...]
        y_inter = _dot_nt(qb, st.astype(BF16))
        st_ref[...] = st * jnp.exp(last) + jnp.where(block_diag, _dot_tn(vb, kl), 0.0)
        ys.append(y_intra + y_inter)
    y = jnp.concatenate(ys, axis=0)
    o_ref[...] = _head_rms(y, ng_ref[...]) * _silu(hg_ref[:, 768:1024])

    @pl.when(i == pl.num_programs(1) - 1)
    def _():
        st_t = st_ref[...].T
        for h in range(4):
            h_ref[0, h] = st_t[h * 64:(h + 1) * 64, h * 64:(h + 1) * 64]


def _prompt_hgrn(hg4, lb, ng, *, batch, seq, t):
    n = hg4.shape[0]
    nt = seq // t
    full = lambda a: pl.BlockSpec(a.shape, lambda b, i: (0,) * a.ndim)
    return pl.pallas_call(
        functools.partial(_hgrn_kernel, t=t),
        grid=(batch, nt),
        in_specs=[pl.BlockSpec((t, 1024), lambda b, i: (b * nt + i, 0)), full(lb), full(ng)],
        out_specs=[
            pl.BlockSpec((t, 256), lambda b, i: (b * nt + i, 0)),
            pl.BlockSpec((1, 4, 64, 64), lambda b, i: (b, 0, 0, 0)),
        ],
        out_shape=[jax.ShapeDtypeStruct((n, 256), F32), jax.ShapeDtypeStruct((batch, 4, 64, 64), F32)],
        scratch_shapes=[pltpu.VMEM((256, 256), F32)],
        compiler_params=_cparams("arbitrary", "arbitrary"),
        name="prompt_hgrn",
    )(hg4, lb, ng)


S5_ROW = 16
S5_GROUPS = 16
S5_STATE = 64
S5_SLAB = 128
S5_SEG = 4096
S5_LAM_ROWS = 32


def _s5_discretize(a_re, a_im, b_re, b_im, log_dt):
    dt = jnp.exp(log_dt)[:, None]
    mag = jnp.exp(a_re * dt)
    ab_re = mag * jnp.cos(a_im * dt)
    ab_im = mag * jnp.sin(a_im * dt)
    den = a_re * a_re + a_im * a_im
    coef_re = ((ab_re - 1.0) * a_re + ab_im * a_im) / den
    coef_im = (ab_im * a_re - (ab_re - 1.0) * a_im) / den
    bb_re = coef_re[..., None] * b_re - coef_im[..., None] * b_im
    bb_im = coef_re[..., None] * b_im + coef_im[..., None] * b_re
    return a_re * dt, a_im * dt, bb_re, bb_im


def _lam_pow(log_mag, arg, m):
    mag = jnp.exp(m * log_mag)
    return mag * jnp.cos(m * arg), mag * jnp.sin(m * arg)


def _s5_expand_b(bb):
    eye = jnp.eye(S5_GROUPS, dtype=F32)
    return jnp.einsum("gnk,gj->gkjn", bb, eye).reshape(S5_GROUPS * 16, S5_GROUPS * S5_STATE)


def _s5_expand_c(cc):
    eye = jnp.eye(S5_GROUPS, dtype=F32)
    return jnp.einsum("ghn,gj->gnjh", cc, eye).reshape(S5_GROUPS * S5_STATE, S5_GROUPS * 16)


def _s5_prompt_matrices(log_mag, arg, bb_re, bb_im, c_re, c_im):
    r, g, n = S5_ROW, S5_GROUPS, S5_STATE
    hp = lax.Precision.HIGHEST
    eye = jnp.eye(g, dtype=F32)
    taus = jnp.arange(r, dtype=F32)[:, None, None]
    pw_re, pw_im = _lam_pow(log_mag[None], arg[None], taus)
    p_re = pw_re[..., None] * bb_re[None] - pw_im[..., None] * bb_im[None]
    p_im = pw_re[..., None] * bb_im[None] + pw_im[..., None] * bb_re[None]
    kern = (jnp.einsum("ghn,tgnk->tghk", c_re, p_re, precision=hp)
            - jnp.einsum("ghn,tgnk->tghk", c_im, p_im, precision=hp))
    kf = jnp.einsum("tghk,gj->tgkjh", kern, eye).reshape(r, g * 16, g * 16)
    ns = g * n // S5_SLAB
    slab_cols = lambda m: m.reshape(g * 16, ns, S5_SLAB).transpose(1, 0, 2)
    slab_rows = lambda m: m.reshape(ns, S5_SLAB, g * 16)
    bf = jnp.concatenate([slab_cols(_s5_expand_b(bb_re)), slab_cols(_s5_expand_b(bb_im))], axis=2)
    cf = jnp.concatenate([slab_rows(_s5_expand_c(c_re)), -slab_rows(_s5_expand_c(c_im))], axis=1)
    ks = jnp.arange((S5_LAM_ROWS - 2) // 2, dtype=F32)
    mults = jnp.concatenate([jnp.ones((1,), F32), r * 2.0 ** ks])[:, None, None]
    l_re, l_im = _lam_pow(log_mag[None], arg[None], mults)
    tab = jnp.stack([l_re, l_im], axis=1).reshape(-1, g * n)
    lam = tab.reshape(tab.shape[0], ns, S5_SLAB).transpose(1, 0, 2)
    return kf.astype(BF16), bf.astype(BF16), cf.astype(BF16), lam


def _shift_rows(x, d):
    return jnp.concatenate([jnp.zeros((d, x.shape[1]), x.dtype), x[:x.shape[0] - d]], axis=0)


def _s5_kernel(u_ref, kf_ref, bf_ref, cf_ref, lam_ref, y_ref, xre_ref, xim_ref,
               ubf_ref, xprev_ref, xin_ref, *, seg):
    r = S5_ROW
    rows = seg // r
    ns = bf_ref.shape[0]
    sg = pl.program_id(1)

    @pl.when(sg == 0)
    def _():
        xin_ref[...] = jnp.zeros_like(xin_ref)

    for s in range(r):
        ubf_ref[s] = jnp.concatenate(
            [u_ref[0, pl.ds(s, rows, stride=r), :], u_ref[1, pl.ds(s, rows, stride=r), :]], axis=1).astype(BF16)

    first_row = _iota2((rows, S5_SLAB), 0) == 0
    for j in range(ns):
        l_re, l_im = lam_ref[j, 0:1, :], lam_ref[j, 1:2, :]
        x_re = jnp.zeros((rows, S5_SLAB), F32)
        x_im = jnp.zeros((rows, S5_SLAB), F32)
        for s in range(r):
            z = _dot(ubf_ref[s], bf_ref[j])
            x_re, x_im = (l_re * x_re - l_im * x_im + z[:, :S5_SLAB],
                          l_re * x_im + l_im * x_re + z[:, S5_SLAB:])
        in_re, in_im = xin_ref[0, j, 0:1, :], xin_ref[1, j, 0:1, :]
        m_re, m_im = lam_ref[j, 2:3, :], lam_ref[j, 3:4, :]
        x_re = x_re + jnp.where(first_row, m_re * in_re - m_im * in_im, 0.0)
        x_im = x_im + jnp.where(first_row, m_re * in_im + m_im * in_re, 0.0)
        d, k = 1, 0
        while d < rows:
            m_re, m_im = lam_ref[j, 2 + 2 * k:3 + 2 * k, :], lam_ref[j, 3 + 2 * k:4 + 2 * k, :]
            s_re, s_im = _shift_rows(x_re, d), _shift_rows(x_im, d)
            x_re, x_im = x_re + m_re * s_re - m_im * s_im, x_im + m_re * s_im + m_im * s_re
            d, k = 2 * d, k + 1
        xprev_ref[0, j] = jnp.concatenate([in_re, x_re[:rows - 1]], axis=0)
        xprev_ref[1, j] = jnp.concatenate([in_im, x_im[:rows - 1]], axis=0)
        xin_ref[0, j, 0:1, :] = x_re[rows - 1:rows]
        xin_ref[1, j, 0:1, :] = x_im[rows - 1:rows]

    for p in range(r):
        acc = _dot(ubf_ref[0], kf_ref[p])
        for s in range(1, p + 1):
            acc = acc + _dot(ubf_ref[s], kf_ref[p - s])
        for j in range(ns):
            l_re, l_im = lam_ref[j, 0:1, :], lam_ref[j, 1:2, :]
            x_re, x_im = xprev_ref[0, j], xprev_ref[1, j]
            x_re, x_im = l_re * x_re - l_im * x_im, l_re * x_im + l_im * x_re
            xprev_ref[0, j] = x_re
            xprev_ref[1, j] = x_im
            acc = acc + _dot(jnp.concatenate([x_re, x_im], axis=1).astype(BF16), cf_ref[j])
        y_ref[0, pl.ds(p, rows, stride=r), :] = acc[:, 0:128]
        y_ref[1, pl.ds(p, rows, stride=r), :] = acc[:, 128:256]

    @pl.when(sg == pl.num_programs(1) - 1)
    def _():
        xre_ref[0] = jnp.concatenate([xin_ref[0, j, 0:1, :] for j in range(ns)], axis=1)
        xim_ref[0] = jnp.concatenate([xin_ref[1, j, 0:1, :] for j in range(ns)], axis=1)


def _s5_prompt_scan(u_halves, disc, c_re, c_im, *, batch, seq):
    log_mag, arg, bb_re, bb_im = disc
    g, n = S5_GROUPS, S5_STATE
    seg = min(S5_SEG, seq)
    nseg = seq // seg
    rows = seg // S5_ROW
    assert (rows - 1).bit_length() <= (S5_LAM_ROWS - 2) // 2
    mats = _s5_prompt_matrices(log_mag, arg, bb_re, bb_im, c_re, c_im)
    ns = g * n // S5_SLAB
    full = lambda a: pl.BlockSpec(a.shape, lambda b, s: (0,) * a.ndim)
    halves = pl.BlockSpec((2, seg, 128), lambda b, s: (0, b * nseg + s, 0))
    state = pl.BlockSpec((1, 1, g * n), lambda b, s: (b, 0, 0))
    y, x_re, x_im = pl.pallas_call(
        functools.partial(_s5_kernel, seg=seg),
        grid=(batch, nseg),
        in_specs=[halves] + [full(a) for a in mats],
        out_specs=[halves, state, state],
        out_shape=[jax.ShapeDtypeStruct((2, batch * seq, 128), F32),
                   jax.ShapeDtypeStruct((batch, 1, g * n), F32),
                   jax.ShapeDtypeStruct((batch, 1, g * n), F32)],
        scratch_shapes=[pltpu.VMEM((S5_ROW, rows, 256), BF16),
                        pltpu.VMEM((2, ns, rows, S5_SLAB), F32),
                        pltpu.VMEM((2, ns, 8, S5_SLAB), F32)],
        compiler_params=_cparams("arbitrary", "arbitrary"),
        name="prompt_s5",
    )(u_halves, *mats)
    return y, x_re.reshape(batch, g, n), x_im.reshape(batch, g, n)


D_FF = 2816
FFN_SPLIT = 2


def _layer_norm(x, g, b):
    mu = jnp.mean(x, axis=-1, keepdims=True)
    xc = x - mu
    var = jnp.mean(xc * xc, axis=-1, keepdims=True)
    return xc * lax.rsqrt(var + EPS) * g + b


def _post_kernel(*refs, t, decode, alpha):
    (x_ref, oa_ref, ob_ref, yc_ref, u_ref, od_ref, s5d_ref, wglu_ref, bglu_ref, wout_ref, ln1g_ref, ln1b_ref,
     wup_ref, fcw_ref, fcb_ref, wdown_ref, ln2g_ref, ln2b_ref) = refs[:18]
    if decode:
        st_ref, xo_ref, sto_ref = refs[18:]
    else:
        xo_ref, sto_ref, tail_ref = refs[18:]
        i = pl.program_id(1)

        @pl.when(i == 0)
        def _():
            tail_ref[...] = jnp.zeros_like(tail_ref)

    if decode:
        yc_pre, u = yc_ref[...], u_ref[...]
    else:
        yc_pre = jnp.concatenate([yc_ref[0], yc_ref[1]], axis=1)
        u = jnp.concatenate([u_ref[0], u_ref[1]], axis=1)
    yc = _gelu_tanh(yc_pre + s5d_ref[...] * u)
    oc = yc * _sigmoid(_dot(yc.astype(BF16), wglu_ref[...]) + bglu_ref[...])
    mix = _dot(oa_ref[...].astype(BF16), wout_ref[0:256, :])
    mix = mix + _dot(ob_ref[...].astype(BF16), wout_ref[256:512, :])
    mix = mix + _dot(oc.astype(BF16), wout_ref[512:768, :])
    mix = mix + _dot(od_ref[...].astype(BF16), wout_ref[768:1024, :])
    x1 = _layer_norm(alpha * x_ref[...] + mix, ln1g_ref[...], ln1b_ref[...])
    x1b = x1.astype(BF16)
    ffn = None
    width = D_FF // FFN_SPLIT
    for c in range(FFN_SPLIT):
        lo, hi = c * width, (c + 1) * width
        uh = _dot(x1b, wup_ref[:, lo:hi])
        gh = _dot(x1b, wup_ref[:, D_FF + lo:D_FF + hi])
        if decode:
            s0 = st_ref[:, lo:hi]
            s1 = st_ref[:, D_FF + lo:D_FF + hi]
            gconv = fcb_ref[:, lo:hi] + fcw_ref[0:1, lo:hi] * s0 + fcw_ref[1:2, lo:hi] * s1 + fcw_ref[2:3, lo:hi] * gh
            sto_ref[:, lo:hi] = s1
            sto_ref[:, D_FF + lo:D_FF + hi] = gh
        else:
            ext = jnp.concatenate([tail_ref[:, lo:hi], gh], axis=0)
            gconv = fcb_ref[:, lo:hi]
            for j in range(3):
                gconv = gconv + ext[6 + j:6 + j + t] * fcw_ref[j:j + 1, lo:hi]
            tail_ref[:, lo:hi] = gh[t - 8:t]

            @pl.when(i == pl.num_programs(1) - 1)
            def _():
                sto_ref[0, :, lo:hi] = ext[t + 6:t + 8]

        hmid = (_gelu_tanh(gconv) * uh).astype(BF16)
        part = _dot(hmid, wdown_ref[lo:hi, :])
        ffn = part if ffn is None else ffn + part
    xo_ref[...] = _layer_norm(alpha * x1 + ffn, ln2g_ref[...], ln2b_ref[...])


def _post(x, oa, ob, yc, u, od, weights, *, alpha, batch=None, seq=None, t=None, ffn_state=None):
    decode = ffn_state is not None
    n = x.shape[0]
    if decode:
        t = n
        grid = (1,)
        row = lambda i: (0, 0)
        const = lambda a: pl.BlockSpec(a.shape, lambda i: (0,) * a.ndim, pipeline_mode=pl.Buffered(1))
        extra_in = [pl.BlockSpec(ffn_state.shape, row)]
        extra_args = [ffn_state]
        out_specs = [pl.BlockSpec((t, 1024), row), pl.BlockSpec((t, 2 * D_FF), row)]
        out_shape = [jax.ShapeDtypeStruct((n, 1024), F32), jax.ShapeDtypeStruct((n, 2 * D_FF), F32)]
        scratch = []
        sem = ("arbitrary",)
    else:
        nt = seq // t
        grid = (batch, nt)
        row = lambda b, i: (b * nt + i, 0)
        const = lambda a: pl.BlockSpec(a.shape, lambda b, i: (0,) * a.ndim, pipeline_mode=pl.Buffered(1))
        extra_in, extra_args = [], []
        out_specs = [pl.BlockSpec((t, 1024), row), pl.BlockSpec((1, 2, D_FF), lambda b, i: (b, 0, 0))]
        out_shape = [jax.ShapeDtypeStruct((n, 1024), F32), jax.ShapeDtypeStruct((batch, 2, D_FF), F32)]
        scratch = [pltpu.VMEM((8, D_FF), F32)]
        sem = ("arbitrary", "arbitrary")
    acts = (x, oa, ob, yc, u, od)

    def act_spec(a):
        if a.ndim == 3:
            return pl.BlockSpec((2, t, 128), lambda b, i: (0, b * nt + i, 0))
        return pl.BlockSpec((t, a.shape[1]), row)

    return pl.pallas_call(
        functools.partial(_post_kernel, t=t, decode=decode, alpha=alpha),
        grid=grid,
        in_specs=[act_spec(a) for a in acts] + [const(w) for w in weights] + extra_in,
        out_specs=out_specs,
        out_shape=out_shape,
        scratch_shapes=scratch,
        compiler_params=_cparams(*sem),
        name="post_decode" if decode else "post_prompt",
    )(*acts, *weights, *extra_args)


PAGE = 128
PAGES_PER_STEP = 32


def _decode_attn_kernel(pt_ref, q_ref, bias_ref, *refs, pp):
    del pt_ref
    k_refs, v_refs = refs[:pp], refs[pp:2 * pp]
    o_ref, qb_ref, c_ref, acc_ref = refs[2 * pp:]
    s = pl.program_id(1)
    eye = _iota2((256, 256), 0) == _iota2((256, 256), 1)

    @pl.when(s == 0)
    def _():
        q_col = jnp.sum(jnp.where(eye, q_ref[0], 0.0), axis=1, keepdims=True)
        qb_ref[...] = jnp.broadcast_to(q_col, (256, PAGE))
        c_ref[...] = jnp.zeros_like(c_ref)
        acc_ref[...] = jnp.zeros_like(acc_ref)

    ones_sfx = _suffix_ones(PAGE)
    qb = qb_ref[...]
    c = c_ref[...]
    acc = acc_ref[...]
    for r in reversed(range(pp)):
        prod = k_refs[r][0] * qb
        z = jnp.concatenate(
            [jnp.sum(prod[h * HEAD_DIM:(h + 1) * HEAD_DIM], axis=0, keepdims=True) for h in range(4)]
            + [jnp.zeros((4, PAGE), F32)], axis=0) + bias_ref[...]
        sp = _softplus2(z)
        rest = _dot(sp.astype(BF16), ones_sfx)
        w = jnp.exp2(z - rest - c)
        v_t = v_refs[r][0]
        acc = acc + jnp.concatenate(
            [v_t[h * HEAD_DIM:(h + 1) * HEAD_DIM] * w[h:h + 1, :] for h in range(4)], axis=0)
        c = c + rest[:, 0:1]
    c_ref[...] = c
    acc_ref[...] = acc

    @pl.when(s == pl.num_programs(1) - 1)
    def _():
        o_col = jnp.sum(acc, axis=1, keepdims=True)
        o_ref[0] = jnp.sum(jnp.where(eye, o_col, 0.0), axis=0, keepdims=True)


def _decode_attention(q, pool_kt, pool_vt, page_table, bias8, *, base):
    nseq, npages = page_table.shape
    pp = PAGES_PER_STEP
    nsteps = npages // pp

    def page_map(r):
        return lambda b, s, pt: (base + pt[b, (nsteps - 1 - s) * pp + r], 0, 0)

    page_specs = [pl.BlockSpec((1, 256, PAGE), page_map(r)) for r in range(pp)]
    out = pl.pallas_call(
        functools.partial(_decode_attn_kernel, pp=pp),
        grid_spec=pltpu.PrefetchScalarGridSpec(
            num_scalar_prefetch=1,
            grid=(nseq, nsteps),
            in_specs=[pl.BlockSpec((1, 1, 256), lambda b, s, pt: (b, 0, 0)),
                      pl.BlockSpec((8, PAGE), lambda b, s, pt: (0, 0))] + page_specs + page_specs,
            out_specs=pl.BlockSpec((1, 1, 256), lambda b, s, pt: (b, 0, 0)),
            scratch_shapes=[pltpu.VMEM((256, PAGE), F32), pltpu.VMEM((8, PAGE), F32), pltpu.VMEM((256, PAGE), F32)],
        ),
        out_shape=jax.ShapeDtypeStruct((nseq, 1, 256), F32),
        compiler_params=_cparams("arbitrary", "arbitrary"),
        name="decode_attention",
    )(page_table, q.reshape(nseq, 1, 256), bias8, *([pool_kt] * pp), *([pool_vt] * pp))
    return out.reshape(nseq, 256)


def _dot_f32(a, b):
    a0, a1, a2 = _split3(a)
    b0, b1, b2 = _split3(b)
    return (_dot(a0, b0) + (_dot(a0, b1) + _dot(a1, b0))
            + (_dot(a0, b2) + _dot(a2, b0) + _dot(a1, b1)))


_R_DEC, _R_XDT, _R_BM, _R_CM, _R_F, _R_K, _R_Q, _R_V, _R_END = 0, 256, 512, 640, 768, 1024, 1280, 1536, 1792


def _decode_mixers_kernel(zx_ref, u_ref, hg_ref, cs_ref, hssm_ref, x0re_ref, x0im_ref, hhg_ref,
                          cw_ref, cb_ref, dtb_ref, alog_ref, d_ref, ngs_ref, lb_ref, ngh_ref,
                          lre_ref, lim_ref, bre_ref, bim_ref, cre_ref, cim_ref,
                          ob_ref, cso_ref, hssmo_ref, yc_ref, xre_ref, xim_ref, od_ref, hhgo_ref,
                          rows_ref, yssm_ref, yhg_ref, *, nseq):
    z = zx_ref[:, 0:256]
    xbc = zx_ref[:, 256:768]
    s0, s1, s2 = cs_ref[:, 0:512], cs_ref[:, 512:1024], cs_ref[:, 1024:1536]
    conv = cb_ref[...] + cw_ref[0:1, :] * s0 + cw_ref[1:2, :] * s1 + cw_ref[2:3, :] * s2 + cw_ref[3:4, :] * xbc
    cso_ref[:, 0:512] = s1
    cso_ref[:, 512:1024] = s2
    cso_ref[:, 1024:1536] = xbc
    act = _silu(conv)
    xs = act[:, 0:256]
    dt = _softplus(zx_ref[:, 768:1024] + dtb_ref[...])
    rows_ref[:, _R_DEC:_R_XDT] = jnp.exp(dt * (-jnp.exp(alog_ref[...])))
    rows_ref[:, _R_XDT:_R_BM] = xs * dt
    rows_ref[:, _R_BM:_R_F] = act[:, 256:512]
    lb = lb_ref[...]
    fr = hg_ref[:, 256:512]
    rows_ref[:, _R_F:_R_K] = lb + (1.0 - lb) * _sigmoid(fr)
    rows_ref[:, _R_K:_R_Q] = (1.0 - lb) * _sigmoid(-fr)
    rows_ref[:, _R_Q:_R_V] = _silu(hg_ref[:, 0:256])
    rows_ref[:, _R_V:_R_END] = hg_ref[:, 512:768]

    eye = _iota2((256, 256), 0) == _iota2((256, 256), 1)
    row_id = _iota2((256, 64), 0)

    def to_col(r):
        return jnp.sum(jnp.where(eye, r, 0.0), axis=1, keepdims=True)

    def to_row(c):
        return jnp.sum(jnp.where(eye, c, 0.0), axis=0, keepdims=True)

    def per_seq(b, get):
        bm = get(_R_BM, _R_CM)
        cm = get(_R_CM, _R_F)
        bm_rows = jnp.where(row_id < 128, bm[:, 0:64], bm[:, 64:128])
        cm_rows = jnp.where(row_id < 128, cm[:, 0:64], cm[:, 64:128])
        hn = hssm_ref[b] * to_col(get(_R_DEC, _R_XDT)) + to_col(get(_R_XDT, _R_BM)) * bm_rows
        hssmo_ref[b] = hn
        y_ssm = to_row(jnp.sum(hn * cm_rows, axis=1, keepdims=True))
        v = get(_R_V, _R_END)
        v_rows = jnp.where(row_id < 64, v[:, 0:64],
                           jnp.where(row_id < 128, v[:, 64:128],
                                     jnp.where(row_id < 192, v[:, 128:192], v[:, 192:256])))
        gn = hhg_ref[b] * to_col(get(_R_F, _R_K)) + to_col(get(_R_K, _R_Q)) * v_rows
        hhgo_ref[b] = gn
        qg = to_col(get(_R_Q, _R_V)) * gn
        y_hg = jnp.concatenate(
            [jnp.sum(qg[h * 64:(h + 1) * 64], axis=0, keepdims=True) for h in range(4)], axis=1)
        return y_ssm, y_hg

    def per_octet(o, carry):
        base = pl.multiple_of(o * 8, 8)
        blk = rows_ref[pl.ds(base, 8), :]
        ys = [per_seq(base + r, lambda lo, hi, r=r: blk[r:r + 1, lo:hi]) for r in range(8)]
        yssm_ref[pl.ds(base, 8), :] = jnp.concatenate([y[0] for y in ys], axis=0)
        yhg_ref[pl.ds(base, 8), :] = jnp.concatenate([y[1] for y in ys], axis=0)
        return carry

    lax.fori_loop(0, nseq // 8, per_octet, 0)

    y = yssm_ref[...] + d_ref[...] * xs
    ob_ref[...] = _group_rms(y * _silu(z), ngs_ref[...], 128)
    od_ref[...] = _head_rms(yhg_ref[...], ngh_ref[...]) * _silu(hg_ref[:, 768:1024])
    u = u_ref[...]
    x0r, x0i = x0re_ref[...], x0im_ref[...]
    lr, li = lre_ref[...], lim_ref[...]
    xr = lr * x0r - li * x0i + _dot_f32(u, bre_ref[...])
    xi = lr * x0i + li * x0r + _dot_f32(u, bim_ref[...])
    xre_ref[...] = xr
    xim_ref[...] = xi
    yc_ref[...] = _dot_f32(xr, cre_ref[...]) - _dot_f32(xi, cim_ref[...])


def _decode_mixers(zx, u, hg4, conv_state, h_ssm, x0_re, x0_im, h_hg, params):
    nseq = zx.shape[0]
    args = (zx, u, hg4, conv_state, h_ssm, x0_re, x0_im, h_hg, *params)
    full = lambda a: pl.BlockSpec(a.shape, lambda i: (0,) * a.ndim)
    out_shape = [
        jax.ShapeDtypeStruct((nseq, 256), F32),
        jax.ShapeDtypeStruct((nseq, 1536), F32),
        jax.ShapeDtypeStruct((nseq, 256, 64), F32),
        jax.ShapeDtypeStruct((nseq, 256), F32),
        jax.ShapeDtypeStruct((nseq, 1024), F32),
        jax.ShapeDtypeStruct((nseq, 1024), F32),
        jax.ShapeDtypeStruct((nseq, 256), F32),
        jax.ShapeDtypeStruct((nseq, 256, 64), F32),
    ]
    return pl.pallas_call(
        functools.partial(_decode_mixers_kernel, nseq=nseq),
        grid=(1,),
        in_specs=[full(a) for a in args],
        out_specs=[full(s) for s in out_shape],
        out_shape=out_shape,
        scratch_shapes=[pltpu.VMEM((nseq, _R_END), F32), pltpu.VMEM((nseq, 256), F32), pltpu.VMEM((nseq, 256), F32)],
        compiler_params=_cparams("arbitrary"),
        name="decode_mixers",
    )(*args)


def _s5_decode_matrices(log_mag, arg, bb_re, bb_im, c_re, c_im):
    g, n = S5_GROUPS, S5_STATE
    lam_re, lam_im = _lam_pow(log_mag, arg, 1.0)
    return (lam_re.reshape(1, g * n), lam_im.reshape(1, g * n),
            _s5_expand_b(bb_re), _s5_expand_b(bb_im), _s5_expand_c(c_re), _s5_expand_c(c_im))


ATTN_TQ = 1024
ATTN_TK = 256
PROJ_TILE = 512
SSD_TILE = 256
HGRN_TILE = 256
POST_TILE = 512


def _rearranged_w_in(w_in):
    q, k, v, z, xbc, dt, u, hq, hf, hi, hg = jnp.split(
        w_in, [256, 512, 768, 1024, 1536, 1540, 1796, 2052, 2308, 2564], axis=1)
    dt_full = jnp.repeat(dt, HEAD_DIM, axis=1)
    q_scale = LOG2E * HEAD_DIM ** -0.5
    return jnp.concatenate([q * q_scale, k, v, z, xbc, dt_full, u, hq, hf, hi, hg], axis=1).astype(BF16)


def kernel(x_prompt, x_sample, cache_k, cache_v, state_ssm_conv, state_ssm, state_s5_re, state_s5_im, state_hgrn, state_ffn_conv, page_table, ln1_g, ln1_b, ln2_g, ln2_b, w_in, w_out, sb_logit_bias, ssm_conv_w, ssm_conv_b, ssm_dt_bias, ssm_a_log, ssm_d, ssm_norm_g, s5_a_re, s5_a_im, s5_b_re, s5_b_im, s5_c_re, s5_c_im, s5_d, s5_log_dt, s5_w_glu, s5_b_glu, hg_lb_logits, hg_norm_g, w_up, ffn_conv_w, ffn_conv_b, w_down):
    depth = w_in.shape[0]
    bp, seq, dm = x_prompt.shape
    ns = x_sample.shape[0]
    n_phys = cache_k.shape[1]
    alpha = (2 * depth) ** 0.25
    row = lambda a: a.reshape(1, -1)
    rep = lambda a: jnp.repeat(a, HEAD_DIM).reshape(1, -1)

    pr = jax.nn.softmax(hg_lb_logits.astype(F32), axis=0)
    lbs = jnp.cumsum(pr, axis=0) - pr[0:1]
    pool_kt = cache_k.transpose(0, 1, 3, 4, 2).reshape(depth * n_phys, 256, PAGE)
    pool_vt = cache_v.transpose(0, 1, 3, 4, 2).reshape(depth * n_phys, 256, PAGE)

    xp = x_prompt.reshape(bp * seq, dm)
    xs = x_sample.reshape(ns, dm)
    k_all = jnp.zeros((depth, bp, 256, seq), F32)
    v_all = jnp.zeros((depth, bp, 256, seq), F32)
    outs_p, outs_s = [], []
    for l in range(depth):
        w_p = _rearranged_w_in(w_in[l])
        disc = _s5_discretize(s5_a_re[l], s5_a_im[l], s5_b_re[l], s5_b_im[l], s5_log_dt[l])
        ssd_w = (ssm_conv_w[l], row(ssm_conv_b[l]), rep(ssm_dt_bias[l]), rep(ssm_a_log[l]), rep(ssm_d[l]),
                 row(ssm_norm_g[l]))
        post_w = (row(s5_d[l]), s5_w_glu[l].astype(BF16), row(s5_b_glu[l]), w_out[l].astype(BF16),
                  row(ln1_g[l]), row(ln1_b[l]), w_up[l].astype(BF16), ffn_conv_w[l], row(ffn_conv_b[l]),
                  w_down[l].astype(BF16), row(ln2_g[l]), row(ln2_b[l]))
        lb = row(lbs[l])
        ngh = row(hg_norm_g[l])

        q, vb, zx, u, hg4, kt, k_all, v_all = _projection(
            xp, w_p, tm=PROJ_TILE, tk=ATTN_TK, seq=seq, layer=l, k_all=k_all, v_all=v_all)
        oa = _prompt_attention(q, kt, vb, sb_logit_bias[l], batch=bp, seq=seq, tq=ATTN_TQ, tk=ATTN_TK)
        ob, conv_p, ssm_p = _prompt_ssd(zx, *ssd_w, batch=bp, seq=seq, t=SSD_TILE)
        yc, re_p, im_p = _s5_prompt_scan(u, disc, s5_c_re[l], s5_c_im[l], batch=bp, seq=seq)
        od, hg_p = _prompt_hgrn(hg4, lb, ngh, batch=bp, seq=seq, t=HGRN_TILE)
        xp, ffn_p = _post(xp, oa, ob, yc, u, od, post_w, alpha=alpha, batch=bp, seq=seq, t=POST_TILE)
        outs_p.append((conv_p, ssm_p, re_p, im_p, hg_p, ffn_p))

        q, k, v, zx, u, hg4 = _projection(xs, w_p, tm=ns)
        bias8 = jnp.broadcast_to(jnp.pad(sb_logit_bias[l] * LOG2E, (0, 4))[:, None], (8, PAGE))
        oa = _decode_attention(q, pool_kt, pool_vt, page_table, bias8, base=l * n_phys)
        dec_w = (*ssd_w, lb, ngh, *_s5_decode_matrices(*disc, s5_c_re[l], s5_c_im[l]))
        ob, conv_s, ssm_s, yc, re_s, im_s, od, hg_s = _decode_mixers(
            zx, u, hg4, state_ssm_conv[l].reshape(ns, 1536), state_ssm[l].reshape(ns, 256, 64),
            state_s5_re[l].reshape(ns, 1024), state_s5_im[l].reshape(ns, 1024),
            state_hgrn[l].reshape(ns, 256, 64), dec_w)
        xs, ffn_s = _post(xs, oa, ob, yc, u, od, post_w, alpha=alpha,
                          ffn_state=state_ffn_conv[l].reshape(ns, 2 * D_FF))
        outs_s.append((k.reshape(ns, 1, 4, HEAD_DIM), v.reshape(ns, 1, 4, HEAD_DIM),
                       conv_s.reshape(ns, 3, 512), ssm_s.reshape(ns, 4, 64, 64),
                       re_s.reshape(ns, S5_GROUPS, S5_STATE), im_s.reshape(ns, S5_GROUPS, S5_STATE),
                       hg_s.reshape(ns, 4, 64, 64), ffn_s.reshape(ns, 2, D_FF)))

    sp = [jnp.stack(col, axis=0) for col in zip(*outs_p)]
    ss = [jnp.stack(col, axis=0) for col in zip(*outs_s)]
    k_prompt = k_all.reshape(depth, bp, 4, HEAD_DIM, seq).transpose(0, 1, 4, 2, 3)
    v_prompt = v_all.reshape(depth, bp, 4, HEAD_DIM, seq).transpose(0, 1, 4, 2, 3)
    return (xp.reshape(bp, seq, dm), xs.reshape(ns, 1, dm), k_prompt, v_prompt, ss[0], ss[1], sp[0], ss[2],
            sp[1], ss[3], sp[2], ss[4], sp[3], ss[5], sp[4], ss[6], sp[5], ss[7])
```

```python
import functools
import math

import jax
import jax.numpy as jnp
from jax import lax
from jax.experimental import pallas as pl
from jax.experimental.pallas import tpu as pltpu

F32 = jnp.float32
BF16 = jnp.bfloat16

HEAD_DIM = 64
GROUP_WIDTH = 256
EPS = 1e-5
NEG_BIG = -1e30

VMEM_LIMIT_BYTES = 56 * 1024 * 1024


def _cparams(*sem):
    return pltpu.CompilerParams(dimension_semantics=sem, vmem_limit_bytes=VMEM_LIMIT_BYTES)


def _layer_spec(a, layer, **kw):
    return pl.BlockSpec((1,) + a.shape[1:], lambda *_: (layer,) + (0,) * (a.ndim - 1), **kw)


def _drop_layer_axis(*refs):
    return tuple(r.at[0] for r in refs)


def _dot(a, b):
    return jnp.dot(a, b, preferred_element_type=F32)


def _dot_nt(a, b):
    return lax.dot_general(a, b, (((1,), (1,)), ((), ())), preferred_element_type=F32)


def _dot_tn(a, b):
    return lax.dot_general(a, b, (((0,), (0,)), ((), ())), preferred_element_type=F32)


def _split2(x):
    hi = x.astype(BF16)
    lo = (x - hi.astype(F32)).astype(BF16)
    return hi, lo


def _split3(x):
    hi = x.astype(BF16)
    r = x - hi.astype(F32)
    mid = r.astype(BF16)
    lo = (r - mid.astype(F32)).astype(BF16)
    return hi, mid, lo


def _mm01_left(m01, x, passes):
    parts = _split3(x) if passes == 3 else _split2(x)
    acc = _dot(m01, parts[0])
    for p in parts[1:]:
        acc = acc + _dot(m01, p)
    return acc


def _mm01_right(x, m01, passes):
    parts = _split3(x) if passes == 3 else _split2(x)
    acc = _dot(parts[0], m01)
    for p in parts[1:]:
        acc = acc + _dot(p, m01)
    return acc


def _sigmoid(x):
    return 1.0 / (1.0 + jnp.exp(-x))


def _silu(x):
    return x * _sigmoid(x)


def _softplus(x):
    return jnp.maximum(x, 0.0) + jnp.log(1.0 + jnp.exp(-jnp.abs(x)))


def _log_sigmoid_neg(z):
    return jnp.minimum(-z, 0.0) - jnp.log(1.0 + jnp.exp(-jnp.abs(z)))


def _gelu_tanh(x):
    c = math.sqrt(2.0 / math.pi)
    return 0.5 * x * (1.0 + jnp.tanh(c * (x + 0.044715 * (x * x * x))))


def _iota2(shape, dim):
    return lax.broadcasted_iota(jnp.int32, shape, dim)


PROJ_W = 3072


def _proj_kernel(x_ref, w_ref, *refs, tk):
    (w_ref,) = _drop_layer_axis(w_ref)
    x = x_ref[...].astype(BF16)

    def mm(lo, hi):
        return _dot(x, w_ref[:, lo:hi])

    q = mm(0, 256)
    k = mm(256, 512)
    v = mm(512, 768)
    u = mm(1792, 2048)
    if tk:
        _, _, q_ref, vb_ref, zx_ref, u_ref, hg_ref, kt_ref, kall_ref, vall_ref = refs
        u_ref[0] = u[:, 0:128]
        u_ref[1] = u[:, 128:256]
        q_ref[...] = q.astype(BF16)
        vb_ref[...] = v.astype(BF16)
        k_t = k.T
        kall_ref[0, 0] = k_t
        vall_ref[0, 0] = v.T
        for c in range(k.shape[0] // tk):
            kt_ref[c] = k_t[:, c * tk:(c + 1) * tk].astype(BF16)
    else:
        q_ref, k_ref, v_ref, zx_ref, u_ref, hg_ref = refs
        q_ref[...] = q
        k_ref[...] = k
        v_ref[...] = v
        u_ref[...] = u
    zx_ref[...] = mm(768, 1792)
    hg_ref[...] = mm(2048, 3072)


def _projection(x2d, w_p, *, tm, tk=0, seq=None, layer=None, k_all=None, v_all=None):
    n, d = x2d.shape
    row = lambda i: (i, 0)
    rows = lambda width, dtype: (jax.ShapeDtypeStruct((n, width), dtype), pl.BlockSpec((tm, width), row))
    in_specs = [pl.BlockSpec((tm, d), row), _layer_spec(w_p, layer)]
    args = [x2d, w_p]
    aliases = {}
    if tk:
        nt = seq // tm
        slab = pl.BlockSpec((1, 1, 256, tm), lambda i: (layer, i // nt, 0, i % nt))
        halves = (jax.ShapeDtypeStruct((2, n, 128), F32), pl.BlockSpec((2, tm, 128), lambda i: (0, i, 0)))
        outs = [rows(256, BF16), rows(256, BF16), rows(1024, F32), halves, rows(1024, F32),
                (jax.ShapeDtypeStruct((n // tk, 256, tk), BF16), pl.BlockSpec((tm // tk, 256, tk), lambda i: (i, 0, 0))),
                (jax.ShapeDtypeStruct(k_all.shape, F32), slab), (jax.ShapeDtypeStruct(v_all.shape, F32), slab)]
        in_specs += [pl.BlockSpec(memory_space=pl.ANY), pl.BlockSpec(memory_space=pl.ANY)]
        args += [k_all, v_all]
        aliases = {2: 6, 3: 7}
    else:
        outs = [rows(256, F32), rows(256, F32), rows(256, F32), rows(1024, F32), rows(256, F32), rows(1024, F32)]
    return pl.pallas_call(
        functools.partial(_proj_kernel, tk=tk),
        grid=(n // tm,),
        in_specs=in_specs,
        out_specs=[o[1] for o in outs],
        out_shape=[o[0] for o in outs],
        input_output_aliases=aliases,
        compiler_params=_cparams("arbitrary"),
        name="projection",
    )(*args)


LOG2E = 1.4426950408889634


def _suffix_ones(n):
    return jnp.where(_iota2((n, n), 0) >= _iota2((n, n), 1), 1.0, 0.0).astype(BF16)


def _softplus2(z):
    return jnp.maximum(z, 0.0) + jnp.log2(1.0 + jnp.exp2(-jnp.abs(z)))


def _sb_block(qs, kt_pair, v_pair, bias_col, ones_sfx, carry, mask):
    z = _dot(qs, kt_pair) + bias_col
    sp = _softplus2(z)
    if mask is not None:
        sp = jnp.where(mask, sp, 0.0)
    rest = _dot(sp.astype(BF16), ones_sfx)
    w = jnp.exp2(z - rest - carry)
    if mask is not None:
        w = jnp.where(mask, w, 0.0)
    return _dot(w.astype(BF16), v_pair), rest[:, 0:1]


def _attn_kernel(bias_ref, q_ref, kt_ref, v_ref, o_ref, qs_ref, acc_ref, *, tq, tk, layer):
    hp = pl.program_id(1)
    i = pl.program_id(2)
    ones_sfx = _suffix_ones(tk)
    r = tq // tk
    lane_head = _iota2((tq, 128), 1) // HEAD_DIM
    q = q_ref[...]
    qs_ref[0:tq, :] = jnp.where(lane_head == 0, q, jnp.zeros_like(q))
    qs_ref[tq:2 * tq, :] = jnp.where(lane_head == 1, q, jnp.zeros_like(q))
    bias0 = bias_ref[layer, 2 * hp] * LOG2E
    bias1 = bias_ref[layer, 2 * hp + 1] * LOG2E

    def kv(j):
        start = pl.multiple_of(j * tk, tk)
        return kt_ref[j], v_ref[pl.ds(start, tk), :]

    acc = jnp.zeros((2 * tq, 128), F32)
    tot = jnp.zeros((2 * tq, 1), F32)
    for d in reversed(range(r)):
        lo = d * tk
        nd = tq - lo
        qs_d = jnp.concatenate([qs_ref[lo:tq, :], qs_ref[tq + lo:2 * tq, :]], axis=0)
        tot_d = jnp.concatenate([tot[lo:tq], tot[tq + lo:2 * tq]], axis=0)
        bias_d = jnp.where(_iota2((2 * nd, 1), 0) < nd, bias0, bias1)
        mask = _iota2((2 * nd, tk), 1) < _iota2((2 * nd, tk), 0) % nd
        kt_p, v_p = kv(i * r + d)
        da, dt = _sb_block(qs_d, kt_p, v_p, bias_d, ones_sfx, tot_d, mask)
        pieces_a = [da[:nd], da[nd:]]
        pieces_t = [dt[:nd], dt[nd:]]
        if lo:
            pieces_a = [jnp.zeros((lo, 128), F32), da[:nd], jnp.zeros((lo, 128), F32), da[nd:]]
            pieces_t = [jnp.zeros((lo, 1), F32), dt[:nd], jnp.zeros((lo, 1), F32), dt[nd:]]
        acc = acc + jnp.concatenate(pieces_a, axis=0)
        tot = tot + jnp.concatenate(pieces_t, axis=0)
    acc_ref[...] = acc
    bias_col = jnp.where(_iota2((2 * tq, 1), 0) < tq, bias0, bias1)

    def body(it, tot):
        for k in range(r):
            kt_p, v_p = kv((i - it) * r - 1 - k)
            da, dt = _sb_block(qs_ref[...], kt_p, v_p, bias_col, ones_sfx, tot, None)
            acc_ref[...] += da
            tot = tot + dt
        return tot

    lax.fori_loop(0, i, body, tot)
    o_ref[...] = jnp.where(lane_head == 0, acc_ref[0:tq, :], acc_ref[tq:2 * tq, :])


def _prompt_attention(q, kt, vb, bias, *, batch, seq, tq, tk, layer):
    n = q.shape[0]
    nq = seq // tq
    return pl.pallas_call(
        functools.partial(_attn_kernel, tq=tq, tk=tk, layer=layer),
        grid=(batch, 2, nq),
        in_specs=[
            pl.BlockSpec(memory_space=pltpu.SMEM),
            pl.BlockSpec((tq, 128), lambda b, hp, i: (b * nq + i, hp)),
            pl.BlockSpec((seq // tk, 128, tk), lambda b, hp, i: (b, hp, 0)),
            pl.BlockSpec((seq, 128), lambda b, hp, i: (b, hp)),
        ],
        out_specs=pl.BlockSpec((tq, 128), lambda b, hp, i: (b * nq + i, hp)),
        out_shape=jax.ShapeDtypeStruct((n, 256), F32),
        scratch_shapes=[pltpu.VMEM((2 * tq, 128), BF16), pltpu.VMEM((2 * tq, 128), F32)],
        compiler_params=_cparams("arbitrary", "arbitrary", "arbitrary"),
        name="prompt_attention",
    )(bias, q, kt, vb)


def _lower_incl(n):
    return jnp.where(_iota2((n, n), 1) <= _iota2((n, n), 0), 1.0, 0.0).astype(BF16)


def _group_rms(x, gain, width):
    outs = []
    for g in range(x.shape[1] // width):
        xg = x[:, g * width:(g + 1) * width]
        ms = jnp.mean(xg * xg, axis=1, keepdims=True)
        outs.append(xg * lax.rsqrt(ms + EPS))
    return jnp.concatenate(outs, axis=1) * gain


def _ssd_kernel(zx_ref, cw_ref, cb_ref, dtb_ref, alog_ref, d_ref, ng_ref,
                o_ref, conv_ref, h_ref, tail_ref, hs_ref, *, t):
    cw_ref, cb_ref, dtb_ref, alog_ref, d_ref, ng_ref = _drop_layer_axis(
        cw_ref, cb_ref, dtb_ref, alog_ref, d_ref, ng_ref)
    i = pl.program_id(1)

    @pl.when(i == 0)
    def _():
        tail_ref[...] = jnp.zeros_like(tail_ref)
        hs_ref[...] = jnp.zeros_like(hs_ref)

    z = zx_ref[:, 0:256]
    xbc = zx_ref[:, 256:768]
    ext = jnp.concatenate([tail_ref[...], xbc], axis=0)
    conv = cb_ref[...]
    for j in range(4):
        conv = conv + ext[5 + j:5 + j + t] * cw_ref[j:j + 1, :]
    tail_ref[...] = xbc[t - 8:t]
    act = _silu(conv)
    xs, bm, cm = act[:, 0:256], act[:, 256:384], act[:, 384:512]
    dt = _softplus(zx_ref[:, 768:1024] + dtb_ref[...])
    da = dt * (-jnp.exp(alog_ref[...]))
    cum = _mm01_left(_lower_incl(t), da, 3)
    cum_t = cum.T
    xdt = xs * dt
    causal = _iota2((t, t), 1) <= _iota2((t, t), 0)
    bmb = bm.astype(BF16)
    cmb = cm.astype(BF16)
    gmat = [_dot_nt(cmb[:, g * 64:(g + 1) * 64], bmb[:, g * 64:(g + 1) * 64]) for g in range(2)]
    ys = []
    for h in range(4):
        hs = slice(h * 64, (h + 1) * 64)
        gs = slice((h // 2) * 64, (h // 2 + 1) * 64)
        cum_h = cum[:, hs]
        seg = cum[:, h * 64:h * 64 + 1] - cum_t[h * 64:h * 64 + 1, :]
        decay = jnp.exp(jnp.where(causal, seg, NEG_BIG))
        y_intra = _dot((gmat[h // 2] * decay).astype(BF16), xdt[:, hs].astype(BF16))
        last = cum_h[t - 1:t, :]
        h_prev = hs_ref[h]
        y_inter = _dot_nt((cm[:, gs] * jnp.exp(cum_h)).astype(BF16), h_prev.astype(BF16))
        s_c = _dot_tn((xdt[:, hs] * jnp.exp(last - cum_h)).astype(BF16), bmb[:, gs])
        hs_ref[h] = h_prev * jnp.exp(last) + s_c
        ys.append(y_intra + y_inter)
    y = jnp.concatenate(ys, axis=1) + d_ref[...] * xs
    o_ref[...] = _group_rms(y * _silu(z), ng_ref[...], 128)

    @pl.when(i == pl.num_programs(1) - 1)
    def _():
        conv_ref[0] = ext[t + 5:t + 8]
        h_ref[0] = hs_ref[...]


def _prompt_ssd(zx, cw, cb, dtb, alog, d_full, ng, *, batch, seq, t, layer):
    n = zx.shape[0]
    nt = seq // t
    return pl.pallas_call(
        functools.partial(_ssd_kernel, t=t),
        grid=(batch, nt),
        in_specs=[pl.BlockSpec((t, 1024), lambda b, i: (b * nt + i, 0))]
        + [_layer_spec(a, layer) for a in (cw, cb, dtb, alog, d_full, ng)],
        out_specs=[
            pl.BlockSpec((t, 256), lambda b, i: (b * nt + i, 0)),
            pl.BlockSpec((1, 3, 512), lambda b, i: (b, 0, 0)),
            pl.BlockSpec((1, 4, 64, 64), lambda b, i: (b, 0, 0, 0)),
        ],
        out_shape=[
            jax.ShapeDtypeStruct((n, 256), F32),
            jax.ShapeDtypeStruct((batch, 3, 512), F32),
            jax.ShapeDtypeStruct((batch, 4, 64, 64), F32),
        ],
        scratch_shapes=[pltpu.VMEM((8, 512), F32), pltpu.VMEM((4, 64, 64), F32)],
        compiler_params=_cparams("arbitrary", "arbitrary"),
        name="prompt_ssd",
    )(zx, cw, cb, dtb, alog, d_full, ng)


HG_CHUNK = 32


def _head_mean_matrix():
    same = (_iota2((256, 256), 0) // 64) == (_iota2((256, 256), 1) // 64)
    return jnp.where(same, 1.0, 0.0).astype(BF16)


def _head_rms(y, gain):
    ms = _mm01_right(y * y, _head_mean_matrix(), 2) * (1.0 / 64.0)
    return y * lax.rsqrt(ms + EPS) * gain


def _hgrn_kernel(hg_ref, lb_ref, ng_ref, o_ref, h_ref, st_ref, *, t):
    lb_ref, ng_ref = _drop_layer_axis(lb_ref, ng_ref)
    i = pl.program_id(1)
    c = HG_CHUNK

    @pl.when(i == 0)
    def _():
        st_ref[...] = jnp.zeros_like(st_ref)

    lb = lb_ref[...]
    q = _silu(hg_ref[:, 0:256])
    fr = hg_ref[:, 256:512]
    log_f = jnp.log(lb + (1.0 - lb) * _sigmoid(fr))
    k = (1.0 - lb) * _sigmoid(-fr)
    v = hg_ref[:, 512:768]
    same_chunk = (_iota2((t, t), 0) // c) == (_iota2((t, t), 1) // c)
    lmat = jnp.where(same_chunk & (_iota2((t, t), 1) <= _iota2((t, t), 0)), 1.0, 0.0).astype(BF16)
    b_all = _mm01_left(lmat, log_f, 3)
    lane_head = _iota2((1, 256), 1) // 64
    head_lane = [lane_head == h for h in range(4)]
    stack_causal = _iota2((4 * c, c), 1) <= (_iota2((4 * c, c), 0) % c)
    block_diag = (_iota2((256, 256), 0) // 64) == (_iota2((256, 256), 1) // 64)
    ys = []
    for n in range(t // c):
        rows = slice(n * c, (n + 1) * c)
        b = b_all[rows]
        ref = b[c // 2 - 1:c // 2]
        last = b[c - 1:c]
        q_c, k_c, v_c = q[rows], k[rows], v[rows]
        qe = q_c * jnp.exp(b - ref)
        ke = (k_c * jnp.exp(ref - b)).astype(BF16)
        kl = (k_c * jnp.exp(last - b)).astype(BF16)
        qb = (q_c * jnp.exp(b)).astype(BF16)
        vb = v_c.astype(BF16)
        q_stack = jnp.concatenate([jnp.where(m, qe, 0.0) for m in head_lane], axis=0).astype(BF16)
        scores = jnp.where(stack_causal, _dot_nt(q_stack, ke), 0.0)
        y4 = _dot(scores.astype(BF16), vb)
        y_intra = jnp.where(head_lane[0], y4[0:c], 0.0)
        for h in range(1, 4):
            y_intra = y_intra + jnp.where(head_lane[h], y4[h * c:(h + 1) * c], 0.0)
        st = st_ref[...]
        y_inter = _dot_nt(qb, st.astype(BF16))
        st_ref[...] = st * jnp.exp(last) + jnp.where(block_diag, _dot_tn(vb, kl), 0.0)
        ys.append(y_intra + y_inter)
    y = jnp.concatenate(ys, axis=0)
    o_ref[...] = _head_rms(y, ng_ref[...]) * _silu(hg_ref[:, 768:1024])

    @pl.when(i == pl.num_programs(1) - 1)
    def _():
        st_t = st_ref[...].T
        for h in range(4):
            h_ref[0, h] = st_t[h * 64:(h + 1) * 64, h * 64:(h + 1) * 64]


def _prompt_hgrn(hg4, lb, ng, *, batch, seq, t, layer):
    n = hg4.shape[0]
    nt = seq // t
    return pl.pallas_call(
        functools.partial(_hgrn_kernel, t=t),
        grid=(batch, nt),
        in_specs=[pl.BlockSpec((t, 1024), lambda b, i: (b * nt + i, 0)), _layer_spec(lb, layer), _layer_spec(ng, layer)],
        out_specs=[
            pl.BlockSpec((t, 256), lambda b, i: (b * nt + i, 0)),
            pl.BlockSpec((1, 4, 64, 64), lambda b, i: (b, 0, 0, 0)),
        ],
        out_shape=[jax.ShapeDtypeStruct((n, 256), F32), jax.ShapeDtypeStruct((batch, 4, 64, 64), F32)],
        scratch_shapes=[pltpu.VMEM((256, 256), F32)],
        compiler_params=_cparams("arbitrary", "arbitrary"),
        name="prompt_hgrn",
    )(hg4, lb, ng)


S5_ROW = 16
S5_GROUPS = 16
S5_STATE = 64
S5_SLAB = 128
S5_SEG = 4096
S5_LAM_ROWS = 32


def _s5_discretize(a_re, a_im, b_re, b_im, log_dt):
    dt = jnp.exp(log_dt)[:, None]
    mag = jnp.exp(a_re * dt)
    ab_re = mag * jnp.cos(a_im * dt)
    ab_im = mag * jnp.sin(a_im * dt)
    den = a_re * a_re + a_im * a_im
    coef_re = ((ab_re - 1.0) * a_re + ab_im * a_im) / den
    coef_im = (ab_im * a_re - (ab_re - 1.0) * a_im) / den
    bb_re = coef_re[..., None] * b_re - coef_im[..., None] * b_im
    bb_im = coef_re[..., None] * b_im + coef_im[..., None] * b_re
    return a_re * dt, a_im * dt, bb_re, bb_im


def _lam_pow(log_mag, arg, m):
    mag = jnp.exp(m * log_mag)
    return mag * jnp.cos(m * arg), mag * jnp.sin(m * arg)


def _s5_expand_b(bb):
    eye = jnp.eye(S5_GROUPS, dtype=F32)
    return jnp.einsum("gnk,gj->gkjn", bb, eye).reshape(S5_GROUPS * 16, S5_GROUPS * S5_STATE)


def _s5_expand_c(cc):
    eye = jnp.eye(S5_GROUPS, dtype=F32)
    return jnp.einsum("ghn,gj->gnjh", cc, eye).reshape(S5_GROUPS * S5_STATE, S5_GROUPS * 16)


def _s5_prompt_matrices(log_mag, arg, bb_re, bb_im, c_re, c_im):
    r, g, n = S5_ROW, S5_GROUPS, S5_STATE
    hp = lax.Precision.HIGHEST
    eye = jnp.eye(g, dtype=F32)
    taus = jnp.arange(r, dtype=F32)[:, None, None]
    pw_re, pw_im = _lam_pow(log_mag[None], arg[None], taus)
    p_re = pw_re[..., None] * bb_re[None] - pw_im[..., None] * bb_im[None]
    p_im = pw_re[..., None] * bb_im[None] + pw_im[..., None] * bb_re[None]
    kern = (jnp.einsum("ghn,tgnk->tghk", c_re, p_re, precision=hp)
            - jnp.einsum("ghn,tgnk->tghk", c_im, p_im, precision=hp))
    kf = jnp.einsum("tghk,gj->tgkjh", kern, eye).reshape(r, g * 16, g * 16)
    ns = g * n // S5_SLAB
    slab_cols = lambda m: m.reshape(g * 16, ns, S5_SLAB).transpose(1, 0, 2)
    slab_rows = lambda m: m.reshape(ns, S5_SLAB, g * 16)
    bf = jnp.concatenate([slab_cols(_s5_expand_b(bb_re)), slab_cols(_s5_expand_b(bb_im))], axis=2)
    cf = jnp.concatenate([slab_rows(_s5_expand_c(c_re)), -slab_rows(_s5_expand_c(c_im))], axis=1)
    ks = jnp.arange((S5_LAM_ROWS - 2) // 2, dtype=F32)
    mults = jnp.concatenate([jnp.ones((1,), F32), r * 2.0 ** ks])[:, None, None]
    l_re, l_im = _lam_pow(log_mag[None], arg[None], mults)
    tab = jnp.stack([l_re, l_im], axis=1).reshape(-1, g * n)
    lam = tab.reshape(tab.shape[0], ns, S5_SLAB).transpose(1, 0, 2)
    return kf.astype(BF16), bf.astype(BF16), cf.astype(BF16), lam


def _shift_rows(x, d):
    return jnp.concatenate([jnp.zeros((d, x.shape[1]), x.dtype), x[:x.shape[0] - d]], axis=0)


def _s5_kernel(u_ref, kf_ref, bf_ref, cf_ref, lam_ref, y_ref, xre_ref, xim_ref,
               ubf_ref, xprev_ref, xin_ref, *, seg):
    kf_ref, bf_ref, cf_ref, lam_ref = _drop_layer_axis(kf_ref, bf_ref, cf_ref, lam_ref)
    r = S5_ROW
    rows = seg // r
    ns = bf_ref.shape[0]
    sg = pl.program_id(1)

    @pl.when(sg == 0)
    def _():
        xin_ref[...] = jnp.zeros_like(xin_ref)

    for s in range(r):
        ubf_ref[s] = jnp.concatenate(
            [u_ref[0, pl.ds(s, rows, stride=r), :], u_ref[1, pl.ds(s, rows, stride=r), :]], axis=1).astype(BF16)

    first_row = _iota2((rows, S5_SLAB), 0) == 0
    for j in range(ns):
        l_re, l_im = lam_ref[j, 0:1, :], lam_ref[j, 1:2, :]
        x_re = jnp.zeros((rows, S5_SLAB), F32)
        x_im = jnp.zeros((rows, S5_SLAB), F32)
        for s in range(r):
            z = _dot(ubf_ref[s], bf_ref[j])
            x_re, x_im = (l_re * x_re - l_im * x_im + z[:, :S5_SLAB],
                          l_re * x_im + l_im * x_re + z[:, S5_SLAB:])
        in_re, in_im = xin_ref[0, j, 0:1, :], xin_ref[1, j, 0:1, :]
        m_re, m_im = lam_ref[j, 2:3, :], lam_ref[j, 3:4, :]
        x_re = x_re + jnp.where(first_row, m_re * in_re - m_im * in_im, 0.0)
        x_im = x_im + jnp.where(first_row, m_re * in_im + m_im * in_re, 0.0)
        d, k = 1, 0
        while d < rows:
            m_re, m_im = lam_ref[j, 2 + 2 * k:3 + 2 * k, :], lam_ref[j, 3 + 2 * k:4 + 2 * k, :]
            s_re, s_im = _shift_rows(x_re, d), _shift_rows(x_im, d)
            x_re, x_im = x_re + m_re * s_re - m_im * s_im, x_im + m_re * s_im + m_im * s_re
            d, k = 2 * d, k + 1
        xprev_ref[0, j] = jnp.concatenate([in_re, x_re[:rows - 1]], axis=0)
        xprev_ref[1, j] = jnp.concatenate([in_im, x_im[:rows - 1]], axis=0)
        xin_ref[0, j, 0:1, :] = x_re[rows - 1:rows]
        xin_ref[1, j, 0:1, :] = x_im[rows - 1:rows]

    for p in range(r):
        acc = _dot(ubf_ref[0], kf_ref[p])
        for s in range(1, p + 1):
            acc = acc + _dot(ubf_ref[s], kf_ref[p - s])
        for j in range(ns):
            l_re, l_im = lam_ref[j, 0:1, :], lam_ref[j, 1:2, :]
            x_re, x_im = xprev_ref[0, j], xprev_ref[1, j]
            x_re, x_im = l_re * x_re - l_im * x_im, l_re * x_im + l_im * x_re
            xprev_ref[0, j] = x_re
            xprev_ref[1, j] = x_im
            acc = acc + _dot(jnp.concatenate([x_re, x_im], axis=1).astype(BF16), cf_ref[j])
        y_ref[0, pl.ds(p, rows, stride=r), :] = acc[:, 0:128]
        y_ref[1, pl.ds(p, rows, stride=r), :] = acc[:, 128:256]

    @pl.when(sg == pl.num_programs(1) - 1)
    def _():
        xre_ref[0] = jnp.concatenate([xin_ref[0, j, 0:1, :] for j in range(ns)], axis=1)
        xim_ref[0] = jnp.concatenate([xin_ref[1, j, 0:1, :] for j in range(ns)], axis=1)


def _s5_prompt_scan(u_halves, mats, *, batch, seq, layer):
    g, n = S5_GROUPS, S5_STATE
    seg = min(S5_SEG, seq)
    nseg = seq // seg
    rows = seg // S5_ROW
    assert (rows - 1).bit_length() <= (S5_LAM_ROWS - 2) // 2
    ns = g * n // S5_SLAB
    full = lambda a: _layer_spec(a, layer)
    halves = pl.BlockSpec((2, seg, 128), lambda b, s: (0, b * nseg + s, 0))
    state = pl.BlockSpec((1, 1, g * n), lambda b, s: (b, 0, 0))
    y, x_re, x_im = pl.pallas_call(
        functools.partial(_s5_kernel, seg=seg),
        grid=(batch, nseg),
        in_specs=[halves] + [full(a) for a in mats],
        out_specs=[halves, state, state],
        out_shape=[jax.ShapeDtypeStruct((2, batch * seq, 128), F32),
                   jax.ShapeDtypeStruct((batch, 1, g * n), F32),
                   jax.ShapeDtypeStruct((batch, 1, g * n), F32)],
        scratch_shapes=[pltpu.VMEM((S5_ROW, rows, 256), BF16),
                        pltpu.VMEM((2, ns, rows, S5_SLAB), F32),
                        pltpu.VMEM((2, ns, 8, S5_SLAB), F32)],
        compiler_params=_cparams("arbitrary", "arbitrary"),
        name="prompt_s5",
    )(u_halves, *mats)
    return y, x_re.reshape(batch, g, n), x_im.reshape(batch, g, n)


D_FF = 2816
FFN_SPLIT = 2


def _layer_norm(x, g, b):
    mu = jnp.mean(x, axis=-1, keepdims=True)
    xc = x - mu
    var = jnp.mean(xc * xc, axis=-1, keepdims=True)
    return xc * lax.rsqrt(var + EPS) * g + b


def _post_kernel(*refs, t, decode, alpha):
    x_ref, oa_ref, ob_ref, yc_ref, u_ref, od_ref = refs[:6]
    (s5d_ref, wglu_ref, bglu_ref, wout_ref, ln1g_ref, ln1b_ref,
     wup_ref, fcw_ref, fcb_ref, wdown_ref, ln2g_ref, ln2b_ref) = _drop_layer_axis(*refs[6:18])
    if decode:
        st_ref, xo_ref, sto_ref = refs[18:]
    else:
        xo_ref, sto_ref, tail_ref = refs[18:]
        i = pl.program_id(1)

        @pl.when(i == 0)
        def _():
            tail_ref[...] = jnp.zeros_like(tail_ref)

    if decode:
        yc_pre, u = yc_ref[...], u_ref[...]
    else:
        yc_pre = jnp.concatenate([yc_ref[0], yc_ref[1]], axis=1)
        u = jnp.concatenate([u_ref[0], u_ref[1]], axis=1)
    yc = _gelu_tanh(yc_pre + s5d_ref[...] * u)
    oc = yc * _sigmoid(_dot(yc.astype(BF16), wglu_ref[...]) + bglu_ref[...])
    mix = _dot(oa_ref[...].astype(BF16), wout_ref[0:256, :])
    mix = mix + _dot(ob_ref[...].astype(BF16), wout_ref[256:512, :])
    mix = mix + _dot(oc.astype(BF16), wout_ref[512:768, :])
    mix = mix + _dot(od_ref[...].astype(BF16), wout_ref[768:1024, :])
    x1 = _layer_norm(alpha * x_ref[...] + mix, ln1g_ref[...], ln1b_ref[...])
    x1b = x1.astype(BF16)
    ffn = None
    width = D_FF // FFN_SPLIT
    for c in range(FFN_SPLIT):
        lo, hi = c * width, (c + 1) * width
        uh = _dot(x1b, wup_ref[:, lo:hi])
        gh = _dot(x1b, wup_ref[:, D_FF + lo:D_FF + hi])
        if decode:
            s0 = st_ref[:, lo:hi]
            s1 = st_ref[:, D_FF + lo:D_FF + hi]
            gconv = fcb_ref[:, lo:hi] + fcw_ref[0:1, lo:hi] * s0 + fcw_ref[1:2, lo:hi] * s1 + fcw_ref[2:3, lo:hi] * gh
            sto_ref[:, lo:hi] = s1
            sto_ref[:, D_FF + lo:D_FF + hi] = gh
        else:
            ext = jnp.concatenate([tail_ref[:, lo:hi], gh], axis=0)
            gconv = fcb_ref[:, lo:hi]
            for j in range(3):
                gconv = gconv + ext[6 + j:6 + j + t] * fcw_ref[j:j + 1, lo:hi]
            tail_ref[:, lo:hi] = gh[t - 8:t]

            @pl.when(i == pl.num_programs(1) - 1)
            def _():
                sto_ref[0, :, lo:hi] = ext[t + 6:t + 8]

        hmid = (_gelu_tanh(gconv) * uh).astype(BF16)
        part = _dot(hmid, wdown_ref[lo:hi, :])
        ffn = part if ffn is None else ffn + part
    xo_ref[...] = _layer_norm(alpha * x1 + ffn, ln2g_ref[...], ln2b_ref[...])


def _post(x, oa, ob, yc, u, od, weights, *, alpha, layer, batch=None, seq=None, t=None, ffn_state=None):
    decode = ffn_state is not None
    n = x.shape[0]
    if decode:
        t = n
        grid = (1,)
        row = lambda i: (0, 0)
        extra_in = [pl.BlockSpec(ffn_state.shape, row)]
        extra_args = [ffn_state]
        out_specs = [pl.BlockSpec((t, 1024), row), pl.BlockSpec((t, 2 * D_FF), row)]
        out_shape = [jax.ShapeDtypeStruct((n, 1024), F32), jax.ShapeDtypeStruct((n, 2 * D_FF), F32)]
        scratch = []
        sem = ("arbitrary",)
    else:
        nt = seq // t
        grid = (batch, nt)
        row = lambda b, i: (b * nt + i, 0)
        extra_in, extra_args = [], []
        out_specs = [pl.BlockSpec((t, 1024), row), pl.BlockSpec((1, 2, D_FF), lambda b, i: (b, 0, 0))]
        out_shape = [jax.ShapeDtypeStruct((n, 1024), F32), jax.ShapeDtypeStruct((batch, 2, D_FF), F32)]
        scratch = [pltpu.VMEM((8, D_FF), F32)]
        sem = ("arbitrary", "arbitrary")
    acts = (x, oa, ob, yc, u, od)

    def act_spec(a):
        if a.ndim == 3:
            return pl.BlockSpec((2, t, 128), lambda b, i: (0, b * nt + i, 0))
        return pl.BlockSpec((t, a.shape[1]), row)

    return pl.pallas_call(
        functools.partial(_post_kernel, t=t, decode=decode, alpha=alpha),
        grid=grid,
        in_specs=[act_spec(a) for a in acts]
        + [_layer_spec(w, layer, pipeline_mode=pl.Buffered(1)) for w in weights] + extra_in,
        out_specs=out_specs,
        out_shape=out_shape,
        scratch_shapes=scratch,
        compiler_params=_cparams(*sem),
        name="post_decode" if decode else "post_prompt",
    )(*acts, *weights, *extra_args)


PAGE = 128
PAGES_PER_STEP = 32


def _decode_attn_kernel(pt_ref, q_ref, bias_ref, *refs, pp):
    del pt_ref
    (bias_ref,) = _drop_layer_axis(bias_ref)
    k_refs, v_refs = refs[:pp], refs[pp:2 * pp]
    o_ref, qb_ref, c_ref, acc_ref = refs[2 * pp:]
    s = pl.program_id(1)
    eye = _iota2((256, 256), 0) == _iota2((256, 256), 1)

    @pl.when(s == 0)
    def _():
        q_col = jnp.sum(jnp.where(eye, q_ref[0], 0.0), axis=1, keepdims=True)
        qb_ref[...] = jnp.broadcast_to(q_col, (256, PAGE))
        c_ref[...] = jnp.zeros_like(c_ref)
        acc_ref[...] = jnp.zeros_like(acc_ref)

    ones_sfx = _suffix_ones(PAGE)
    qb = qb_ref[...]
    c = c_ref[...]
    acc = acc_ref[...]
    for r in reversed(range(pp)):
        prod = k_refs[r][0] * qb
        z = jnp.concatenate(
            [jnp.sum(prod[h * HEAD_DIM:(h + 1) * HEAD_DIM], axis=0, keepdims=True) for h in range(4)]
            + [jnp.zeros((4, PAGE), F32)], axis=0) + bias_ref[...]
        sp = _softplus2(z)
        rest = _dot(sp.astype(BF16), ones_sfx)
        w = jnp.exp2(z - rest - c)
        v_t = v_refs[r][0]
        acc = acc + jnp.concatenate(
            [v_t[h * HEAD_DIM:(h + 1) * HEAD_DIM] * w[h:h + 1, :] for h in range(4)], axis=0)
        c = c + rest[:, 0:1]
    c_ref[...] = c
    acc_ref[...] = acc

    @pl.when(s == pl.num_programs(1) - 1)
    def _():
        o_col = jnp.sum(acc, axis=1, keepdims=True)
        o_ref[0] = jnp.sum(jnp.where(eye, o_col, 0.0), axis=0, keepdims=True)


def _decode_attention(q, pool_kt, pool_vt, page_table, bias8, *, base, layer):
    nseq, npages = page_table.shape
    pp = PAGES_PER_STEP
    nsteps = npages // pp

    def page_map(r):
        return lambda b, s, pt: (base + pt[b, (nsteps - 1 - s) * pp + r], 0, 0)

    page_specs = [pl.BlockSpec((1, 256, PAGE), page_map(r)) for r in range(pp)]
    out = pl.pallas_call(
        functools.partial(_decode_attn_kernel, pp=pp),
        grid_spec=pltpu.PrefetchScalarGridSpec(
            num_scalar_prefetch=1,
            grid=(nseq, nsteps),
            in_specs=[pl.BlockSpec((1, 1, 256), lambda b, s, pt: (b, 0, 0)),
                      _layer_spec(bias8, layer)] + page_specs + page_specs,
            out_specs=pl.BlockSpec((1, 1, 256), lambda b, s, pt: (b, 0, 0)),
            scratch_shapes=[pltpu.VMEM((256, PAGE), F32), pltpu.VMEM((8, PAGE), F32), pltpu.VMEM((256, PAGE), F32)],
        ),
        out_shape=jax.ShapeDtypeStruct((nseq, 1, 256), F32),
        compiler_params=_cparams("arbitrary", "arbitrary"),
        name="decode_attention",
    )(page_table, q.reshape(nseq, 1, 256), bias8, *([pool_kt] * pp), *([pool_vt] * pp))
    return out.reshape(nseq, 256)


def _dot_f32(a, b):
    a0, a1, a2 = _split3(a)
    b0, b1, b2 = _split3(b)
    return (_dot(a0, b0) + (_dot(a0, b1) + _dot(a1, b0))
            + (_dot(a0, b2) + _dot(a2, b0) + _dot(a1, b1)))


_R_DEC, _R_XDT, _R_BM, _R_CM, _R_F, _R_K, _R_Q, _R_V, _R_END = 0, 256, 512, 640, 768, 1024, 1280, 1536, 1792


def _decode_mixers_kernel(zx_ref, u_ref, hg_ref, cs_ref, hssm_ref, x0re_ref, x0im_ref, hhg_ref,
                          cw_ref, cb_ref, dtb_ref, alog_ref, d_ref, ngs_ref, lb_ref, ngh_ref,
                          lre_ref, lim_ref, bre_ref, bim_ref, cre_ref, cim_ref,
                          ob_ref, cso_ref, hssmo_ref, yc_ref, xre_ref, xim_ref, od_ref, hhgo_ref,
                          rows_ref, yssm_ref, yhg_ref, *, nseq):
    (cw_ref, cb_ref, dtb_ref, alog_ref, d_ref, ngs_ref, lb_ref, ngh_ref,
     lre_ref, lim_ref, bre_ref, bim_ref, cre_ref, cim_ref) = _drop_layer_axis(
        cw_ref, cb_ref, dtb_ref, alog_ref, d_ref, ngs_ref, lb_ref, ngh_ref,
        lre_ref, lim_ref, bre_ref, bim_ref, cre_ref, cim_ref)
    z = zx_ref[:, 0:256]
    xbc = zx_ref[:, 256:768]
    s0, s1, s2 = cs_ref[:, 0:512], cs_ref[:, 512:1024], cs_ref[:, 1024:1536]
    conv = cb_ref[...] + cw_ref[0:1, :] * s0 + cw_ref[1:2, :] * s1 + cw_ref[2:3, :] * s2 + cw_ref[3:4, :] * xbc
    cso_ref[:, 0:512] = s1
    cso_ref[:, 512:1024] = s2
    cso_ref[:, 1024:1536] = xbc
    act = _silu(conv)
    xs = act[:, 0:256]
    dt = _softplus(zx_ref[:, 768:1024] + dtb_ref[...])
    rows_ref[:, _R_DEC:_R_XDT] = jnp.exp(dt * (-jnp.exp(alog_ref[...])))
    rows_ref[:, _R_XDT:_R_BM] = xs * dt
    rows_ref[:, _R_BM:_R_F] = act[:, 256:512]
    lb = lb_ref[...]
    fr = hg_ref[:, 256:512]
    rows_ref[:, _R_F:_R_K] = lb + (1.0 - lb) * _sigmoid(fr)
    rows_ref[:, _R_K:_R_Q] = (1.0 - lb) * _sigmoid(-fr)
    rows_ref[:, _R_Q:_R_V] = _silu(hg_ref[:, 0:256])
    rows_ref[:, _R_V:_R_END] = hg_ref[:, 512:768]

    eye = _iota2((256, 256), 0) == _iota2((256, 256), 1)
    row_id = _iota2((256, 64), 0)

    def to_col(r):
        return jnp.sum(jnp.where(eye, r, 0.0), axis=1, keepdims=True)

    def to_row(c):
        return jnp.sum(jnp.where(eye, c, 0.0), axis=0, keepdims=True)

    def per_seq(b, get):
        bm = get(_R_BM, _R_CM)
        cm = get(_R_CM, _R_F)
        bm_rows = jnp.where(row_id < 128, bm[:, 0:64], bm[:, 64:128])
        cm_rows = jnp.where(row_id < 128, cm[:, 0:64], cm[:, 64:128])
        hn = hssm_ref[b] * to_col(get(_R_DEC, _R_XDT)) + to_col(get(_R_XDT, _R_BM)) * bm_rows
        hssmo_ref[b] = hn
        y_ssm = to_row(jnp.sum(hn * cm_rows, axis=1, keepdims=True))
        v = get(_R_V, _R_END)
        v_rows = jnp.where(row_id < 64, v[:, 0:64],
                           jnp.where(row_id < 128, v[:, 64:128],
                                     jnp.where(row_id < 192, v[:, 128:192], v[:, 192:256])))
        gn = hhg_ref[b] * to_col(get(_R_F, _R_K)) + to_col(get(_R_K, _R_Q)) * v_rows
        hhgo_ref[b] = gn
        qg = to_col(get(_R_Q, _R_V)) * gn
        y_hg = jnp.concatenate(
            [jnp.sum(qg[h * 64:(h + 1) * 64], axis=0, keepdims=True) for h in range(4)], axis=1)
        return y_ssm, y_hg

    def per_octet(o, carry):
        base = pl.multiple_of(o * 8, 8)
        blk = rows_ref[pl.ds(base, 8), :]
        ys = [per_seq(base + r, lambda lo, hi, r=r: blk[r:r + 1, lo:hi]) for r in range(8)]
        yssm_ref[pl.ds(base, 8), :] = jnp.concatenate([y[0] for y in ys], axis=0)
        yhg_ref[pl.ds(base, 8), :] = jnp.concatenate([y[1] for y in ys], axis=0)
        return carry

    lax.fori_loop(0, nseq // 8, per_octet, 0)

    y = yssm_ref[...] + d_ref[...] * xs
    ob_ref[...] = _group_rms(y * _silu(z), ngs_ref[...], 128)
    od_ref[...] = _head_rms(yhg_ref[...], ngh_ref[...]) * _silu(hg_ref[:, 768:1024])
    u = u_ref[...]
    x0r, x0i = x0re_ref[...], x0im_ref[...]
    lr, li = lre_ref[...], lim_ref[...]
    xr = lr * x0r - li * x0i + _dot_f32(u, bre_ref[...])
    xi = lr * x0i + li * x0r + _dot_f32(u, bim_ref[...])
    xre_ref[...] = xr
    xim_ref[...] = xi
    yc_ref[...] = _dot_f32(xr, cre_ref[...]) - _dot_f32(xi, cim_ref[...])


def _decode_mixers(zx, u, hg4, conv_state, h_ssm, x0_re, x0_im, h_hg, params, *, layer):
    nseq = zx.shape[0]
    acts = (zx, u, hg4, conv_state, h_ssm, x0_re, x0_im, h_hg)
    args = (*acts, *params)
    full = lambda a: pl.BlockSpec(a.shape, lambda i: (0,) * a.ndim)
    out_shape = [
        jax.ShapeDtypeStruct((nseq, 256), F32),
        jax.ShapeDtypeStruct((nseq, 1536), F32),
        jax.ShapeDtypeStruct((nseq, 256, 64), F32),
        jax.ShapeDtypeStruct((nseq, 256), F32),
        jax.ShapeDtypeStruct((nseq, 1024), F32),
        jax.ShapeDtypeStruct((nseq, 1024), F32),
        jax.ShapeDtypeStruct((nseq, 256), F32),
        jax.ShapeDtypeStruct((nseq, 256, 64), F32),
    ]
    return pl.pallas_call(
        functools.partial(_decode_mixers_kernel, nseq=nseq),
        grid=(1,),
        in_specs=[full(a) for a in acts] + [_layer_spec(p, layer) for p in params],
        out_specs=[full(s) for s in out_shape],
        out_shape=out_shape,
        scratch_shapes=[pltpu.VMEM((nseq, _R_END), F32), pltpu.VMEM((nseq, 256), F32), pltpu.VMEM((nseq, 256), F32)],
        compiler_params=_cparams("arbitrary"),
        name="decode_mixers",
    )(*args)


def _s5_decode_matrices(log_mag, arg, bb_re, bb_im, c_re, c_im):
    g, n = S5_GROUPS, S5_STATE
    lam_re, lam_im = _lam_pow(log_mag, arg, 1.0)
    return (lam_re.reshape(1, g * n), lam_im.reshape(1, g * n),
            _s5_expand_b(bb_re), _s5_expand_b(bb_im), _s5_expand_c(c_re), _s5_expand_c(c_im))


ATTN_TQ = 1024
ATTN_TK = 256
PROJ_TILE = 512
SSD_TILE = 256
HGRN_TILE = 256
POST_TILE = 512


def _rearranged_w_in(w_in):
    q, k, v, z, xbc, dt, u, hq, hf, hi, hg = jnp.split(
        w_in, [256, 512, 768, 1024, 1536, 1540, 1796, 2052, 2308, 2564], axis=1)
    dt_full = jnp.repeat(dt, HEAD_DIM, axis=1)
    q_scale = LOG2E * HEAD_DIM ** -0.5
    return jnp.concatenate([q * q_scale, k, v, z, xbc, dt_full, u, hq, hf, hi, hg], axis=1).astype(BF16)


def kernel(x_prompt, x_sample, cache_k, cache_v, state_ssm_conv, state_ssm, state_s5_re, state_s5_im, state_hgrn, state_ffn_conv, page_table, ln1_g, ln1_b, ln2_g, ln2_b, w_in, w_out, sb_logit_bias, ssm_conv_w, ssm_conv_b, ssm_dt_bias, ssm_a_log, ssm_d, ssm_norm_g, s5_a_re, s5_a_im, s5_b_re, s5_b_im, s5_c_re, s5_c_im, s5_d, s5_log_dt, s5_w_glu, s5_b_glu, hg_lb_logits, hg_norm_g, w_up, ffn_conv_w, ffn_conv_b, w_down):
    depth = w_in.shape[0]
    bp, seq, dm = x_prompt.shape
    ns = x_sample.shape[0]
    n_phys = cache_k.shape[1]
    alpha = (2 * depth) ** 0.25
    row = lambda a: a[:, None, :]
    rep = lambda a: jnp.repeat(a, HEAD_DIM, axis=1)[:, None, :]
    pr = jax.nn.softmax(hg_lb_logits.astype(F32), axis=0)
    lb = row(jnp.cumsum(pr, axis=0) - pr[0:1])
    ngh = row(hg_norm_g)
    w_p = jax.vmap(_rearranged_w_in)(w_in)
    disc = jax.vmap(_s5_discretize)(s5_a_re, s5_a_im, s5_b_re, s5_b_im, s5_log_dt)
    s5_mats = jax.vmap(_s5_prompt_matrices)(*disc, s5_c_re, s5_c_im)
    ssd_w = (ssm_conv_w, row(ssm_conv_b), rep(ssm_dt_bias), rep(ssm_a_log), rep(ssm_d), row(ssm_norm_g))
    post_w = (row(s5_d), s5_w_glu.astype(BF16), row(s5_b_glu), w_out.astype(BF16), row(ln1_g), row(ln1_b),
              w_up.astype(BF16), ffn_conv_w, row(ffn_conv_b), w_down.astype(BF16), row(ln2_g), row(ln2_b))
    dec_w = (*ssd_w, lb, ngh, *jax.vmap(_s5_decode_matrices)(*disc, s5_c_re, s5_c_im))
    bias8 = jnp.broadcast_to(jnp.pad(sb_logit_bias * LOG2E, ((0, 0), (0, 4)))[:, :, None], (depth, 8, PAGE))
    pool_kt = cache_k.transpose(0, 1, 3, 4, 2).reshape(depth * n_phys, 256, PAGE)
    pool_vt = cache_v.transpose(0, 1, 3, 4, 2).reshape(depth * n_phys, 256, PAGE)

    xp = x_prompt.reshape(bp * seq, dm)
    xs = x_sample.reshape(ns, dm)
    k_all = jnp.zeros((depth, bp, 256, seq), F32)
    v_all = jnp.zeros((depth, bp, 256, seq), F32)
    outs_p, outs_s = [], []
    for l in range(depth):
        q, vb, zx, u, hg4, kt, k_all, v_all = _projection(
            xp, w_p, tm=PROJ_TILE, tk=ATTN_TK, seq=seq, layer=l, k_all=k_all, v_all=v_all)
        oa = _prompt_attention(q, kt, vb, sb_logit_bias, batch=bp, seq=seq, tq=ATTN_TQ, tk=ATTN_TK, layer=l)
        ob, conv_p, ssm_p = _prompt_ssd(zx, *ssd_w, batch=bp, seq=seq, t=SSD_TILE, layer=l)
        yc, re_p, im_p = _s5_prompt_scan(u, s5_mats, batch=bp, seq=seq, layer=l)
        od, hg_p = _prompt_hgrn(hg4, lb, ngh, batch=bp, seq=seq, t=HGRN_TILE, layer=l)
        xp, ffn_p = _post(xp, oa, ob, yc, u, od, post_w, alpha=alpha, layer=l, batch=bp, seq=seq, t=POST_TILE)
        outs_p.append((conv_p, ssm_p, re_p, im_p, hg_p, ffn_p))

        q, k, v, zx, u, hg4 = _projection(xs, w_p, tm=ns, layer=l)
        oa = _decode_attention(q, pool_kt, pool_vt, page_table, bias8, base=l * n_phys, layer=l)
        ob, conv_s, ssm_s, yc, re_s, im_s, od, hg_s = _decode_mixers(
            zx, u, hg4, state_ssm_conv[l].reshape(ns, 1536), state_ssm[l].reshape(ns, 256, 64),
            state_s5_re[l].reshape(ns, 1024), state_s5_im[l].reshape(ns, 1024),
            state_hgrn[l].reshape(ns, 256, 64), dec_w, layer=l)
        xs, ffn_s = _post(xs, oa, ob, yc, u, od, post_w, alpha=alpha, layer=l,
                          ffn_state=state_ffn_conv[l].reshape(ns, 2 * D_FF))
        outs_s.append((k.reshape(ns, 1, 4, HEAD_DIM), v.reshape(ns, 1, 4, HEAD_DIM),
                       conv_s.reshape(ns, 3, 512), ssm_s.reshape(ns, 4, 64, 64),
                       re_s.reshape(ns, S5_GROUPS, S5_STATE), im_s.reshape(ns, S5_GROUPS, S5_STATE),
                       hg_s.reshape(ns, 4, 64, 64), ffn_s.reshape(ns, 2, D_FF)))

    sp = [jnp.stack(col, axis=0) for col in zip(*outs_p)]
    ss = [jnp.stack(col, axis=0) for col in zip(*outs_s)]
    k_prompt = k_all.reshape(depth, bp, 4, HEAD_DIM, seq).transpose(0, 1, 4, 2, 3)
    v_prompt = v_all.reshape(depth, bp, 4, HEAD_DIM, seq).transpose(0, 1, 4, 2, 3)
    return (xp.reshape(bp, seq, dm), xs.reshape(ns, 1, dm), k_prompt, v_prompt, ss[0], ss[1], sp[0], ss[2],
            sp[1], ss[3], sp[2], ss[4], sp[3], ss[5], sp[4], ss[6], sp[5], ss[7])
```

```python
import functools
import math

import jax
import jax.numpy as jnp
from jax import lax
from jax.experimental import pallas as pl
from jax.experimental.pallas import tpu as pltpu

F32 = jnp.float32
BF16 = jnp.bfloat16

HEAD_DIM = 64
GROUP_WIDTH = 256
EPS = 1e-5
NEG_BIG = -1e30

VMEM_LIMIT_BYTES = 56 * 1024 * 1024


def _cparams(*sem):
    return pltpu.CompilerParams(dimension_semantics=sem, vmem_limit_bytes=VMEM_LIMIT_BYTES)


def _layer_spec(a, layer, **kw):
    return pl.BlockSpec((1,) + a.shape[1:], lambda *_: (layer,) + (0,) * (a.ndim - 1), **kw)


def _drop_layer_axis(*refs):
    return tuple(r.at[0] for r in refs)


def _dot(a, b):
    return jnp.dot(a, b, preferred_element_type=F32)


def _dot_nt(a, b):
    return lax.dot_general(a, b, (((1,), (1,)), ((), ())), preferred_element_type=F32)


def _dot_tn(a, b):
    return lax.dot_general(a, b, (((0,), (0,)), ((), ())), preferred_element_type=F32)


def _split2(x):
    hi = x.astype(BF16)
    lo = (x - hi.astype(F32)).astype(BF16)
    return hi, lo


def _split3(x):
    hi = x.astype(BF16)
    r = x - hi.astype(F32)
    mid = r.astype(BF16)
    lo = (r - mid.astype(F32)).astype(BF16)
    return hi, mid, lo


def _mm01_left(m01, x, passes):
    parts = _split3(x) if passes == 3 else _split2(x)
    acc = _dot(m01, parts[0])
    for p in parts[1:]:
        acc = acc + _dot(m01, p)
    return acc


def _mm01_right(x, m01, passes):
    parts = _split3(x) if passes == 3 else _split2(x)
    acc = _dot(parts[0], m01)
    for p in parts[1:]:
        acc = acc + _dot(p, m01)
    return acc


def _sigmoid(x):
    return 1.0 / (1.0 + jnp.exp(-x))


def _silu(x):
    return x * _sigmoid(x)


def _softplus(x):
    return jnp.maximum(x, 0.0) + jnp.log(1.0 + jnp.exp(-jnp.abs(x)))


def _log_sigmoid_neg(z):
    return jnp.minimum(-z, 0.0) - jnp.log(1.0 + jnp.exp(-jnp.abs(z)))


def _gelu_tanh(x):
    c = math.sqrt(2.0 / math.pi)
    return 0.5 * x * (1.0 + jnp.tanh(c * (x + 0.044715 * (x * x * x))))


def _iota2(shape, dim):
    return lax.broadcasted_iota(jnp.int32, shape, dim)


PROJ_W = 3072


def _proj_kernel(x_ref, w_ref, *refs, tk):
    (w_ref,) = _drop_layer_axis(w_ref)
    x = x_ref[...].astype(BF16)

    def mm(lo, hi):
        return _dot(x, w_ref[:, lo:hi])

    q = mm(0, 256)
    k = mm(256, 512)
    v = mm(512, 768)
    u = mm(1792, 2048)
    if tk:
        _, _, q_ref, vb_ref, zx_ref, u_ref, hg_ref, kt_ref, kall_ref, vall_ref = refs
        u_ref[0] = u[:, 0:128]
        u_ref[1] = u[:, 128:256]
        q_ref[...] = q.astype(BF16)
        vb_ref[...] = v.astype(BF16)
        k_t = k.T
        kall_ref[0, 0] = k_t
        vall_ref[0, 0] = v.T
        for c in range(k.shape[0] // tk):
            kt_ref[c] = k_t[:, c * tk:(c + 1) * tk].astype(BF16)
    else:
        q_ref, k_ref, v_ref, zx_ref, u_ref, hg_ref = refs
        q_ref[...] = q
        k_ref[...] = k
        v_ref[...] = v
        u_ref[...] = u
    zx_ref[...] = mm(768, 1792)
    hg_ref[...] = mm(2048, 3072)


def _projection(x2d, w_p, *, tm, tk=0, seq=None, layer=None, k_all=None, v_all=None):
    n, d = x2d.shape
    row = lambda i: (i, 0)
    rows = lambda width, dtype: (jax.ShapeDtypeStruct((n, width), dtype), pl.BlockSpec((tm, width), row))
    in_specs = [pl.BlockSpec((tm, d), row), _layer_spec(w_p, layer)]
    args = [x2d, w_p]
    aliases = {}
    if tk:
        nt = seq // tm
        slab = pl.BlockSpec((1, 1, 256, tm), lambda i: (layer, i // nt, 0, i % nt))
        halves = (jax.ShapeDtypeStruct((2, n, 128), F32), pl.BlockSpec((2, tm, 128), lambda i: (0, i, 0)))
        outs = [rows(256, BF16), rows(256, BF16), rows(1024, F32), halves, rows(1024, F32),
                (jax.ShapeDtypeStruct((n // tk, 256, tk), BF16), pl.BlockSpec((tm // tk, 256, tk), lambda i: (i, 0, 0))),
                (jax.ShapeDtypeStruct(k_all.shape, F32), slab), (jax.ShapeDtypeStruct(v_all.shape, F32), slab)]
        in_specs += [pl.BlockSpec(memory_space=pl.ANY), pl.BlockSpec(memory_space=pl.ANY)]
        args += [k_all, v_all]
        aliases = {2: 6, 3: 7}
    else:
        outs = [rows(256, F32), rows(256, F32), rows(256, F32), rows(1024, F32), rows(256, F32), rows(1024, F32)]
    return pl.pallas_call(
        functools.partial(_proj_kernel, tk=tk),
        grid=(n // tm,),
        in_specs=in_specs,
        out_specs=[o[1] for o in outs],
        out_shape=[o[0] for o in outs],
        input_output_aliases=aliases,
        compiler_params=_cparams("arbitrary"),
        name="projection",
    )(*args)


LOG2E = 1.4426950408889634


def _suffix_ones(n):
    return jnp.where(_iota2((n, n), 0) >= _iota2((n, n), 1), 1.0, 0.0).astype(BF16)


def _softplus2(z):
    return jnp.maximum(z, 0.0) + jnp.log2(1.0 + jnp.exp2(-jnp.abs(z)))


def _sb_block(qs, kt_pair, v_pair, bias_col, ones_sfx, carry, mask):
    z = _dot(qs, kt_pair) + bias_col
    sp = _softplus2(z)
    if mask is not None:
        sp = jnp.where(mask, sp, 0.0)
    rest = _dot(sp.astype(BF16), ones_sfx)
    w = jnp.exp2(z - rest - carry)
    if mask is not None:
        w = jnp.where(mask, w, 0.0)
    return _dot(w.astype(BF16), v_pair), rest[:, 0:1]


def _attn_kernel(bias_ref, q_ref, kt_ref, v_ref, o_ref, qs_ref, acc_ref, *, tq, tk, layer):
    hp = pl.program_id(1)
    i = pl.program_id(2)
    ones_sfx = _suffix_ones(tk)
    r = tq // tk
    lane_head = _iota2((tq, 128), 1) // HEAD_DIM
    q = q_ref[...]
    qs_ref[0:tq, :] = jnp.where(lane_head == 0, q, jnp.zeros_like(q))
    qs_ref[tq:2 * tq, :] = jnp.where(lane_head == 1, q, jnp.zeros_like(q))
    bias0 = bias_ref[layer, 2 * hp] * LOG2E
    bias1 = bias_ref[layer, 2 * hp + 1] * LOG2E

    def kv(j):
        start = pl.multiple_of(j * tk, tk)
        return kt_ref[j], v_ref[pl.ds(start, tk), :]

    acc = jnp.zeros((2 * tq, 128), F32)
    tot = jnp.zeros((2 * tq, 1), F32)
    for d in reversed(range(r)):
        lo = d * tk
        nd = tq - lo
        qs_d = jnp.concatenate([qs_ref[lo:tq, :], qs_ref[tq + lo:2 * tq, :]], axis=0)
        tot_d = jnp.concatenate([tot[lo:tq], tot[tq + lo:2 * tq]], axis=0)
        bias_d = jnp.where(_iota2((2 * nd, 1), 0) < nd, bias0, bias1)
        mask = _iota2((2 * nd, tk), 1) < _iota2((2 * nd, tk), 0) % nd
        kt_p, v_p = kv(i * r + d)
        da, dt = _sb_block(qs_d, kt_p, v_p, bias_d, ones_sfx, tot_d, mask)
        pieces_a = [da[:nd], da[nd:]]
        pieces_t = [dt[:nd], dt[nd:]]
        if lo:
            pieces_a = [jnp.zeros((lo, 128), F32), da[:nd], jnp.zeros((lo, 128), F32), da[nd:]]
            pieces_t = [jnp.zeros((lo, 1), F32), dt[:nd], jnp.zeros((lo, 1), F32), dt[nd:]]
        acc = acc + jnp.concatenate(pieces_a, axis=0)
        tot = tot + jnp.concatenate(pieces_t, axis=0)
    acc_ref[...] = acc
    bias_col = jnp.where(_iota2((2 * tq, 1), 0) < tq, bias0, bias1)

    def body(it, tot):
        for k in range(r):
            kt_p, v_p = kv((i - it) * r - 1 - k)
            da, dt = _sb_block(qs_ref[...], kt_p, v_p, bias_col, ones_sfx, tot, None)
            acc_ref[...] += da
            tot = tot + dt
        return tot

    lax.fori_loop(0, i, body, tot)
    o_ref[...] = jnp.where(lane_head == 0, acc_ref[0:tq, :], acc_ref[tq:2 * tq, :])


def _prompt_attention(q, kt, vb, bias, *, batch, seq, tq, tk, layer):
    n = q.shape[0]
    nq = seq // tq
    return pl.pallas_call(
        functools.partial(_attn_kernel, tq=tq, tk=tk, layer=layer),
        grid=(batch, 2, nq),
        in_specs=[
            pl.BlockSpec(memory_space=pltpu.SMEM),
            pl.BlockSpec((tq, 128), lambda b, hp, i: (b * nq + i, hp)),
            pl.BlockSpec((seq // tk, 128, tk), lambda b, hp, i: (b, hp, 0)),
            pl.BlockSpec((seq, 128), lambda b, hp, i: (b, hp)),
        ],
        out_specs=pl.BlockSpec((tq, 128), lambda b, hp, i: (b * nq + i, hp)),
        out_shape=jax.ShapeDtypeStruct((n, 256), F32),
        scratch_shapes=[pltpu.VMEM((2 * tq, 128), BF16), pltpu.VMEM((2 * tq, 128), F32)],
        compiler_params=_cparams("arbitrary", "arbitrary", "arbitrary"),
        name="prompt_attention",
    )(bias, q, kt, vb)


def _lower_incl(n):
    return jnp.where(_iota2((n, n), 1) <= _iota2((n, n), 0), 1.0, 0.0).astype(BF16)


def _group_rms(x, gain, width):
    outs = []
    for g in range(x.shape[1] // width):
        xg = x[:, g * width:(g + 1) * width]
        ms = jnp.mean(xg * xg, axis=1, keepdims=True)
        outs.append(xg * lax.rsqrt(ms + EPS))
    return jnp.concatenate(outs, axis=1) * gain


def _ssd_kernel(zx_ref, cw_ref, cb_ref, dtb_ref, alog_ref, d_ref, ng_ref,
                o_ref, conv_ref, h_ref, tail_ref, hs_ref, *, t, nb):
    cw_ref, cb_ref, dtb_ref, alog_ref, d_ref, ng_ref = _drop_layer_axis(
        cw_ref, cb_ref, dtb_ref, alog_ref, d_ref, ng_ref)
    i = pl.program_id(0)

    @pl.when(i == 0)
    def _():
        tail_ref[...] = jnp.zeros_like(tail_ref)
        hs_ref[...] = jnp.zeros_like(hs_ref)

    causal = _iota2((t, t), 1) <= _iota2((t, t), 0)
    lower = _lower_incl(t)
    outs, exts = [], []
    for s in range(nb):
        z = zx_ref[s, :, 0:256]
        xbc = zx_ref[s, :, 256:768]
        ext = jnp.concatenate([tail_ref[s], xbc], axis=0)
        exts.append(ext)
        conv = cb_ref[...]
        for j in range(4):
            conv = conv + ext[5 + j:5 + j + t] * cw_ref[j:j + 1, :]
        tail_ref[s] = xbc[t - 8:t]
        act = _silu(conv)
        xs, bm, cm = act[:, 0:256], act[:, 256:384], act[:, 384:512]
        dt = _softplus(zx_ref[s, :, 768:1024] + dtb_ref[...])
        da = dt * (-jnp.exp(alog_ref[...]))
        cum = _mm01_left(lower, da, 3)
        cum_t = cum.T
        xdt = xs * dt
        bmb = bm.astype(BF16)
        cmb = cm.astype(BF16)
        gmat = [_dot_nt(cmb[:, g * 64:(g + 1) * 64], bmb[:, g * 64:(g + 1) * 64]) for g in range(2)]
        ys = []
        for h in range(4):
            hs = slice(h * 64, (h + 1) * 64)
            gs = slice((h // 2) * 64, (h // 2 + 1) * 64)
            cum_h = cum[:, hs]
            seg = cum[:, h * 64:h * 64 + 1] - cum_t[h * 64:h * 64 + 1, :]
            decay = jnp.exp(jnp.where(causal, seg, NEG_BIG))
            y_intra = _dot((gmat[h // 2] * decay).astype(BF16), xdt[:, hs].astype(BF16))
            last = cum_h[t - 1:t, :]
            h_prev = hs_ref[s, h]
            y_inter = _dot_nt((cm[:, gs] * jnp.exp(cum_h)).astype(BF16), h_prev.astype(BF16))
            s_c = _dot_tn((xdt[:, hs] * jnp.exp(last - cum_h)).astype(BF16), bmb[:, gs])
            hs_ref[s, h] = h_prev * jnp.exp(last) + s_c
            ys.append(y_intra + y_inter)
        y = jnp.concatenate(ys, axis=1) + d_ref[...] * xs
        outs.append(_group_rms(y * _silu(z), ng_ref[...], 128))
    o_ref[...] = jnp.stack(outs, axis=0)

    @pl.when(i == pl.num_programs(0) - 1)
    def _():
        for s in range(nb):
            conv_ref[s] = exts[s][t + 5:t + 8]
        h_ref[...] = hs_ref[...]


def _prompt_ssd(zx, cw, cb, dtb, alog, d_full, ng, *, batch, seq, t, layer):
    n = zx.shape[0]
    o, conv, h = pl.pallas_call(
        functools.partial(_ssd_kernel, t=t, nb=batch),
        grid=(seq // t,),
        in_specs=[pl.BlockSpec((batch, t, 1024), lambda i: (0, i, 0))]
        + [_layer_spec(a, layer) for a in (cw, cb, dtb, alog, d_full, ng)],
        out_specs=[
            pl.BlockSpec((batch, t, 256), lambda i: (0, i, 0)),
            pl.BlockSpec((batch, 3, 512), lambda i: (0, 0, 0)),
            pl.BlockSpec((batch, 4, 64, 64), lambda i: (0, 0, 0, 0)),
        ],
        out_shape=[
            jax.ShapeDtypeStruct((batch, seq, 256), F32),
            jax.ShapeDtypeStruct((batch, 3, 512), F32),
            jax.ShapeDtypeStruct((batch, 4, 64, 64), F32),
        ],
        scratch_shapes=[pltpu.VMEM((batch, 8, 512), F32), pltpu.VMEM((batch, 4, 64, 64), F32)],
        compiler_params=_cparams("arbitrary"),
        name="prompt_ssd",
    )(zx.reshape(batch, seq, 1024), cw, cb, dtb, alog, d_full, ng)
    return o.reshape(n, 256), conv, h


HG_CHUNK = 32


def _head_mean_matrix():
    same = (_iota2((256, 256), 0) // 64) == (_iota2((256, 256), 1) // 64)
    return jnp.where(same, 1.0, 0.0).astype(BF16)


def _head_rms(y, gain):
    ms = _mm01_right(y * y, _head_mean_matrix(), 2) * (1.0 / 64.0)
    return y * lax.rsqrt(ms + EPS) * gain


def _hgrn_kernel(hg_ref, lb_ref, ng_ref, o_ref, h_ref, st_ref, *, t, nb):
    lb_ref, ng_ref = _drop_layer_axis(lb_ref, ng_ref)
    i = pl.program_id(0)
    c = HG_CHUNK

    @pl.when(i == 0)
    def _():
        st_ref[...] = jnp.zeros_like(st_ref)

    lb = lb_ref[...]
    same_chunk = (_iota2((t, t), 0) // c) == (_iota2((t, t), 1) // c)
    lmat = jnp.where(same_chunk & (_iota2((t, t), 1) <= _iota2((t, t), 0)), 1.0, 0.0).astype(BF16)
    lane_head = _iota2((1, 256), 1) // 64
    head_lane = [lane_head == h for h in range(4)]
    stack_causal = _iota2((4 * c, c), 1) <= (_iota2((4 * c, c), 0) % c)
    block_diag = (_iota2((256, 256), 0) // 64) == (_iota2((256, 256), 1) // 64)
    pre = []
    for s in range(nb):
        q = _silu(hg_ref[s, :, 0:256])
        fr = hg_ref[s, :, 256:512]
        log_f = jnp.log(lb + (1.0 - lb) * _sigmoid(fr))
        k = (1.0 - lb) * _sigmoid(-fr)
        v = hg_ref[s, :, 512:768]
        pre.append((q, k, v, _mm01_left(lmat, log_f, 3)))
    ys = [[] for _ in range(nb)]
    for n in range(t // c):
        rows = slice(n * c, (n + 1) * c)
        for s in range(nb):
            q, k, v, b_all = pre[s]
            b = b_all[rows]
            ref = b[c // 2 - 1:c // 2]
            last = b[c - 1:c]
            q_c, k_c, v_c = q[rows], k[rows], v[rows]
            qe = q_c * jnp.exp(b - ref)
            ke = (k_c * jnp.exp(ref - b)).astype(BF16)
            kl = (k_c * jnp.exp(last - b)).astype(BF16)
            qb = (q_c * jnp.exp(b)).astype(BF16)
            vb = v_c.astype(BF16)
            q_stack = jnp.concatenate([jnp.where(m, qe, 0.0) for m in head_lane], axis=0).astype(BF16)
            scores = jnp.where(stack_causal, _dot_nt(q_stack, ke), 0.0)
            y4 = _dot(scores.astype(BF16), vb)
            y_intra = jnp.where(head_lane[0], y4[0:c], 0.0)
            for h in range(1, 4):
                y_intra = y_intra + jnp.where(head_lane[h], y4[h * c:(h + 1) * c], 0.0)
            st = st_ref[s]
            y_inter = _dot_nt(qb, st.astype(BF16))
            st_ref[s] = st * jnp.exp(last) + jnp.where(block_diag, _dot_tn(vb, kl), 0.0)
            ys[s].append(y_intra + y_inter)
    outs = []
    for s in range(nb):
        y = jnp.concatenate(ys[s], axis=0)
        outs.append(_head_rms(y, ng_ref[...]) * _silu(hg_ref[s, :, 768:1024]))
    o_ref[...] = jnp.stack(outs, axis=0)

    @pl.when(i == pl.num_programs(0) - 1)
    def _():
        for s in range(nb):
            st_t = st_ref[s].T
            for h in range(4):
                h_ref[s, h] = st_t[h * 64:(h + 1) * 64, h * 64:(h + 1) * 64]


def _prompt_hgrn(hg4, lb, ng, *, batch, seq, t, layer):
    n = hg4.shape[0]
    o, h = pl.pallas_call(
        functools.partial(_hgrn_kernel, t=t, nb=batch),
        grid=(seq // t,),
        in_specs=[pl.BlockSpec((batch, t, 1024), lambda i: (0, i, 0)), _layer_spec(lb, layer), _layer_spec(ng, layer)],
        out_specs=[
            pl.BlockSpec((batch, t, 256), lambda i: (0, i, 0)),
            pl.BlockSpec((batch, 4, 64, 64), lambda i: (0, 0, 0, 0)),
        ],
        out_shape=[jax.ShapeDtypeStruct((batch, seq, 256), F32), jax.ShapeDtypeStruct((batch, 4, 64, 64), F32)],
        scratch_shapes=[pltpu.VMEM((batch, 256, 256), F32)],
        compiler_params=_cparams("arbitrary"),
        name="prompt_hgrn",
    )(hg4.reshape(batch, seq, 1024), lb, ng)
    return o.reshape(n, 256), h


S5_ROW = 16
S5_GROUPS = 16
S5_STATE = 64
S5_SLAB = 128
S5_SEG = 4096
S5_LAM_ROWS = 32


def _s5_discretize(a_re, a_im, b_re, b_im, log_dt):
    dt = jnp.exp(log_dt)[:, None]
    mag = jnp.exp(a_re * dt)
    ab_re = mag * jnp.cos(a_im * dt)
    ab_im = mag * jnp.sin(a_im * dt)
    den = a_re * a_re + a_im * a_im
    coef_re = ((ab_re - 1.0) * a_re + ab_im * a_im) / den
    coef_im = (ab_im * a_re - (ab_re - 1.0) * a_im) / den
    bb_re = coef_re[..., None] * b_re - coef_im[..., None] * b_im
    bb_im = coef_re[..., None] * b_im + coef_im[..., None] * b_re
    return a_re * dt, a_im * dt, bb_re, bb_im


def _lam_pow(log_mag, arg, m):
    mag = jnp.exp(m * log_mag)
    return mag * jnp.cos(m * arg), mag * jnp.sin(m * arg)


def _block_diag_expand(a, rows_per_group, cols_per_group):
    g = S5_GROUPS
    tiled = jnp.dot(a, jnp.tile(jnp.eye(cols_per_group, dtype=F32), (1, g)), precision=lax.Precision.HIGHEST)
    same = (jnp.arange(g * rows_per_group)[:, None] // rows_per_group
            == jnp.arange(g * cols_per_group)[None, :] // cols_per_group)
    return jnp.where(same, tiled, 0.0)


def _s5_expand_b(bb):
    return _block_diag_expand(jnp.swapaxes(bb, 1, 2).reshape(S5_GROUPS * 16, S5_STATE), 16, S5_STATE)


def _s5_expand_c(cc):
    return _block_diag_expand(jnp.swapaxes(cc, 1, 2).reshape(S5_GROUPS * S5_STATE, 16), S5_STATE, 16)


def _s5_prompt_matrices(log_mag, arg, bb_re, bb_im, c_re, c_im):
    r, g, n = S5_ROW, S5_GROUPS, S5_STATE
    hp = lax.Precision.HIGHEST
    taus = jnp.arange(r, dtype=F32)[:, None, None]
    pw_re, pw_im = _lam_pow(log_mag[None], arg[None], taus)
    p_re = pw_re[..., None] * bb_re[None] - pw_im[..., None] * bb_im[None]
    p_im = pw_re[..., None] * bb_im[None] + pw_im[..., None] * bb_re[None]
    kern = (jnp.einsum("ghn,tgnk->tghk", c_re, p_re, precision=hp)
            - jnp.einsum("ghn,tgnk->tghk", c_im, p_im, precision=hp))
    kern_rows = jnp.swapaxes(kern, 2, 3).reshape(r, g * 16, 16)
    kf = jax.vmap(lambda a: _block_diag_expand(a, 16, 16))(kern_rows)
    ns = g * n // S5_SLAB
    slab_cols = lambda m: m.reshape(g * 16, ns, S5_SLAB).transpose(1, 0, 2)
    slab_rows = lambda m: m.reshape(ns, S5_SLAB, g * 16)
    bf = jnp.concatenate([slab_cols(_s5_expand_b(bb_re)), slab_cols(_s5_expand_b(bb_im))], axis=2)
    cf = jnp.concatenate([slab_rows(_s5_expand_c(c_re)), -slab_rows(_s5_expand_c(c_im))], axis=1)
    ks = jnp.arange((S5_LAM_ROWS - 2) // 2, dtype=F32)
    mults = jnp.concatenate([jnp.ones((1,), F32), r * 2.0 ** ks])[:, None, None]
    l_re, l_im = _lam_pow(log_mag[None], arg[None], mults)
    tab = jnp.stack([l_re, l_im], axis=1).reshape(-1, g * n)
    lam = tab.reshape(tab.shape[0], ns, S5_SLAB).transpose(1, 0, 2)
    return kf.astype(BF16), bf.astype(BF16), cf.astype(BF16), lam


def _shift_rows(x, d):
    return jnp.concatenate([jnp.zeros((d, x.shape[1]), x.dtype), x[:x.shape[0] - d]], axis=0)


def _s5_kernel(u_ref, kf_ref, bf_ref, cf_ref, lam_ref, y_ref, xre_ref, xim_ref,
               ubf_ref, xprev_ref, xin_ref, *, seg):
    kf_ref, bf_ref, cf_ref, lam_ref = _drop_layer_axis(kf_ref, bf_ref, cf_ref, lam_ref)
    r = S5_ROW
    rows = seg // r
    ns = bf_ref.shape[0]
    sg = pl.program_id(1)

    @pl.when(sg == 0)
    def _():
        xin_ref[...] = jnp.zeros_like(xin_ref)

    for s in range(r):
        ubf_ref[s] = jnp.concatenate(
            [u_ref[0, pl.ds(s, rows, stride=r), :], u_ref[1, pl.ds(s, rows, stride=r), :]], axis=1).astype(BF16)

    first_row = _iota2((rows, S5_SLAB), 0) == 0
    for j in range(ns):
        l_re, l_im = lam_ref[j, 0:1, :], lam_ref[j, 1:2, :]
        x_re = jnp.zeros((rows, S5_SLAB), F32)
        x_im = jnp.zeros((rows, S5_SLAB), F32)
        for s in range(r):
            z = _dot(ubf_ref[s], bf_ref[j])
            x_re, x_im = (l_re * x_re - l_im * x_im + z[:, :S5_SLAB],
                          l_re * x_im + l_im * x_re + z[:, S5_SLAB:])
        in_re, in_im = xin_ref[0, j, 0:1, :], xin_ref[1, j, 0:1, :]
        m_re, m_im = lam_ref[j, 2:3, :], lam_ref[j, 3:4, :]
        x_re = x_re + jnp.where(first_row, m_re * in_re - m_im * in_im, 0.0)
        x_im = x_im + jnp.where(first_row, m_re * in_im + m_im * in_re, 0.0)
        d, k = 1, 0
        while d < rows:
            m_re, m_im = lam_ref[j, 2 + 2 * k:3 + 2 * k, :], lam_ref[j, 3 + 2 * k:4 + 2 * k, :]
            s_re, s_im = _shift_rows(x_re, d), _shift_rows(x_im, d)
            x_re, x_im = x_re + m_re * s_re - m_im * s_im, x_im + m_re * s_im + m_im * s_re
            d, k = 2 * d, k + 1
        xprev_ref[0, j] = jnp.concatenate([in_re, x_re[:rows - 1]], axis=0)
        xprev_ref[1, j] = jnp.concatenate([in_im, x_im[:rows - 1]], axis=0)
        xin_ref[0, j, 0:1, :] = x_re[rows - 1:rows]
        xin_ref[1, j, 0:1, :] = x_im[rows - 1:rows]

    for p in range(r):
        acc = _dot(ubf_ref[0], kf_ref[p])
        for s in range(1, p + 1):
            acc = acc + _dot(ubf_ref[s], kf_ref[p - s])
        for j in range(ns):
            l_re, l_im = lam_ref[j, 0:1, :], lam_ref[j, 1:2, :]
            x_re, x_im = xprev_ref[0, j], xprev_ref[1, j]
            x_re, x_im = l_re * x_re - l_im * x_im, l_re * x_im + l_im * x_re
            xprev_ref[0, j] = x_re
            xprev_ref[1, j] = x_im
            acc = acc + _dot(jnp.concatenate([x_re, x_im], axis=1).astype(BF16), cf_ref[j])
        y_ref[0, pl.ds(p, rows, stride=r), :] = acc[:, 0:128]
        y_ref[1, pl.ds(p, rows, stride=r), :] = acc[:, 128:256]

    @pl.when(sg == pl.num_programs(1) - 1)
    def _():
        xre_ref[0] = jnp.concatenate([xin_ref[0, j, 0:1, :] for j in range(ns)], axis=1)
        xim_ref[0] = jnp.concatenate([xin_ref[1, j, 0:1, :] for j in range(ns)], axis=1)


def _s5_prompt_scan(u_halves, mats, *, batch, seq, layer):
    g, n = S5_GROUPS, S5_STATE
    seg = min(S5_SEG, seq)
    nseg = seq // seg
    rows = seg // S5_ROW
    assert (rows - 1).bit_length() <= (S5_LAM_ROWS - 2) // 2
    ns = g * n // S5_SLAB
    full = lambda a: _layer_spec(a, layer)
    halves = pl.BlockSpec((2, seg, 128), lambda b, s: (0, b * nseg + s, 0))
    state = pl.BlockSpec((1, 1, g * n), lambda b, s: (b, 0, 0))
    y, x_re, x_im = pl.pallas_call(
        functools.partial(_s5_kernel, seg=seg),
        grid=(batch, nseg),
        in_specs=[halves] + [full(a) for a in mats],
        out_specs=[halves, state, state],
        out_shape=[jax.ShapeDtypeStruct((2, batch * seq, 128), F32),
                   jax.ShapeDtypeStruct((batch, 1, g * n), F32),
                   jax.ShapeDtypeStruct((batch, 1, g * n), F32)],
        scratch_shapes=[pltpu.VMEM((S5_ROW, rows, 256), BF16),
                        pltpu.VMEM((2, ns, rows, S5_SLAB), F32),
                        pltpu.VMEM((2, ns, 8, S5_SLAB), F32)],
        compiler_params=_cparams("arbitrary", "arbitrary"),
        name="prompt_s5",
    )(u_halves, *mats)
    return y, x_re.reshape(batch, g, n), x_im.reshape(batch, g, n)


D_FF = 2816
FFN_SPLIT = 2


def _layer_norm(x, g, b):
    mu = jnp.mean(x, axis=-1, keepdims=True)
    xc = x - mu
    var = jnp.mean(xc * xc, axis=-1, keepdims=True)
    return xc * lax.rsqrt(var + EPS) * g + b


def _post_kernel(*refs, t, decode, alpha):
    x_ref, oa_ref, ob_ref, yc_ref, u_ref, od_ref = refs[:6]
    (s5d_ref, wglu_ref, bglu_ref, wout_ref, ln1g_ref, ln1b_ref,
     wup_ref, fcw_ref, fcb_ref, wdown_ref, ln2g_ref, ln2b_ref) = _drop_layer_axis(*refs[6:18])
    if decode:
        st_ref, xo_ref, sto_ref = refs[18:]
    else:
        xo_ref, sto_ref, tail_ref = refs[18:]
        i = pl.program_id(1)

        @pl.when(i == 0)
        def _():
            tail_ref[...] = jnp.zeros_like(tail_ref)

    if decode:
        yc_pre, u = yc_ref[...], u_ref[...]
    else:
        yc_pre = jnp.concatenate([yc_ref[0], yc_ref[1]], axis=1)
        u = jnp.concatenate([u_ref[0], u_ref[1]], axis=1)
    yc = _gelu_tanh(yc_pre + s5d_ref[...] * u)
    oc = yc * _sigmoid(_dot(yc.astype(BF16), wglu_ref[...]) + bglu_ref[...])
    mix = _dot(oa_ref[...].astype(BF16), wout_ref[0:256, :])
    mix = mix + _dot(ob_ref[...].astype(BF16), wout_ref[256:512, :])
    mix = mix + _dot(oc.astype(BF16), wout_ref[512:768, :])
    mix = mix + _dot(od_ref[...].astype(BF16), wout_ref[768:1024, :])
    x1 = _layer_norm(alpha * x_ref[...] + mix, ln1g_ref[...], ln1b_ref[...])
    x1b = x1.astype(BF16)
    ffn = None
    width = D_FF // FFN_SPLIT
    for c in range(FFN_SPLIT):
        lo, hi = c * width, (c + 1) * width
        uh = _dot(x1b, wup_ref[:, lo:hi])
        gh = _dot(x1b, wup_ref[:, D_FF + lo:D_FF + hi])
        if decode:
            s0 = st_ref[:, lo:hi]
            s1 = st_ref[:, D_FF + lo:D_FF + hi]
            gconv = fcb_ref[:, lo:hi] + fcw_ref[0:1, lo:hi] * s0 + fcw_ref[1:2, lo:hi] * s1 + fcw_ref[2:3, lo:hi] * gh
            sto_ref[:, lo:hi] = s1
            sto_ref[:, D_FF + lo:D_FF + hi] = gh
        else:
            ext = jnp.concatenate([tail_ref[:, lo:hi], gh], axis=0)
            gconv = fcb_ref[:, lo:hi]
            for j in range(3):
                gconv = gconv + ext[6 + j:6 + j + t] * fcw_ref[j:j + 1, lo:hi]
            tail_ref[:, lo:hi] = gh[t - 8:t]

            @pl.when(i == pl.num_programs(1) - 1)
            def _():
                sto_ref[0, :, lo:hi] = ext[t + 6:t + 8]

        hmid = (_gelu_tanh(gconv) * uh).astype(BF16)
        part = _dot(hmid, wdown_ref[lo:hi, :])
        ffn = part if ffn is None else ffn + part
    xo_ref[...] = _layer_norm(alpha * x1 + ffn, ln2g_ref[...], ln2b_ref[...])


def _post(x, oa, ob, yc, u, od, weights, *, alpha, layer, batch=None, seq=None, t=None, ffn_state=None):
    decode = ffn_state is not None
    n = x.shape[0]
    if decode:
        t = n
        grid = (1,)
        row = lambda i: (0, 0)
        extra_in = [pl.BlockSpec(ffn_state.shape, row)]
        extra_args = [ffn_state]
        out_specs = [pl.BlockSpec((t, 1024), row), pl.BlockSpec((t, 2 * D_FF), row)]
        out_shape = [jax.ShapeDtypeStruct((n, 1024), F32), jax.ShapeDtypeStruct((n, 2 * D_FF), F32)]
        scratch = []
        sem = ("arbitrary",)
    else:
        nt = seq // t
        grid = (batch, nt)
        row = lambda b, i: (b * nt + i, 0)
        extra_in, extra_args = [], []
        out_specs = [pl.BlockSpec((t, 1024), row), pl.BlockSpec((1, 2, D_FF), lambda b, i: (b, 0, 0))]
        out_shape = [jax.ShapeDtypeStruct((n, 1024), F32), jax.ShapeDtypeStruct((batch, 2, D_FF), F32)]
        scratch = [pltpu.VMEM((8, D_FF), F32)]
        sem = ("arbitrary", "arbitrary")
    acts = (x, oa, ob, yc, u, od)

    def act_spec(a):
        if a.ndim == 3:
            return pl.BlockSpec((2, t, 128), lambda b, i: (0, b * nt + i, 0))
        return pl.BlockSpec((t, a.shape[1]), row)

    return pl.pallas_call(
        functools.partial(_post_kernel, t=t, decode=decode, alpha=alpha),
        grid=grid,
        in_specs=[act_spec(a) for a in acts]
        + [_layer_spec(w, layer, pipeline_mode=pl.Buffered(1)) for w in weights] + extra_in,
        out_specs=out_specs,
        out_shape=out_shape,
        scratch_shapes=scratch,
        compiler_params=_cparams(*sem),
        name="post_decode" if decode else "post_prompt",
    )(*acts, *weights, *extra_args)


PAGE = 128
PAGES_PER_STEP = 32


def _decode_attn_kernel(pt_ref, q_ref, bias_ref, *refs, pp):
    del pt_ref
    (bias_ref,) = _drop_layer_axis(bias_ref)
    k_refs, v_refs = refs[:pp], refs[pp:2 * pp]
    o_ref, qb_ref, c_ref, acc_ref = refs[2 * pp:]
    s = pl.program_id(1)
    eye = _iota2((256, 256), 0) == _iota2((256, 256), 1)

    @pl.when(s == 0)
    def _():
        q_col = jnp.sum(jnp.where(eye, q_ref[0], 0.0), axis=1, keepdims=True)
        qb_ref[...] = jnp.broadcast_to(q_col, (256, PAGE))
        c_ref[...] = jnp.zeros_like(c_ref)
        acc_ref[...] = jnp.zeros_like(acc_ref)

    ones_sfx = _suffix_ones(PAGE)
    qb = qb_ref[...]
    c = c_ref[...]
    acc = acc_ref[...]
    for r in reversed(range(pp)):
        prod = k_refs[r][0] * qb
        z = jnp.concatenate(
            [jnp.sum(prod[h * HEAD_DIM:(h + 1) * HEAD_DIM], axis=0, keepdims=True) for h in range(4)]
            + [jnp.zeros((4, PAGE), F32)], axis=0) + bias_ref[...]
        sp = _softplus2(z)
        rest = _dot(sp.astype(BF16), ones_sfx)
        w = jnp.exp2(z - rest - c)
        v_t = v_refs[r][0]
        acc = acc + jnp.concatenate(
            [v_t[h * HEAD_DIM:(h + 1) * HEAD_DIM] * w[h:h + 1, :] for h in range(4)], axis=0)
        c = c + rest[:, 0:1]
    c_ref[...] = c
    acc_ref[...] = acc

    @pl.when(s == pl.num_programs(1) - 1)
    def _():
        o_col = jnp.sum(acc, axis=1, keepdims=True)
        o_ref[0] = jnp.sum(jnp.where(eye, o_col, 0.0), axis=0, keepdims=True)


def _decode_attention(q, pool_kt, pool_vt, page_table, bias8, *, base, layer):
    nseq, npages = page_table.shape
    pp = PAGES_PER_STEP
    nsteps = npages // pp

    def page_map(r):
        return lambda b, s, pt: (base + pt[b, (nsteps - 1 - s) * pp + r], 0, 0)

    page_specs = [pl.BlockSpec((1, 256, PAGE), page_map(r)) for r in range(pp)]
    out = pl.pallas_call(
        functools.partial(_decode_attn_kernel, pp=pp),
        grid_spec=pltpu.PrefetchScalarGridSpec(
            num_scalar_prefetch=1,
            grid=(nseq, nsteps),
            in_specs=[pl.BlockSpec((1, 1, 256), lambda b, s, pt: (b, 0, 0)),
                      _layer_spec(bias8, layer)] + page_specs + page_specs,
            out_specs=pl.BlockSpec((1, 1, 256), lambda b, s, pt: (b, 0, 0)),
            scratch_shapes=[pltpu.VMEM((256, PAGE), F32), pltpu.VMEM((8, PAGE), F32), pltpu.VMEM((256, PAGE), F32)],
        ),
        out_shape=jax.ShapeDtypeStruct((nseq, 1, 256), F32),
        compiler_params=_cparams("arbitrary", "arbitrary"),
        name="decode_attention",
    )(page_table, q.reshape(nseq, 1, 256), bias8, *([pool_kt] * pp), *([pool_vt] * pp))
    return out.reshape(nseq, 256)


def _dot_f32(a, b):
    a0, a1, a2 = _split3(a)
    b0, b1, b2 = _split3(b)
    return (_dot(a0, b0) + (_dot(a0, b1) + _dot(a1, b0))
            + (_dot(a0, b2) + _dot(a2, b0) + _dot(a1, b1)))


_R_DEC, _R_XDT, _R_BM, _R_CM, _R_F, _R_K, _R_Q, _R_V, _R_END = 0, 256, 512, 640, 768, 1024, 1280, 1536, 1792


def _decode_mixers_kernel(zx_ref, u_ref, hg_ref, cs_ref, hssm_ref, x0re_ref, x0im_ref, hhg_ref,
                          cw_ref, cb_ref, dtb_ref, alog_ref, d_ref, ngs_ref, lb_ref, ngh_ref,
                          lre_ref, lim_ref, bre_ref, bim_ref, cre_ref, cim_ref,
                          ob_ref, cso_ref, hssmo_ref, yc_ref, xre_ref, xim_ref, od_ref, hhgo_ref,
                          rows_ref, yssm_ref, yhg_ref, *, nseq):
    (cw_ref, cb_ref, dtb_ref, alog_ref, d_ref, ngs_ref, lb_ref, ngh_ref,
     lre_ref, lim_ref, bre_ref, bim_ref, cre_ref, cim_ref) = _drop_layer_axis(
        cw_ref, cb_ref, dtb_ref, alog_ref, d_ref, ngs_ref, lb_ref, ngh_ref,
        lre_ref, lim_ref, bre_ref, bim_ref, cre_ref, cim_ref)
    z = zx_ref[:, 0:256]
    xbc = zx_ref[:, 256:768]
    s0, s1, s2 = cs_ref[:, 0:512], cs_ref[:, 512:1024], cs_ref[:, 1024:1536]
    conv = cb_ref[...] + cw_ref[0:1, :] * s0 + cw_ref[1:2, :] * s1 + cw_ref[2:3, :] * s2 + cw_ref[3:4, :] * xbc
    cso_ref[:, 0:512] = s1
    cso_ref[:, 512:1024] = s2
    cso_ref[:, 1024:1536] = xbc
    act = _silu(conv)
    xs = act[:, 0:256]
    dt = _softplus(zx_ref[:, 768:1024] + dtb_ref[...])
    rows_ref[:, _R_DEC:_R_XDT] = jnp.exp(dt * (-jnp.exp(alog_ref[...])))
    rows_ref[:, _R_XDT:_R_BM] = xs * dt
    rows_ref[:, _R_BM:_R_F] = act[:, 256:512]
    lb = lb_ref[...]
    fr = hg_ref[:, 256:512]
    rows_ref[:, _R_F:_R_K] = lb + (1.0 - lb) * _sigmoid(fr)
    rows_ref[:, _R_K:_R_Q] = (1.0 - lb) * _sigmoid(-fr)
    rows_ref[:, _R_Q:_R_V] = _silu(hg_ref[:, 0:256])
    rows_ref[:, _R_V:_R_END] = hg_ref[:, 512:768]

    eye = _iota2((256, 256), 0) == _iota2((256, 256), 1)
    row_id = _iota2((256, 64), 0)

    def to_col(r):
        return jnp.sum(jnp.where(eye, r, 0.0), axis=1, keepdims=True)

    def to_row(c):
        return jnp.sum(jnp.where(eye, c, 0.0), axis=0, keepdims=True)

    def per_seq(b, get):
        bm = get(_R_BM, _R_CM)
        cm = get(_R_CM, _R_F)
        bm_rows = jnp.where(row_id < 128, bm[:, 0:64], bm[:, 64:128])
        cm_rows = jnp.where(row_id < 128, cm[:, 0:64], cm[:, 64:128])
        hn = hssm_ref[b] * to_col(get(_R_DEC, _R_XDT)) + to_col(get(_R_XDT, _R_BM)) * bm_rows
        hssmo_ref[b] = hn
        y_ssm = to_row(jnp.sum(hn * cm_rows, axis=1, keepdims=True))
        v = get(_R_V, _R_END)
        v_rows = jnp.where(row_id < 64, v[:, 0:64],
                           jnp.where(row_id < 128, v[:, 64:128],
                                     jnp.where(row_id < 192, v[:, 128:192], v[:, 192:256])))
        gn = hhg_ref[b] * to_col(get(_R_F, _R_K)) + to_col(get(_R_K, _R_Q)) * v_rows
        hhgo_ref[b] = gn
        qg = to_col(get(_R_Q, _R_V)) * gn
        y_hg = jnp.concatenate(
            [jnp.sum(qg[h * 64:(h + 1) * 64], axis=0, keepdims=True) for h in range(4)], axis=1)
        return y_ssm, y_hg

    def per_octet(o, carry):
        base = pl.multiple_of(o * 8, 8)
        blk = rows_ref[pl.ds(base, 8), :]
        ys = [per_seq(base + r, lambda lo, hi, r=r: blk[r:r + 1, lo:hi]) for r in range(8)]
        yssm_ref[pl.ds(base, 8), :] = jnp.concatenate([y[0] for y in ys], axis=0)
        yhg_ref[pl.ds(base, 8), :] = jnp.concatenate([y[1] for y in ys], axis=0)
        return carry

    lax.fori_loop(0, nseq // 8, per_octet, 0)

    y = yssm_ref[...] + d_ref[...] * xs
    ob_ref[...] = _group_rms(y * _silu(z), ngs_ref[...], 128)
    od_ref[...] = _head_rms(yhg_ref[...], ngh_ref[...]) * _silu(hg_ref[:, 768:1024])
    u = u_ref[...]
    x0r, x0i = x0re_ref[...], x0im_ref[...]
    lr, li = lre_ref[...], lim_ref[...]
    xr = lr * x0r - li * x0i + _dot_f32(u, bre_ref[...])
    xi = lr * x0i + li * x0r + _dot_f32(u, bim_ref[...])
    xre_ref[...] = xr
    xim_ref[...] = xi
    yc_ref[...] = _dot_f32(xr, cre_ref[...]) - _dot_f32(xi, cim_ref[...])


def _decode_mixers(zx, u, hg4, conv_state, h_ssm, x0_re, x0_im, h_hg, params, *, layer):
    nseq = zx.shape[0]
    acts = (zx, u, hg4, conv_state, h_ssm, x0_re, x0_im, h_hg)
    args = (*acts, *params)
    full = lambda a: pl.BlockSpec(a.shape, lambda i: (0,) * a.ndim)
    out_shape = [
        jax.ShapeDtypeStruct((nseq, 256), F32),
        jax.ShapeDtypeStruct((nseq, 1536), F32),
        jax.ShapeDtypeStruct((nseq, 256, 64), F32),
        jax.ShapeDtypeStruct((nseq, 256), F32),
        jax.ShapeDtypeStruct((nseq, 1024), F32),
        jax.ShapeDtypeStruct((nseq, 1024), F32),
        jax.ShapeDtypeStruct((nseq, 256), F32),
        jax.ShapeDtypeStruct((nseq, 256, 64), F32),
    ]
    return pl.pallas_call(
        functools.partial(_decode_mixers_kernel, nseq=nseq),
        grid=(1,),
        in_specs=[full(a) for a in acts] + [_layer_spec(p, layer) for p in params],
        out_specs=[full(s) for s in out_shape],
        out_shape=out_shape,
        scratch_shapes=[pltpu.VMEM((nseq, _R_END), F32), pltpu.VMEM((nseq, 256), F32), pltpu.VMEM((nseq, 256), F32)],
        compiler_params=_cparams("arbitrary"),
        name="decode_mixers",
    )(*args)


def _s5_decode_matrices(log_mag, arg, bb_re, bb_im, c_re, c_im):
    g, n = S5_GROUPS, S5_STATE
    lam_re, lam_im = _lam_pow(log_mag, arg, 1.0)
    return (lam_re.reshape(1, g * n), lam_im.reshape(1, g * n),
            _s5_expand_b(bb_re), _s5_expand_b(bb_im), _s5_expand_c(c_re), _s5_expand_c(c_im))


ATTN_TQ = 1024
ATTN_TK = 256
PROJ_TILE = 512
SSD_TILE = 256
HGRN_TILE = 256
POST_TILE = 512


def _rearranged_w_in(w_in):
    q, k, v, z, xbc, dt, u, hq, hf, hi, hg = jnp.split(
        w_in, [256, 512, 768, 1024, 1536, 1540, 1796, 2052, 2308, 2564], axis=1)
    dt_full = jnp.repeat(dt, HEAD_DIM, axis=1)
    q_scale = LOG2E * HEAD_DIM ** -0.5
    return jnp.concatenate([q * q_scale, k, v, z, xbc, dt_full, u, hq, hf, hi, hg], axis=1).astype(BF16)


def kernel(x_prompt, x_sample, cache_k, cache_v, state_ssm_conv, state_ssm, state_s5_re, state_s5_im, state_hgrn, state_ffn_conv, page_table, ln1_g, ln1_b, ln2_g, ln2_b, w_in, w_out, sb_logit_bias, ssm_conv_w, ssm_conv_b, ssm_dt_bias, ssm_a_log, ssm_d, ssm_norm_g, s5_a_re, s5_a_im, s5_b_re, s5_b_im, s5_c_re, s5_c_im, s5_d, s5_log_dt, s5_w_glu, s5_b_glu, hg_lb_logits, hg_norm_g, w_up, ffn_conv_w, ffn_conv_b, w_down):
    depth = w_in.shape[0]
    bp, seq, dm = x_prompt.shape
    ns = x_sample.shape[0]
    n_phys = cache_k.shape[1]
    alpha = (2 * depth) ** 0.25
    row = lambda a: a[:, None, :]
    rep = lambda a: jnp.repeat(a, HEAD_DIM, axis=1)[:, None, :]
    pr = jax.nn.softmax(hg_lb_logits.astype(F32), axis=0)
    lb = row(jnp.cumsum(pr, axis=0) - pr[0:1])
    ngh = row(hg_norm_g)
    w_p = jax.vmap(_rearranged_w_in)(w_in)
    disc = jax.vmap(_s5_discretize)(s5_a_re, s5_a_im, s5_b_re, s5_b_im, s5_log_dt)
    s5_mats = jax.vmap(_s5_prompt_matrices)(*disc, s5_c_re, s5_c_im)
    ssd_w = (ssm_conv_w, row(ssm_conv_b), rep(ssm_dt_bias), rep(ssm_a_log), rep(ssm_d), row(ssm_norm_g))
    post_w = (row(s5_d), s5_w_glu.astype(BF16), row(s5_b_glu), w_out.astype(BF16), row(ln1_g), row(ln1_b),
              w_up.astype(BF16), ffn_conv_w, row(ffn_conv_b), w_down.astype(BF16), row(ln2_g), row(ln2_b))
    dec_w = (*ssd_w, lb, ngh, *jax.vmap(_s5_decode_matrices)(*disc, s5_c_re, s5_c_im))
    bias8 = jnp.broadcast_to(jnp.pad(sb_logit_bias * LOG2E, ((0, 0), (0, 4)))[:, :, None], (depth, 8, PAGE))
    pool_kt = cache_k.transpose(0, 1, 3, 4, 2).reshape(depth * n_phys, 256, PAGE)
    pool_vt = cache_v.transpose(0, 1, 3, 4, 2).reshape(depth * n_phys, 256, PAGE)

    xp = x_prompt.reshape(bp * seq, dm)
    xs = x_sample.reshape(ns, dm)
    k_all = jnp.zeros((depth, bp, 256, seq), F32)
    v_all = jnp.zeros((depth, bp, 256, seq), F32)
    outs_p, outs_s = [], []
    for l in range(depth):
        q, vb, zx, u, hg4, kt, k_all, v_all = _projection(
            xp, w_p, tm=PROJ_TILE, tk=ATTN_TK, seq=seq, layer=l, k_all=k_all, v_all=v_all)
        oa = _prompt_attention(q, kt, vb, sb_logit_bias, batch=bp, seq=seq, tq=ATTN_TQ, tk=ATTN_TK, layer=l)
        ob, conv_p, ssm_p = _prompt_ssd(zx, *ssd_w, batch=bp, seq=seq, t=SSD_TILE, layer=l)
        yc, re_p, im_p = _s5_prompt_scan(u, s5_mats, batch=bp, seq=seq, layer=l)
        od, hg_p = _prompt_hgrn(hg4, lb, ngh, batch=bp, seq=seq, t=HGRN_TILE, layer=l)
        xp, ffn_p = _post(xp, oa, ob, yc, u, od, post_w, alpha=alpha, layer=l, batch=bp, seq=seq, t=POST_TILE)
        outs_p.append((conv_p, ssm_p, re_p, im_p, hg_p, ffn_p))

        q, k, v, zx, u, hg4 = _projection(xs, w_p, tm=ns, layer=l)
        oa = _decode_attention(q, pool_kt, pool_vt, page_table, bias8, base=l * n_phys, layer=l)
        ob, conv_s, ssm_s, yc, re_s, im_s, od, hg_s = _decode_mixers(
            zx, u, hg4, state_ssm_conv[l].reshape(ns, 1536), state_ssm[l].reshape(ns, 256, 64),
            state_s5_re[l].reshape(ns, 1024), state_s5_im[l].reshape(ns, 1024),
            state_hgrn[l].reshape(ns, 256, 64), dec_w, layer=l)
        xs, ffn_s = _post(xs, oa, ob, yc, u, od, post_w, alpha=alpha, layer=l,
                          ffn_state=state_ffn_conv[l].reshape(ns, 2 * D_FF))
        outs_s.append((k.reshape(ns, 1, 4, HEAD_DIM), v.reshape(ns, 1, 4, HEAD_DIM),
                       conv_s.reshape(ns, 3, 512), ssm_s.reshape(ns, 4, 64, 64),
                       re_s.reshape(ns, S5_GROUPS, S5_STATE), im_s.reshape(ns, S5_GROUPS, S5_STATE),
                       hg_s.reshape(ns, 4, 64, 64), ffn_s.reshape(ns, 2, D_FF)))

    sp = [jnp.stack(col, axis=0) for col in zip(*outs_p)]
    ss = [jnp.stack(col, axis=0) for col in zip(*outs_s)]
    k_prompt = k_all.reshape(depth, bp, 4, HEAD_DIM, seq).transpose(0, 1, 4, 2, 3)
    v_prompt = v_all.reshape(depth, bp, 4, HEAD_DIM, seq).transpose(0, 1, 4, 2, 3)
    return (xp.reshape(bp, seq, dm), xs.reshape(ns, 1, dm), k_prompt, v_prompt, ss[0], ss[1], sp[0], ss[2],
            sp[1], ss[3], sp[2], ss[4], sp[3], ss[5], sp[4], ss[6], sp[5], ss[7])
```

```python
import functools
import math

import jax
import jax.numpy as jnp
from jax import lax
from jax.experimental import pallas as pl
from jax.experimental.pallas import tpu as pltpu

F32 = jnp.float32
BF16 = jnp.bfloat16

HEAD_DIM = 64
EPS = 1e-5
NEG_BIG = -1e30

VMEM_LIMIT_BYTES = 56 * 1024 * 1024


def _cparams(*sem):
    return pltpu.CompilerParams(dimension_semantics=sem, vmem_limit_bytes=VMEM_LIMIT_BYTES)


def _layer_spec(a, layer, **kw):
    return pl.BlockSpec((1,) + a.shape[1:], lambda *_: (layer,) + (0,) * (a.ndim - 1), **kw)


def _drop_layer_axis(*refs):
    return tuple(r.at[0] for r in refs)


def _dot(a, b):
    return jnp.dot(a, b, preferred_element_type=F32)


def _dot_nt(a, b):
    return lax.dot_general(a, b, (((1,), (1,)), ((), ())), preferred_element_type=F32)


def _dot_tn(a, b):
    return lax.dot_general(a, b, (((0,), (0,)), ((), ())), preferred_element_type=F32)


def _split2(x):
    hi = x.astype(BF16)
    lo = (x - hi.astype(F32)).astype(BF16)
    return hi, lo


def _split3(x):
    hi = x.astype(BF16)
    r = x - hi.astype(F32)
    mid = r.astype(BF16)
    lo = (r - mid.astype(F32)).astype(BF16)
    return hi, mid, lo


def _mm01_left(m01, x, passes):
    parts = _split3(x) if passes == 3 else _split2(x)
    acc = _dot(m01, parts[0])
    for p in parts[1:]:
        acc = acc + _dot(m01, p)
    return acc


def _mm01_right(x, m01, passes):
    parts = _split3(x) if passes == 3 else _split2(x)
    acc = _dot(parts[0], m01)
    for p in parts[1:]:
        acc = acc + _dot(p, m01)
    return acc


def _sigmoid(x):
    return 1.0 / (1.0 + jnp.exp(-x))


def _silu(x):
    return x * _sigmoid(x)


def _softplus(x):
    return jnp.maximum(x, 0.0) + jnp.log(1.0 + jnp.exp(-jnp.abs(x)))


def _gelu_tanh(x):
    c = math.sqrt(2.0 / math.pi)
    return 0.5 * x * (1.0 + jnp.tanh(c * (x + 0.044715 * (x * x * x))))


def _iota2(shape, dim):
    return lax.broadcasted_iota(jnp.int32, shape, dim)


PROJ_W = 3072


def _proj_kernel(x_ref, w_ref, *refs, tk):
    (w_ref,) = _drop_layer_axis(w_ref)
    x = x_ref[...].astype(BF16)

    def mm(lo, hi):
        return _dot(x, w_ref[:, lo:hi])

    q = mm(0, 256)
    k = mm(256, 512)
    v = mm(512, 768)
    u = mm(1792, 2048)
    if tk:
        _, _, q_ref, vb_ref, zx_ref, u_ref, hg_ref, kt_ref, kall_ref, vall_ref = refs
        u_ref[0] = u[:, 0:128]
        u_ref[1] = u[:, 128:256]
        q_ref[...] = q.astype(BF16)
        vb_ref[...] = v.astype(BF16)
        k_t = k.T
        kall_ref[0, 0] = k_t
        vall_ref[0, 0] = v.T
        for c in range(k.shape[0] // tk):
            kt_ref[c] = k_t[:, c * tk:(c + 1) * tk].astype(BF16)
    else:
        q_ref, k_ref, v_ref, zx_ref, u_ref, hg_ref = refs
        q_ref[...] = q
        k_ref[...] = k
        v_ref[...] = v
        u_ref[...] = u
    zx_ref[...] = mm(768, 1792)
    hg_ref[...] = mm(2048, 3072)


def _projection(x2d, w_p, *, tm, tk=0, seq=None, layer=None, k_all=None, v_all=None):
    n, d = x2d.shape
    row = lambda i: (i, 0)
    rows = lambda width, dtype: (jax.ShapeDtypeStruct((n, width), dtype), pl.BlockSpec((tm, width), row))
    in_specs = [pl.BlockSpec((tm, d), row), _layer_spec(w_p, layer)]
    args = [x2d, w_p]
    aliases = {}
    if tk:
        nt = seq // tm
        slab = pl.BlockSpec((1, 1, 256, tm), lambda i: (layer, i // nt, 0, i % nt))
        halves = (jax.ShapeDtypeStruct((2, n, 128), F32), pl.BlockSpec((2, tm, 128), lambda i: (0, i, 0)))
        outs = [rows(256, BF16), rows(256, BF16), rows(1024, F32), halves, rows(1024, F32),
                (jax.ShapeDtypeStruct((n // tk, 256, tk), BF16), pl.BlockSpec((tm // tk, 256, tk), lambda i: (i, 0, 0))),
                (jax.ShapeDtypeStruct(k_all.shape, F32), slab), (jax.ShapeDtypeStruct(v_all.shape, F32), slab)]
        in_specs += [pl.BlockSpec(memory_space=pl.ANY), pl.BlockSpec(memory_space=pl.ANY)]
        args += [k_all, v_all]
        aliases = {2: 6, 3: 7}
    else:
        outs = [rows(256, F32), rows(256, F32), rows(256, F32), rows(1024, F32), rows(256, F32), rows(1024, F32)]
    return pl.pallas_call(
        functools.partial(_proj_kernel, tk=tk),
        grid=(n // tm,),
        in_specs=in_specs,
        out_specs=[o[1] for o in outs],
        out_shape=[o[0] for o in outs],
        input_output_aliases=aliases,
        compiler_params=_cparams("arbitrary"),
        name="projection",
    )(*args)


LOG2E = 1.4426950408889634


def _suffix_ones(n):
    return jnp.where(_iota2((n, n), 0) >= _iota2((n, n), 1), 1.0, 0.0).astype(BF16)


def _softplus2(z):
    return jnp.maximum(z, 0.0) + jnp.log2(1.0 + jnp.exp2(-jnp.abs(z)))


def _sb_block(qs, kt_pair, v_pair, bias_col, ones_sfx, carry, mask):
    z = _dot(qs, kt_pair) + bias_col
    sp = _softplus2(z)
    if mask is not None:
        sp = jnp.where(mask, sp, 0.0)
    rest = _dot(sp.astype(BF16), ones_sfx)
    w = jnp.exp2(z - rest - carry)
    if mask is not None:
        w = jnp.where(mask, w, 0.0)
    return _dot(w.astype(BF16), v_pair), rest[:, 0:1]


def _attn_kernel(bias_ref, q_ref, kt_ref, v_ref, o_ref, qs_ref, acc_ref, *, tq, tk, layer):
    hp = pl.program_id(1)
    i = pl.program_id(2)
    ones_sfx = _suffix_ones(tk)
    r = tq // tk
    lane_head = _iota2((tq, 128), 1) // HEAD_DIM
    q = q_ref[...]
    qs_ref[0:tq, :] = jnp.where(lane_head == 0, q, jnp.zeros_like(q))
    qs_ref[tq:2 * tq, :] = jnp.where(lane_head == 1, q, jnp.zeros_like(q))
    bias0 = bias_ref[layer, 2 * hp] * LOG2E
    bias1 = bias_ref[layer, 2 * hp + 1] * LOG2E

    def kv(j):
        start = pl.multiple_of(j * tk, tk)
        return kt_ref[j], v_ref[pl.ds(start, tk), :]

    acc = jnp.zeros((2 * tq, 128), F32)
    tot = jnp.zeros((2 * tq, 1), F32)
    for d in reversed(range(r)):
        lo = d * tk
        nd = tq - lo
        qs_d = jnp.concatenate([qs_ref[lo:tq, :], qs_ref[tq + lo:2 * tq, :]], axis=0)
        tot_d = jnp.concatenate([tot[lo:tq], tot[tq + lo:2 * tq]], axis=0)
        bias_d = jnp.where(_iota2((2 * nd, 1), 0) < nd, bias0, bias1)
        mask = _iota2((2 * nd, tk), 1) < _iota2((2 * nd, tk), 0) % nd
        kt_p, v_p = kv(i * r + d)
        da, dt = _sb_block(qs_d, kt_p, v_p, bias_d, ones_sfx, tot_d, mask)
        pieces_a = [da[:nd], da[nd:]]
        pieces_t = [dt[:nd], dt[nd:]]
        if lo:
            pieces_a = [jnp.zeros((lo, 128), F32), da[:nd], jnp.zeros((lo, 128), F32), da[nd:]]
            pieces_t = [jnp.zeros((lo, 1), F32), dt[:nd], jnp.zeros((lo, 1), F32), dt[nd:]]
        acc = acc + jnp.concatenate(pieces_a, axis=0)
        tot = tot + jnp.concatenate(pieces_t, axis=0)
    acc_ref[...] = acc
    bias_col = jnp.where(_iota2((2 * tq, 1), 0) < tq, bias0, bias1)

    def body(it, tot):
        for k in range(r):
            kt_p, v_p = kv((i - it) * r - 1 - k)
            da, dt = _sb_block(qs_ref[...], kt_p, v_p, bias_col, ones_sfx, tot, None)
            acc_ref[...] += da
            tot = tot + dt
        return tot

    lax.fori_loop(0, i, body, tot)
    o_ref[...] = jnp.where(lane_head == 0, acc_ref[0:tq, :], acc_ref[tq:2 * tq, :])


def _prompt_attention(q, kt, vb, bias, *, batch, seq, tq, tk, layer):
    n = q.shape[0]
    nq = seq // tq
    return pl.pallas_call(
        functools.partial(_attn_kernel, tq=tq, tk=tk, layer=layer),
        grid=(batch, 2, nq),
        in_specs=[
            pl.BlockSpec(memory_space=pltpu.SMEM),
            pl.BlockSpec((tq, 128), lambda b, hp, i: (b * nq + i, hp)),
            pl.BlockSpec((seq // tk, 128, tk), lambda b, hp, i: (b, hp, 0)),
            pl.BlockSpec((seq, 128), lambda b, hp, i: (b, hp)),
        ],
        out_specs=pl.BlockSpec((tq, 128), lambda b, hp, i: (b * nq + i, hp)),
        out_shape=jax.ShapeDtypeStruct((n, 256), F32),
        scratch_shapes=[pltpu.VMEM((2 * tq, 128), BF16), pltpu.VMEM((2 * tq, 128), F32)],
        compiler_params=_cparams("arbitrary", "arbitrary", "arbitrary"),
        name="prompt_attention",
    )(bias, q, kt, vb)


def _lower_incl(n):
    return jnp.where(_iota2((n, n), 1) <= _iota2((n, n), 0), 1.0, 0.0).astype(BF16)


def _group_rms(x, gain, width):
    outs = []
    for g in range(x.shape[1] // width):
        xg = x[:, g * width:(g + 1) * width]
        ms = jnp.mean(xg * xg, axis=1, keepdims=True)
        outs.append(xg * lax.rsqrt(ms + EPS))
    return jnp.concatenate(outs, axis=1) * gain


def _ssd_kernel(zx_ref, cw_ref, cb_ref, dtb_ref, alog_ref, d_ref, ng_ref,
                o_ref, conv_ref, h_ref, tail_ref, hs_ref, *, t, nb):
    cw_ref, cb_ref, dtb_ref, alog_ref, d_ref, ng_ref = _drop_layer_axis(
        cw_ref, cb_ref, dtb_ref, alog_ref, d_ref, ng_ref)
    i = pl.program_id(0)

    @pl.when(i == 0)
    def _():
        tail_ref[...] = jnp.zeros_like(tail_ref)
        hs_ref[...] = jnp.zeros_like(hs_ref)

    causal = _iota2((t, t), 1) <= _iota2((t, t), 0)
    lower = _lower_incl(t)
    outs, exts = [], []
    for s in range(nb):
        z = zx_ref[s, :, 0:256]
        xbc = zx_ref[s, :, 256:768]
        ext = jnp.concatenate([tail_ref[s], xbc], axis=0)
        exts.append(ext)
        conv = cb_ref[...]
        for j in range(4):
            conv = conv + ext[5 + j:5 + j + t] * cw_ref[j:j + 1, :]
        tail_ref[s] = xbc[t - 8:t]
        act = _silu(conv)
        xs, bm, cm = act[:, 0:256], act[:, 256:384], act[:, 384:512]
        dt = _softplus(zx_ref[s, :, 768:1024] + dtb_ref[...])
        da = dt * (-jnp.exp(alog_ref[...]))
        cum = _mm01_left(lower, da, 3)
        cum_t = cum.T
        xdt = xs * dt
        bmb = bm.astype(BF16)
        cmb = cm.astype(BF16)
        gmat = [_dot_nt(cmb[:, g * 64:(g + 1) * 64], bmb[:, g * 64:(g + 1) * 64]) for g in range(2)]
        ys = []
        for h in range(4):
            hs = slice(h * 64, (h + 1) * 64)
            gs = slice((h // 2) * 64, (h // 2 + 1) * 64)
            cum_h = cum[:, hs]
            seg = cum[:, h * 64:h * 64 + 1] - cum_t[h * 64:h * 64 + 1, :]
            decay = jnp.exp(jnp.where(causal, seg, NEG_BIG))
            y_intra = _dot((gmat[h // 2] * decay).astype(BF16), xdt[:, hs].astype(BF16))
            last = cum_h[t - 1:t, :]
            h_prev = hs_ref[s, h]
            y_inter = _dot_nt((cm[:, gs] * jnp.exp(cum_h)).astype(BF16), h_prev.astype(BF16))
            s_c = _dot_tn((xdt[:, hs] * jnp.exp(last - cum_h)).astype(BF16), bmb[:, gs])
            hs_ref[s, h] = h_prev * jnp.exp(last) + s_c
            ys.append(y_intra + y_inter)
        y = jnp.concatenate(ys, axis=1) + d_ref[...] * xs
        outs.append(_group_rms(y * _silu(z), ng_ref[...], 128))
    o_ref[...] = jnp.stack(outs, axis=0)

    @pl.when(i == pl.num_programs(0) - 1)
    def _():
        for s in range(nb):
            conv_ref[s] = exts[s][t + 5:t + 8]
        h_ref[...] = hs_ref[...]


def _prompt_ssd(zx, cw, cb, dtb, alog, d_full, ng, *, batch, seq, t, layer):
    n = zx.shape[0]
    o, conv, h = pl.pallas_call(
        functools.partial(_ssd_kernel, t=t, nb=batch),
        grid=(seq // t,),
        in_specs=[pl.BlockSpec((batch, t, 1024), lambda i: (0, i, 0))]
        + [_layer_spec(a, layer) for a in (cw, cb, dtb, alog, d_full, ng)],
        out_specs=[
            pl.BlockSpec((batch, t, 256), lambda i: (0, i, 0)),
            pl.BlockSpec((batch, 3, 512), lambda i: (0, 0, 0)),
            pl.BlockSpec((batch, 4, 64, 64), lambda i: (0, 0, 0, 0)),
        ],
        out_shape=[
            jax.ShapeDtypeStruct((batch, seq, 256), F32),
            jax.ShapeDtypeStruct((batch, 3, 512), F32),
            jax.ShapeDtypeStruct((batch, 4, 64, 64), F32),
        ],
        scratch_shapes=[pltpu.VMEM((batch, 8, 512), F32), pltpu.VMEM((batch, 4, 64, 64), F32)],
        compiler_params=_cparams("arbitrary"),
        name="prompt_ssd",
    )(zx.reshape(batch, seq, 1024), cw, cb, dtb, alog, d_full, ng)
    return o.reshape(n, 256), conv, h


HG_CHUNK = 32


def _head_mean_matrix():
    same = (_iota2((256, 256), 0) // 64) == (_iota2((256, 256), 1) // 64)
    return jnp.where(same, 1.0, 0.0).astype(BF16)


def _head_rms(y, gain):
    ms = _mm01_right(y * y, _head_mean_matrix(), 2) * (1.0 / 64.0)
    return y * lax.rsqrt(ms + EPS) * gain


def _hgrn_kernel(hg_ref, lb_ref, ng_ref, o_ref, h_ref, st_ref, *, t, nb):
    lb_ref, ng_ref = _drop_layer_axis(lb_ref, ng_ref)
    i = pl.program_id(0)
    c = HG_CHUNK

    @pl.when(i == 0)
    def _():
        st_ref[...] = jnp.zeros_like(st_ref)

    lb = lb_ref[...]
    same_chunk = (_iota2((t, t), 0) // c) == (_iota2((t, t), 1) // c)
    lmat = jnp.where(same_chunk & (_iota2((t, t), 1) <= _iota2((t, t), 0)), 1.0, 0.0).astype(BF16)
    lane_head = _iota2((1, 256), 1) // 64
    head_lane = [lane_head == h for h in range(4)]
    stack_causal = _iota2((4 * c, c), 1) <= (_iota2((4 * c, c), 0) % c)
    block_diag = (_iota2((256, 256), 0) // 64) == (_iota2((256, 256), 1) // 64)
    pre = []
    for s in range(nb):
        q = _silu(hg_ref[s, :, 0:256])
        fr = hg_ref[s, :, 256:512]
        log_f = jnp.log(lb + (1.0 - lb) * _sigmoid(fr))
        k = (1.0 - lb) * _sigmoid(-fr)
        v = hg_ref[s, :, 512:768]
        pre.append((q, k, v, _mm01_left(lmat, log_f, 3)))
    ys = [[] for _ in range(nb)]
    for n in range(t // c):
        rows = slice(n * c, (n + 1) * c)
        for s in range(nb):
            q, k, v, b_all = pre[s]
            b = b_all[rows]
            ref = b[c // 2 - 1:c // 2]
            last = b[c - 1:c]
            q_c, k_c, v_c = q[rows], k[rows], v[rows]
            qe = q_c * jnp.exp(b - ref)
            ke = (k_c * jnp.exp(ref - b)).astype(BF16)
            kl = (k_c * jnp.exp(last - b)).astype(BF16)
            qb = (q_c * jnp.exp(b)).astype(BF16)
            vb = v_c.astype(BF16)
            q_stack = jnp.concatenate([jnp.where(m, qe, 0.0) for m in head_lane], axis=0).astype(BF16)
            scores = jnp.where(stack_causal, _dot_nt(q_stack, ke), 0.0)
            y4 = _dot(scores.astype(BF16), vb)
            y_intra = jnp.where(head_lane[0], y4[0:c], 0.0)
            for h in range(1, 4):
                y_intra = y_intra + jnp.where(head_lane[h], y4[h * c:(h + 1) * c], 0.0)
            st = st_ref[s]
            y_inter = _dot_nt(qb, st.astype(BF16))
            st_ref[s] = st * jnp.exp(last) + jnp.where(block_diag, _dot_tn(vb, kl), 0.0)
            ys[s].append(y_intra + y_inter)
    outs = []
    for s in range(nb):
        y = jnp.concatenate(ys[s], axis=0)
        outs.append(_head_rms(y, ng_ref[...]) * _silu(hg_ref[s, :, 768:1024]))
    o_ref[...] = jnp.stack(outs, axis=0)

    @pl.when(i == pl.num_programs(0) - 1)
    def _():
        for s in range(nb):
            st_t = st_ref[s].T
            for h in range(4):
                h_ref[s, h] = st_t[h * 64:(h + 1) * 64, h * 64:(h + 1) * 64]


def _prompt_hgrn(hg4, lb, ng, *, batch, seq, t, layer):
    n = hg4.shape[0]
    o, h = pl.pallas_call(
        functools.partial(_hgrn_kernel, t=t, nb=batch),
        grid=(seq // t,),
        in_specs=[pl.BlockSpec((batch, t, 1024), lambda i: (0, i, 0)), _layer_spec(lb, layer), _layer_spec(ng, layer)],
        out_specs=[
            pl.BlockSpec((batch, t, 256), lambda i: (0, i, 0)),
            pl.BlockSpec((batch, 4, 64, 64), lambda i: (0, 0, 0, 0)),
        ],
        out_shape=[jax.ShapeDtypeStruct((batch, seq, 256), F32), jax.ShapeDtypeStruct((batch, 4, 64, 64), F32)],
        scratch_shapes=[pltpu.VMEM((batch, 256, 256), F32)],
        compiler_params=_cparams("arbitrary"),
        name="prompt_hgrn",
    )(hg4.reshape(batch, seq, 1024), lb, ng)
    return o.reshape(n, 256), h


S5_ROW = 16
S5_GROUPS = 16
S5_STATE = 64
S5_SLAB = 128
S5_SEG = 4096
S5_LAM_ROWS = 32


def _s5_discretize(a_re, a_im, b_re, b_im, log_dt):
    dt = jnp.exp(log_dt)[:, None]
    mag = jnp.exp(a_re * dt)
    ab_re = mag * jnp.cos(a_im * dt)
    ab_im = mag * jnp.sin(a_im * dt)
    den = a_re * a_re + a_im * a_im
    coef_re = ((ab_re - 1.0) * a_re + ab_im * a_im) / den
    coef_im = (ab_im * a_re - (ab_re - 1.0) * a_im) / den
    bb_re = coef_re[..., None] * b_re - coef_im[..., None] * b_im
    bb_im = coef_re[..., None] * b_im + coef_im[..., None] * b_re
    return a_re * dt, a_im * dt, bb_re, bb_im


def _lam_pow(log_mag, arg, m):
    mag = jnp.exp(m * log_mag)
    return mag * jnp.cos(m * arg), mag * jnp.sin(m * arg)


def _block_diag_expand(a, rows_per_group, cols_per_group):
    g = S5_GROUPS
    tiled = jnp.dot(a, jnp.tile(jnp.eye(cols_per_group, dtype=F32), (1, g)), precision=lax.Precision.HIGHEST)
    same = (jnp.arange(g * rows_per_group)[:, None] // rows_per_group
            == jnp.arange(g * cols_per_group)[None, :] // cols_per_group)
    return jnp.where(same, tiled, 0.0)


def _s5_expand_b(bb):
    return _block_diag_expand(jnp.swapaxes(bb, 1, 2).reshape(S5_GROUPS * 16, S5_STATE), 16, S5_STATE)


def _s5_expand_c(cc):
    return _block_diag_expand(jnp.swapaxes(cc, 1, 2).reshape(S5_GROUPS * S5_STATE, 16), S5_STATE, 16)


def _s5_prompt_matrices(log_mag, arg, bb_re, bb_im, c_re, c_im):
    r, g, n = S5_ROW, S5_GROUPS, S5_STATE
    hp = lax.Precision.HIGHEST
    taus = jnp.arange(r, dtype=F32)[:, None, None]
    pw_re, pw_im = _lam_pow(log_mag[None], arg[None], taus)
    p_re = pw_re[..., None] * bb_re[None] - pw_im[..., None] * bb_im[None]
    p_im = pw_re[..., None] * bb_im[None] + pw_im[..., None] * bb_re[None]
    kern = (jnp.einsum("ghn,tgnk->tghk", c_re, p_re, precision=hp)
            - jnp.einsum("ghn,tgnk->tghk", c_im, p_im, precision=hp))
    kern_rows = jnp.swapaxes(kern, 2, 3).reshape(r, g * 16, 16)
    kf = jax.vmap(lambda a: _block_diag_expand(a, 16, 16))(kern_rows)
    ns = g * n // S5_SLAB
    slab_cols = lambda m: m.reshape(g * 16, ns, S5_SLAB).transpose(1, 0, 2)
    slab_rows = lambda m: m.reshape(ns, S5_SLAB, g * 16)
    bf = jnp.concatenate([slab_cols(_s5_expand_b(bb_re)), slab_cols(_s5_expand_b(bb_im))], axis=2)
    cf = jnp.concatenate([slab_rows(_s5_expand_c(c_re)), -slab_rows(_s5_expand_c(c_im))], axis=1)
    ks = jnp.arange((S5_LAM_ROWS - 2) // 2, dtype=F32)
    mults = jnp.concatenate([jnp.ones((1,), F32), r * 2.0 ** ks])[:, None, None]
    l_re, l_im = _lam_pow(log_mag[None], arg[None], mults)
    tab = jnp.stack([l_re, l_im], axis=1).reshape(-1, g * n)
    lam = tab.reshape(tab.shape[0], ns, S5_SLAB).transpose(1, 0, 2)
    return kf.astype(BF16), bf.astype(BF16), cf.astype(BF16), lam


def _shift_rows(x, d):
    return jnp.concatenate([jnp.zeros((d, x.shape[1]), x.dtype), x[:x.shape[0] - d]], axis=0)


def _s5_kernel(u_ref, kf_ref, bf_ref, cf_ref, lam_ref, y_ref, xre_ref, xim_ref,
               ubf_ref, xprev_ref, xin_ref, *, seg):
    kf_ref, bf_ref, cf_ref, lam_ref = _drop_layer_axis(kf_ref, bf_ref, cf_ref, lam_ref)
    r = S5_ROW
    rows = seg // r
    ns = bf_ref.shape[0]
    sg = pl.program_id(1)

    @pl.when(sg == 0)
    def _():
        xin_ref[...] = jnp.zeros_like(xin_ref)

    for s in range(r):
        ubf_ref[s] = jnp.concatenate(
            [u_ref[0, pl.ds(s, rows, stride=r), :], u_ref[1, pl.ds(s, rows, stride=r), :]], axis=1).astype(BF16)

    first_row = _iota2((rows, S5_SLAB), 0) == 0
    for j in range(ns):
        l_re, l_im = lam_ref[j, 0:1, :], lam_ref[j, 1:2, :]
        x_re = jnp.zeros((rows, S5_SLAB), F32)
        x_im = jnp.zeros((rows, S5_SLAB), F32)
        for s in range(r):
            z = _dot(ubf_ref[s], bf_ref[j])
            x_re, x_im = (l_re * x_re - l_im * x_im + z[:, :S5_SLAB],
                          l_re * x_im + l_im * x_re + z[:, S5_SLAB:])
        in_re, in_im = xin_ref[0, j, 0:1, :], xin_ref[1, j, 0:1, :]
        m_re, m_im = lam_ref[j, 2:3, :], lam_ref[j, 3:4, :]
        x_re = x_re + jnp.where(first_row, m_re * in_re - m_im * in_im, 0.0)
        x_im = x_im + jnp.where(first_row, m_re * in_im + m_im * in_re, 0.0)
        d, k = 1, 0
        while d < rows:
            m_re, m_im = lam_ref[j, 2 + 2 * k:3 + 2 * k, :], lam_ref[j, 3 + 2 * k:4 + 2 * k, :]
            s_re, s_im = _shift_rows(x_re, d), _shift_rows(x_im, d)
            x_re, x_im = x_re + m_re * s_re - m_im * s_im, x_im + m_re * s_im + m_im * s_re
            d, k = 2 * d, k + 1
        xprev_ref[0, j] = jnp.concatenate([in_re, x_re[:rows - 1]], axis=0)
        xprev_ref[1, j] = jnp.concatenate([in_im, x_im[:rows - 1]], axis=0)
        xin_ref[0, j, 0:1, :] = x_re[rows - 1:rows]
        xin_ref[1, j, 0:1, :] = x_im[rows - 1:rows]

    for p in range(r):
        acc = _dot(ubf_ref[0], kf_ref[p])
        for s in range(1, p + 1):
            acc = acc + _dot(ubf_ref[s], kf_ref[p - s])
        for j in range(ns):
            l_re, l_im = lam_ref[j, 0:1, :], lam_ref[j, 1:2, :]
            x_re, x_im = xprev_ref[0, j], xprev_ref[1, j]
            x_re, x_im = l_re * x_re - l_im * x_im, l_re * x_im + l_im * x_re
            xprev_ref[0, j] = x_re
            xprev_ref[1, j] = x_im
            acc = acc + _dot(jnp.concatenate([x_re, x_im], axis=1).astype(BF16), cf_ref[j])
        y_ref[0, pl.ds(p, rows, stride=r), :] = acc[:, 0:128]
        y_ref[1, pl.ds(p, rows, stride=r), :] = acc[:, 128:256]

    @pl.when(sg == pl.num_programs(1) - 1)
    def _():
        xre_ref[0] = jnp.concatenate([xin_ref[0, j, 0:1, :] for j in range(ns)], axis=1)
        xim_ref[0] = jnp.concatenate([xin_ref[1, j, 0:1, :] for j in range(ns)], axis=1)


def _s5_prompt_scan(u_halves, mats, *, batch, seq, layer):
    g, n = S5_GROUPS, S5_STATE
    seg = min(S5_SEG, seq)
    nseg = seq // seg
    rows = seg // S5_ROW
    assert (rows - 1).bit_length() <= (S5_LAM_ROWS - 2) // 2
    ns = g * n // S5_SLAB
    full = lambda a: _layer_spec(a, layer)
    halves = pl.BlockSpec((2, seg, 128), lambda b, s: (0, b * nseg + s, 0))
    state = pl.BlockSpec((1, 1, g * n), lambda b, s: (b, 0, 0))
    y, x_re, x_im = pl.pallas_call(
        functools.partial(_s5_kernel, seg=seg),
        grid=(batch, nseg),
        in_specs=[halves] + [full(a) for a in mats],
        out_specs=[halves, state, state],
        out_shape=[jax.ShapeDtypeStruct((2, batch * seq, 128), F32),
                   jax.ShapeDtypeStruct((batch, 1, g * n), F32),
                   jax.ShapeDtypeStruct((batch, 1, g * n), F32)],
        scratch_shapes=[pltpu.VMEM((S5_ROW, rows, 256), BF16),
                        pltpu.VMEM((2, ns, rows, S5_SLAB), F32),
                        pltpu.VMEM((2, ns, 8, S5_SLAB), F32)],
        compiler_params=_cparams("arbitrary", "arbitrary"),
        name="prompt_s5",
    )(u_halves, *mats)
    return y, x_re.reshape(batch, g, n), x_im.reshape(batch, g, n)


D_FF = 2816
FFN_SPLIT = 2
POST_SUB = 2


def _layer_norm(x, g, b):
    mu = jnp.mean(x, axis=-1, keepdims=True)
    xc = x - mu
    var = jnp.mean(xc * xc, axis=-1, keepdims=True)
    return xc * lax.rsqrt(var + EPS) * g + b


def _post_kernel(*refs, t, decode, alpha):
    x_ref, oa_ref, ob_ref, yc_ref, u_ref, od_ref = refs[:6]
    (s5d_ref, wglu_ref, bglu_ref, wout_ref, ln1g_ref, ln1b_ref,
     wup_ref, fcw_ref, fcb_ref, wdown_ref, ln2g_ref, ln2b_ref) = _drop_layer_axis(*refs[6:18])
    if decode:
        st_ref, xo_ref, sto_ref = refs[18:]
    else:
        xo_ref, sto_ref, tail_ref = refs[18:]
        i = pl.program_id(1)

        @pl.when(i == 0)
        def _():
            tail_ref[...] = jnp.zeros_like(tail_ref)

    nsub = 1 if decode else POST_SUB
    ts = t // nsub
    width = D_FF // FFN_SPLIT
    prev_tail = [None] * FFN_SPLIT
    outs = []
    for sub in range(nsub):
        r0, r1 = sub * ts, (sub + 1) * ts
        if decode:
            yc_pre, u = yc_ref[...], u_ref[...]
        else:
            yc_pre = jnp.concatenate([yc_ref[0, r0:r1], yc_ref[1, r0:r1]], axis=1)
            u = jnp.concatenate([u_ref[0, r0:r1], u_ref[1, r0:r1]], axis=1)
        yc = _gelu_tanh(yc_pre + s5d_ref[...] * u)
        oc = yc * _sigmoid(_dot(yc.astype(BF16), wglu_ref[...]) + bglu_ref[...])
        mix = _dot(oa_ref[r0:r1, :].astype(BF16), wout_ref[0:256, :])
        mix = mix + _dot(ob_ref[r0:r1, :].astype(BF16), wout_ref[256:512, :])
        mix = mix + _dot(oc.astype(BF16), wout_ref[512:768, :])
        mix = mix + _dot(od_ref[r0:r1, :].astype(BF16), wout_ref[768:1024, :])
        x1 = _layer_norm(alpha * x_ref[r0:r1, :] + mix, ln1g_ref[...], ln1b_ref[...])
        x1b = x1.astype(BF16)
        ffn = None
        for c in range(FFN_SPLIT):
            lo, hi = c * width, (c + 1) * width
            uh = _dot(x1b, wup_ref[:, lo:hi])
            gh = _dot(x1b, wup_ref[:, D_FF + lo:D_FF + hi])
            if decode:
                s0 = st_ref[:, lo:hi]
                s1 = st_ref[:, D_FF + lo:D_FF + hi]
                gconv = (fcb_ref[:, lo:hi] + fcw_ref[0:1, lo:hi] * s0 + fcw_ref[1:2, lo:hi] * s1
                         + fcw_ref[2:3, lo:hi] * gh)
                sto_ref[:, lo:hi] = s1
                sto_ref[:, D_FF + lo:D_FF + hi] = gh
            else:
                tail = tail_ref[:, lo:hi] if sub == 0 else prev_tail[c]
                ext = jnp.concatenate([tail, gh], axis=0)
                gconv = fcb_ref[:, lo:hi]
                for j in range(3):
                    gconv = gconv + ext[6 + j:6 + j + ts] * fcw_ref[j:j + 1, lo:hi]
                prev_tail[c] = gh[ts - 8:ts]
                if sub == nsub - 1:
                    tail_ref[:, lo:hi] = gh[ts - 8:ts]

                    @pl.when(i == pl.num_programs(1) - 1)
                    def _():
                        sto_ref[0, :, lo:hi] = ext[ts + 6:ts + 8]

            hmid = (_gelu_tanh(gconv) * uh).astype(BF16)
            part = _dot(hmid, wdown_ref[lo:hi, :])
            ffn = part if ffn is None else ffn + part
        outs.append(_layer_norm(alpha * x1 + ffn, ln2g_ref[...], ln2b_ref[...]))
    xo_ref[...] = outs[0] if nsub == 1 else jnp.concatenate(outs, axis=0)


def _post(x, oa, ob, yc, u, od, weights, *, alpha, layer, batch=None, seq=None, t=None, ffn_state=None):
    decode = ffn_state is not None
    n = x.shape[0]
    if decode:
        t = n
        grid = (1,)
        row = lambda i: (0, 0)
        extra_in = [pl.BlockSpec(ffn_state.shape, row)]
        extra_args = [ffn_state]
        out_specs = [pl.BlockSpec((t, 1024), row), pl.BlockSpec((t, 2 * D_FF), row)]
        out_shape = [jax.ShapeDtypeStruct((n, 1024), F32), jax.ShapeDtypeStruct((n, 2 * D_FF), F32)]
        scratch = []
        sem = ("arbitrary",)
    else:
        nt = seq // t
        grid = (batch, nt)
        row = lambda b, i: (b * nt + i, 0)
        extra_in, extra_args = [], []
        out_specs = [pl.BlockSpec((t, 1024), row), pl.BlockSpec((1, 2, D_FF), lambda b, i: (b, 0, 0))]
        out_shape = [jax.ShapeDtypeStruct((n, 1024), F32), jax.ShapeDtypeStruct((batch, 2, D_FF), F32)]
        scratch = [pltpu.VMEM((8, D_FF), F32)]
        sem = ("arbitrary", "arbitrary")
    acts = (x, oa, ob, yc, u, od)

    def act_spec(a):
        if a.ndim == 3:
            return pl.BlockSpec((2, t, 128), lambda b, i: (0, b * nt + i, 0))
        return pl.BlockSpec((t, a.shape[1]), row)

    return pl.pallas_call(
        functools.partial(_post_kernel, t=t, decode=decode, alpha=alpha),
        grid=grid,
        in_specs=[act_spec(a) for a in acts]
        + [_layer_spec(w, layer, pipeline_mode=pl.Buffered(1)) for w in weights] + extra_in,
        out_specs=out_specs,
        out_shape=out_shape,
        scratch_shapes=scratch,
        compiler_params=_cparams(*sem),
        name="post_decode" if decode else "post_prompt",
    )(*acts, *weights, *extra_args)


PAGE = 128
PAGES_PER_STEP = 64


def _decode_attn_kernel(pt_ref, q_ref, bias_ref, *refs, pp):
    del pt_ref
    (bias_ref,) = _drop_layer_axis(bias_ref)
    k_refs, v_refs = refs[:pp], refs[pp:2 * pp]
    o_ref, qb_ref, c_ref, acc_ref = refs[2 * pp:]
    s = pl.program_id(1)
    eye = _iota2((256, 256), 0) == _iota2((256, 256), 1)

    @pl.when(s == 0)
    def _():
        q_col = jnp.sum(jnp.where(eye, q_ref[0], 0.0), axis=1, keepdims=True)
        qb_ref[...] = jnp.broadcast_to(q_col, (256, PAGE))
        c_ref[...] = jnp.zeros_like(c_ref)
        acc_ref[...] = jnp.zeros_like(acc_ref)

    ones_sfx = _suffix_ones(PAGE)
    qb = qb_ref[...]
    c = c_ref[...]
    acc = acc_ref[...]
    for r in reversed(range(pp)):
        prod = k_refs[r][0] * qb
        z = jnp.concatenate(
            [jnp.sum(prod[h * HEAD_DIM:(h + 1) * HEAD_DIM], axis=0, keepdims=True) for h in range(4)]
            + [jnp.zeros((4, PAGE), F32)], axis=0) + bias_ref[...]
        sp = _softplus2(z)
        rest = _dot(sp.astype(BF16), ones_sfx)
        w = jnp.exp2(z - rest - c)
        v_t = v_refs[r][0]
        acc = acc + jnp.concatenate(
            [v_t[h * HEAD_DIM:(h + 1) * HEAD_DIM] * w[h:h + 1, :] for h in range(4)], axis=0)
        c = c + rest[:, 0:1]
    c_ref[...] = c
    acc_ref[...] = acc

    @pl.when(s == pl.num_programs(1) - 1)
    def _():
        o_col = jnp.sum(acc, axis=1, keepdims=True)
        o_ref[0] = jnp.sum(jnp.where(eye, o_col, 0.0), axis=0, keepdims=True)


def _decode_attention(q, pool_kt, pool_vt, page_table, bias8, *, base, layer):
    nseq, npages = page_table.shape
    pp = PAGES_PER_STEP
    nsteps = npages // pp

    def page_map(r):
        return lambda b, s, pt: (base + pt[b, (nsteps - 1 - s) * pp + r], 0, 0)

    page_specs = [pl.BlockSpec((1, 256, PAGE), page_map(r)) for r in range(pp)]
    out = pl.pallas_call(
        functools.partial(_decode_attn_kernel, pp=pp),
        grid_spec=pltpu.PrefetchScalarGridSpec(
            num_scalar_prefetch=1,
            grid=(nseq, nsteps),
            in_specs=[pl.BlockSpec((1, 1, 256), lambda b, s, pt: (b, 0, 0)),
                      _layer_spec(bias8, layer)] + page_specs + page_specs,
            out_specs=pl.BlockSpec((1, 1, 256), lambda b, s, pt: (b, 0, 0)),
            scratch_shapes=[pltpu.VMEM((256, PAGE), F32), pltpu.VMEM((8, PAGE), F32), pltpu.VMEM((256, PAGE), F32)],
        ),
        out_shape=jax.ShapeDtypeStruct((nseq, 1, 256), F32),
        compiler_params=_cparams("arbitrary", "arbitrary"),
        name="decode_attention",
    )(page_table, q.reshape(nseq, 1, 256), bias8, *([pool_kt] * pp), *([pool_vt] * pp))
    return out.reshape(nseq, 256)


def _dot_f32(a, b):
    a0, a1, a2 = _split3(a)
    b0, b1, b2 = _split3(b)
    return (_dot(a0, b0) + (_dot(a0, b1) + _dot(a1, b0))
            + (_dot(a0, b2) + _dot(a2, b0) + _dot(a1, b1)))


_R_DEC, _R_XDT, _R_BM, _R_CM, _R_F, _R_K, _R_Q, _R_V, _R_END = 0, 256, 512, 640, 768, 1024, 1280, 1536, 1792


def _decode_mixers_kernel(zx_ref, u_ref, hg_ref, cs_ref, hssm_ref, x0re_ref, x0im_ref, hhg_ref,
                          cw_ref, cb_ref, dtb_ref, alog_ref, d_ref, ngs_ref, lb_ref, ngh_ref,
                          lre_ref, lim_ref, bre_ref, bim_ref, cre_ref, cim_ref,
                          ob_ref, cso_ref, hssmo_ref, yc_ref, xre_ref, xim_ref, od_ref, hhgo_ref,
                          rows_ref, yssm_ref, yhg_ref, *, nseq):
    (cw_ref, cb_ref, dtb_ref, alog_ref, d_ref, ngs_ref, lb_ref, ngh_ref,
     lre_ref, lim_ref, bre_ref, bim_ref, cre_ref, cim_ref) = _drop_layer_axis(
        cw_ref, cb_ref, dtb_ref, alog_ref, d_ref, ngs_ref, lb_ref, ngh_ref,
        lre_ref, lim_ref, bre_ref, bim_ref, cre_ref, cim_ref)
    z = zx_ref[:, 0:256]
    xbc = zx_ref[:, 256:768]
    s0, s1, s2 = cs_ref[:, 0:512], cs_ref[:, 512:1024], cs_ref[:, 1024:1536]
    conv = cb_ref[...] + cw_ref[0:1, :] * s0 + cw_ref[1:2, :] * s1 + cw_ref[2:3, :] * s2 + cw_ref[3:4, :] * xbc
    cso_ref[:, 0:512] = s1
    cso_ref[:, 512:1024] = s2
    cso_ref[:, 1024:1536] = xbc
    act = _silu(conv)
    xs = act[:, 0:256]
    dt = _softplus(zx_ref[:, 768:1024] + dtb_ref[...])
    rows_ref[:, _R_DEC:_R_XDT] = jnp.exp(dt * (-jnp.exp(alog_ref[...])))
    rows_ref[:, _R_XDT:_R_BM] = xs * dt
    rows_ref[:, _R_BM:_R_F] = act[:, 256:512]
    lb = lb_ref[...]
    fr = hg_ref[:, 256:512]
    rows_ref[:, _R_F:_R_K] = lb + (1.0 - lb) * _sigmoid(fr)
    rows_ref[:, _R_K:_R_Q] = (1.0 - lb) * _sigmoid(-fr)
    rows_ref[:, _R_Q:_R_V] = _silu(hg_ref[:, 0:256])
    rows_ref[:, _R_V:_R_END] = hg_ref[:, 512:768]

    eye = _iota2((256, 256), 0) == _iota2((256, 256), 1)
    row_id = _iota2((256, 64), 0)

    def to_col(r):
        return jnp.sum(jnp.where(eye, r, 0.0), axis=1, keepdims=True)

    def to_row(c):
        return jnp.sum(jnp.where(eye, c, 0.0), axis=0, keepdims=True)

    def per_seq(b, get):
        bm = get(_R_BM, _R_CM)
        cm = get(_R_CM, _R_F)
        bm_rows = jnp.where(row_id < 128, bm[:, 0:64], bm[:, 64:128])
        cm_rows = jnp.where(row_id < 128, cm[:, 0:64], cm[:, 64:128])
        hn = hssm_ref[b] * to_col(get(_R_DEC, _R_XDT)) + to_col(get(_R_XDT, _R_BM)) * bm_rows
        hssmo_ref[b] = hn
        y_ssm = to_row(jnp.sum(hn * cm_rows, axis=1, keepdims=True))
        v = get(_R_V, _R_END)
        v_rows = jnp.where(row_id < 64, v[:, 0:64],
                           jnp.where(row_id < 128, v[:, 64:128],
                                     jnp.where(row_id < 192, v[:, 128:192], v[:, 192:256])))
        gn = hhg_ref[b] * to_col(get(_R_F, _R_K)) + to_col(get(_R_K, _R_Q)) * v_rows
        hhgo_ref[b] = gn
        qg = to_col(get(_R_Q, _R_V)) * gn
        y_hg = jnp.concatenate(
            [jnp.sum(qg[h * 64:(h + 1) * 64], axis=0, keepdims=True) for h in range(4)], axis=1)
        return y_ssm, y_hg

    def per_octet(o, carry):
        base = pl.multiple_of(o * 8, 8)
        blk = rows_ref[pl.ds(base, 8), :]
        ys = [per_seq(base + r, lambda lo, hi, r=r: blk[r:r + 1, lo:hi]) for r in range(8)]
        yssm_ref[pl.ds(base, 8), :] = jnp.concatenate([y[0] for y in ys], axis=0)
        yhg_ref[pl.ds(base, 8), :] = jnp.concatenate([y[1] for y in ys], axis=0)
        return carry

    lax.fori_loop(0, nseq // 8, per_octet, 0)

    y = yssm_ref[...] + d_ref[...] * xs
    ob_ref[...] = _group_rms(y * _silu(z), ngs_ref[...], 128)
    od_ref[...] = _head_rms(yhg_ref[...], ngh_ref[...]) * _silu(hg_ref[:, 768:1024])
    u = u_ref[...]
    x0r, x0i = x0re_ref[...], x0im_ref[...]
    lr, li = lre_ref[...], lim_ref[...]
    xr = lr * x0r - li * x0i + _dot_f32(u, bre_ref[...])
    xi = lr * x0i + li * x0r + _dot_f32(u, bim_ref[...])
    xre_ref[...] = xr
    xim_ref[...] = xi
    yc_ref[...] = _dot_f32(xr, cre_ref[...]) - _dot_f32(xi, cim_ref[...])


def _decode_mixers(zx, u, hg4, conv_state, h_ssm, x0_re, x0_im, h_hg, params, *, layer):
    nseq = zx.shape[0]
    acts = (zx, u, hg4, conv_state, h_ssm, x0_re, x0_im, h_hg)
    args = (*acts, *params)
    full = lambda a: pl.BlockSpec(a.shape, lambda i: (0,) * a.ndim)
    out_shape = [
        jax.ShapeDtypeStruct((nseq, 256), F32),
        jax.ShapeDtypeStruct((nseq, 1536), F32),
        jax.ShapeDtypeStruct((nseq, 256, 64), F32),
        jax.ShapeDtypeStruct((nseq, 256), F32),
        jax.ShapeDtypeStruct((nseq, 1024), F32),
        jax.ShapeDtypeStruct((nseq, 1024), F32),
        jax.ShapeDtypeStruct((nseq, 256), F32),
        jax.ShapeDtypeStruct((nseq, 256, 64), F32),
    ]
    return pl.pallas_call(
        functools.partial(_decode_mixers_kernel, nseq=nseq),
        grid=(1,),
        in_specs=[full(a) for a in acts] + [_layer_spec(p, layer) for p in params],
        out_specs=[full(s) for s in out_shape],
        out_shape=out_shape,
        scratch_shapes=[pltpu.VMEM((nseq, _R_END), F32), pltpu.VMEM((nseq, 256), F32), pltpu.VMEM((nseq, 256), F32)],
        compiler_params=_cparams("arbitrary"),
        name="decode_mixers",
    )(*args)


def _s5_decode_matrices(log_mag, arg, bb_re, bb_im, c_re, c_im):
    g, n = S5_GROUPS, S5_STATE
    lam_re, lam_im = _lam_pow(log_mag, arg, 1.0)
    return (lam_re.reshape(1, g * n), lam_im.reshape(1, g * n),
            _s5_expand_b(bb_re), _s5_expand_b(bb_im), _s5_expand_c(c_re), _s5_expand_c(c_im))


ATTN_TQ = 1024
ATTN_TK = 256
PROJ_TILE = 512
SSD_TILE = 256
HGRN_TILE = 256
POST_TILE = 512


def _rearranged_w_in(w_in):
    q, k, v, z, xbc, dt, u, hq, hf, hi, hg = jnp.split(
        w_in, [256, 512, 768, 1024, 1536, 1540, 1796, 2052, 2308, 2564], axis=1)
    dt_full = jnp.repeat(dt, HEAD_DIM, axis=1)
    q_scale = LOG2E * HEAD_DIM ** -0.5
    return jnp.concatenate([q * q_scale, k, v, z, xbc, dt_full, u, hq, hf, hi, hg], axis=1).astype(BF16)


def kernel(x_prompt, x_sample, cache_k, cache_v, state_ssm_conv, state_ssm, state_s5_re, state_s5_im, state_hgrn, state_ffn_conv, page_table, ln1_g, ln1_b, ln2_g, ln2_b, w_in, w_out, sb_logit_bias, ssm_conv_w, ssm_conv_b, ssm_dt_bias, ssm_a_log, ssm_d, ssm_norm_g, s5_a_re, s5_a_im, s5_b_re, s5_b_im, s5_c_re, s5_c_im, s5_d, s5_log_dt, s5_w_glu, s5_b_glu, hg_lb_logits, hg_norm_g, w_up, ffn_conv_w, ffn_conv_b, w_down):
    depth = w_in.shape[0]
    bp, seq, dm = x_prompt.shape
    ns = x_sample.shape[0]
    n_phys = cache_k.shape[1]
    alpha = (2 * depth) ** 0.25
    row = lambda a: a[:, None, :]
    rep = lambda a: jnp.repeat(a, HEAD_DIM, axis=1)[:, None, :]
    pr = jax.nn.softmax(hg_lb_logits.astype(F32), axis=0)
    lb = row(jnp.cumsum(pr, axis=0) - pr[0:1])
    ngh = row(hg_norm_g)
    w_p = jax.vmap(_rearranged_w_in)(w_in)
    disc = jax.vmap(_s5_discretize)(s5_a_re, s5_a_im, s5_b_re, s5_b_im, s5_log_dt)
    s5_mats = jax.vmap(_s5_prompt_matrices)(*disc, s5_c_re, s5_c_im)
    ssd_w = (ssm_conv_w, row(ssm_conv_b), rep(ssm_dt_bias), rep(ssm_a_log), rep(ssm_d), row(ssm_norm_g))
    post_w = (row(s5_d), s5_w_glu.astype(BF16), row(s5_b_glu), w_out.astype(BF16), row(ln1_g), row(ln1_b),
              w_up.astype(BF16), ffn_conv_w, row(ffn_conv_b), w_down.astype(BF16), row(ln2_g), row(ln2_b))
    dec_w = (*ssd_w, lb, ngh, *jax.vmap(_s5_decode_matrices)(*disc, s5_c_re, s5_c_im))
    bias8 = jnp.broadcast_to(jnp.pad(sb_logit_bias * LOG2E, ((0, 0), (0, 4)))[:, :, None], (depth, 8, PAGE))
    pool_kt = cache_k.transpose(0, 1, 3, 4, 2).reshape(depth * n_phys, 256, PAGE)
    pool_vt = cache_v.transpose(0, 1, 3, 4, 2).reshape(depth * n_phys, 256, PAGE)

    xp = x_prompt.reshape(bp * seq, dm)
    xs = x_sample.reshape(ns, dm)
    k_all = jnp.zeros((depth, bp, 256, seq), F32)
    v_all = jnp.zeros((depth, bp, 256, seq), F32)
    outs_p, outs_s = [], []
    for l in range(depth):
        q, vb, zx, u, hg4, kt, k_all, v_all = _projection(
            xp, w_p, tm=PROJ_TILE, tk=ATTN_TK, seq=seq, layer=l, k_all=k_all, v_all=v_all)
        oa = _prompt_attention(q, kt, vb, sb_logit_bias, batch=bp, seq=seq, tq=ATTN_TQ, tk=ATTN_TK, layer=l)
        ob, conv_p, ssm_p = _prompt_ssd(zx, *ssd_w, batch=bp, seq=seq, t=SSD_TILE, layer=l)
        yc, re_p, im_p = _s5_prompt_scan(u, s5_mats, batch=bp, seq=seq, layer=l)
        od, hg_p = _prompt_hgrn(hg4, lb, ngh, batch=bp, seq=seq, t=HGRN_TILE, layer=l)
        xp, ffn_p = _post(xp, oa, ob, yc, u, od, post_w, alpha=alpha, layer=l, batch=bp, seq=seq, t=POST_TILE)
        outs_p.append((conv_p, ssm_p, re_p, im_p, hg_p, ffn_p))

        q, k, v, zx, u, hg4 = _projection(xs, w_p, tm=ns, layer=l)
        oa = _decode_attention(q, pool_kt, pool_vt, page_table, bias8, base=l * n_phys, layer=l)
        ob, conv_s, ssm_s, yc, re_s, im_s, od, hg_s = _decode_mixers(
            zx, u, hg4, state_ssm_conv[l].reshape(ns, 1536), state_ssm[l].reshape(ns, 256, 64),
            state_s5_re[l].reshape(ns, 1024), state_s5_im[l].reshape(ns, 1024),
            state_hgrn[l].reshape(ns, 256, 64), dec_w, layer=l)
        xs, ffn_s = _post(xs, oa, ob, yc, u, od, post_w, alpha=alpha, layer=l,
                          ffn_state=state_ffn_conv[l].reshape(ns, 2 * D_FF))
        outs_s.append((k.reshape(ns, 1, 4, HEAD_DIM), v.reshape(ns, 1, 4, HEAD_DIM),
                       conv_s.reshape(ns, 3, 512), ssm_s.reshape(ns, 4, 64, 64),
                       re_s.reshape(ns, S5_GROUPS, S5_STATE), im_s.reshape(ns, S5_GROUPS, S5_STATE),
                       hg_s.reshape(ns, 4, 64, 64), ffn_s.reshape(ns, 2, D_FF)))

    sp = [jnp.stack(col, axis=0) for col in zip(*outs_p)]
    ss = [jnp.stack(col, axis=0) for col in zip(*outs_s)]
    k_prompt = k_all.reshape(depth, bp, 4, HEAD_DIM, seq).transpose(0, 1, 4, 2, 3)
    v_prompt = v_all.reshape(depth, bp, 4, HEAD_DIM, seq).transpose(0, 1, 4, 2, 3)
    return (xp.reshape(bp, seq, dm), xs.reshape(ns, 1, dm), k_prompt, v_prompt, ss[0], ss[1], sp[0], ss[2],
            sp[1], ss[3], sp[2], ss[4], sp[3], ss[5], sp[4], ss[6], sp[5], ss[7])
```

```python
import functools
import math

import jax
import jax.numpy as jnp
from jax import lax
from jax.experimental import pallas as pl
from jax.experimental.pallas import tpu as pltpu

F32 = jnp.float32
BF16 = jnp.bfloat16

HEAD_DIM = 64
EPS = 1e-5
NEG_BIG = -1e30

VMEM_LIMIT_BYTES = 56 * 1024 * 1024


def _cparams(*sem):
    return pltpu.CompilerParams(dimension_semantics=sem, vmem_limit_bytes=VMEM_LIMIT_BYTES)


def _layer_spec(a, layer, **kw):
    return pl.BlockSpec((1,) + a.shape[1:], lambda *_: (layer,) + (0,) * (a.ndim - 1), **kw)


def _drop_layer_axis(*refs):
    return tuple(r.at[0] for r in refs)


def _dot(a, b):
    return jnp.dot(a, b, preferred_element_type=F32)


def _dot_nt(a, b):
    return lax.dot_general(a, b, (((1,), (1,)), ((), ())), preferred_element_type=F32)


def _dot_tn(a, b):
    return lax.dot_general(a, b, (((0,), (0,)), ((), ())), preferred_element_type=F32)


def _split2(x):
    hi = x.astype(BF16)
    lo = (x - hi.astype(F32)).astype(BF16)
    return hi, lo


def _split3(x):
    hi = x.astype(BF16)
    r = x - hi.astype(F32)
    mid = r.astype(BF16)
    lo = (r - mid.astype(F32)).astype(BF16)
    return hi, mid, lo


def _mm01_left(m01, x, passes):
    parts = _split3(x) if passes == 3 else _split2(x)
    acc = _dot(m01, parts[0])
    for p in parts[1:]:
        acc = acc + _dot(m01, p)
    return acc


def _mm01_right(x, m01, passes):
    parts = _split3(x) if passes == 3 else _split2(x)
    acc = _dot(parts[0], m01)
    for p in parts[1:]:
        acc = acc + _dot(p, m01)
    return acc


def _sigmoid(x):
    return 1.0 / (1.0 + jnp.exp(-x))


def _silu(x):
    return x * _sigmoid(x)


def _softplus(x):
    return jnp.maximum(x, 0.0) + jnp.log(1.0 + jnp.exp(-jnp.abs(x)))


def _gelu_tanh(x):
    c = math.sqrt(2.0 / math.pi)
    return 0.5 * x * (1.0 + jnp.tanh(c * (x + 0.044715 * (x * x * x))))


def _iota2(shape, dim):
    return lax.broadcasted_iota(jnp.int32, shape, dim)


PROJ_W = 3072


def _proj_kernel(x_ref, w_ref, *refs, tk):
    (w_ref,) = _drop_layer_axis(w_ref)
    x = x_ref[...].astype(BF16)

    def mm(lo, hi):
        return _dot(x, w_ref[:, lo:hi])

    q = mm(0, 256)
    k = mm(256, 512)
    v = mm(512, 768)
    u = mm(1792, 2048)
    if tk:
        _, _, q_ref, vb_ref, zx_ref, u_ref, hg_ref, kt_ref, kall_ref, vall_ref = refs
        u_ref[0] = u[:, 0:128]
        u_ref[1] = u[:, 128:256]
        q_ref[...] = q.astype(BF16)
        vb_ref[...] = v.astype(BF16)
        k_t = k.T
        kall_ref[0, 0] = k_t
        vall_ref[0, 0] = v.T
        for c in range(k.shape[0] // tk):
            kt_ref[c] = k_t[:, c * tk:(c + 1) * tk].astype(BF16)
    else:
        q_ref, k_ref, v_ref, zx_ref, u_ref, hg_ref = refs
        q_ref[...] = q
        k_ref[...] = k
        v_ref[...] = v
        u_ref[...] = u
    zx_ref[...] = mm(768, 1792)
    hg_ref[...] = mm(2048, 3072)


def _projection(x2d, w_p, *, tm, tk=0, seq=None, layer=None, k_all=None, v_all=None):
    n, d = x2d.shape
    row = lambda i: (i, 0)
    rows = lambda width, dtype: (jax.ShapeDtypeStruct((n, width), dtype), pl.BlockSpec((tm, width), row))
    in_specs = [pl.BlockSpec((tm, d), row), _layer_spec(w_p, layer)]
    args = [x2d, w_p]
    aliases = {}
    if tk:
        nt = seq // tm
        slab = pl.BlockSpec((1, 1, 256, tm), lambda i: (layer, i // nt, 0, i % nt))
        halves = (jax.ShapeDtypeStruct((2, n, 128), F32), pl.BlockSpec((2, tm, 128), lambda i: (0, i, 0)))
        outs = [rows(256, BF16), rows(256, BF16), rows(1024, F32), halves, rows(1024, F32),
                (jax.ShapeDtypeStruct((n // tk, 256, tk), BF16), pl.BlockSpec((tm // tk, 256, tk), lambda i: (i, 0, 0))),
                (jax.ShapeDtypeStruct(k_all.shape, F32), slab), (jax.ShapeDtypeStruct(v_all.shape, F32), slab)]
        in_specs += [pl.BlockSpec(memory_space=pl.ANY), pl.BlockSpec(memory_space=pl.ANY)]
        args += [k_all, v_all]
        aliases = {2: 6, 3: 7}
    else:
        outs = [rows(256, F32), rows(256, F32), rows(256, F32), rows(1024, F32), rows(256, F32), rows(1024, F32)]
    return pl.pallas_call(
        functools.partial(_proj_kernel, tk=tk),
        grid=(n // tm,),
        in_specs=in_specs,
        out_specs=[o[1] for o in outs],
        out_shape=[o[0] for o in outs],
        input_output_aliases=aliases,
        compiler_params=_cparams("arbitrary"),
        name="projection",
    )(*args)


LOG2E = 1.4426950408889634


def _suffix_ones(n):
    return jnp.where(_iota2((n, n), 0) >= _iota2((n, n), 1), 1.0, 0.0).astype(BF16)


def _softplus2(z):
    return jnp.maximum(z, 0.0) + jnp.log2(1.0 + jnp.exp2(-jnp.abs(z)))


def _sb_block(qs, kt_pair, v_pair, bias_col, ones_sfx, carry, mask):
    z = _dot(qs, kt_pair) + bias_col
    sp = _softplus2(z)
    if mask is not None:
        sp = jnp.where(mask, sp, 0.0)
    rest = _dot(sp.astype(BF16), ones_sfx)
    w = jnp.exp2(z - rest - carry)
    if mask is not None:
        w = jnp.where(mask, w, 0.0)
    return _dot(w.astype(BF16), v_pair), rest[:, 0:1]


def _attn_kernel(bias_ref, q_ref, kt_ref, v_ref, o_ref, qs_ref, acc_ref, *, tq, tk, layer):
    hp = pl.program_id(1)
    i = pl.program_id(2)
    ones_sfx = _suffix_ones(tk)
    r = tq // tk
    lane_head = _iota2((tq, 128), 1) // HEAD_DIM
    q = q_ref[...]
    qs_ref[0:tq, :] = jnp.where(lane_head == 0, q, jnp.zeros_like(q))
    qs_ref[tq:2 * tq, :] = jnp.where(lane_head == 1, q, jnp.zeros_like(q))
    bias0 = bias_ref[layer, 2 * hp] * LOG2E
    bias1 = bias_ref[layer, 2 * hp + 1] * LOG2E

    def kv(j):
        start = pl.multiple_of(j * tk, tk)
        return kt_ref[j], v_ref[pl.ds(start, tk), :]

    acc = jnp.zeros((2 * tq, 128), F32)
    tot = jnp.zeros((2 * tq, 1), F32)
    for d in reversed(range(r)):
        kt_p, v_p = kv(i * r + d)
        lo, hi = d * tk, (d + 1) * tk
        spans = [(lo, hi, True)] + ([(hi, tq, False)] if hi < tq else [])
        for a, b, masked in spans:
            nd = b - a
            qs_d = jnp.concatenate([qs_ref[a:b, :], qs_ref[tq + a:tq + b, :]], axis=0)
            tot_d = jnp.concatenate([tot[a:b], tot[tq + a:tq + b]], axis=0)
            bias_d = jnp.where(_iota2((2 * nd, 1), 0) < nd, bias0, bias1)
            mask = (_iota2((2 * nd, tk), 1) < _iota2((2 * nd, tk), 0) % nd) if masked else None
            da, dt = _sb_block(qs_d, kt_p, v_p, bias_d, ones_sfx, tot_d, mask)
            za, zt = jnp.zeros((a, 128), F32), jnp.zeros((a, 1), F32)
            zb, zu = jnp.zeros((tq - b, 128), F32), jnp.zeros((tq - b, 1), F32)
            pieces_a = [x for x in (za, da[:nd], zb, za, da[nd:], zb) if x.shape[0]]
            pieces_t = [x for x in (zt, dt[:nd], zu, zt, dt[nd:], zu) if x.shape[0]]
            acc = acc + jnp.concatenate(pieces_a, axis=0)
            tot = tot + jnp.concatenate(pieces_t, axis=0)
    acc_ref[...] = acc
    bias_col = jnp.where(_iota2((2 * tq, 1), 0) < tq, bias0, bias1)

    def body(it, tot):
        for k in range(r):
            kt_p, v_p = kv((i - it) * r - 1 - k)
            da, dt = _sb_block(qs_ref[...], kt_p, v_p, bias_col, ones_sfx, tot, None)
            acc_ref[...] += da
            tot = tot + dt
        return tot

    lax.fori_loop(0, i, body, tot)
    o_ref[...] = jnp.where(lane_head == 0, acc_ref[0:tq, :], acc_ref[tq:2 * tq, :])


def _prompt_attention(q, kt, vb, bias, *, batch, seq, tq, tk, layer):
    n = q.shape[0]
    nq = seq // tq
    return pl.pallas_call(
        functools.partial(_attn_kernel, tq=tq, tk=tk, layer=layer),
        grid=(batch, 2, nq),
        in_specs=[
            pl.BlockSpec(memory_space=pltpu.SMEM),
            pl.BlockSpec((tq, 128), lambda b, hp, i: (b * nq + i, hp)),
            pl.BlockSpec((seq // tk, 128, tk), lambda b, hp, i: (b, hp, 0)),
            pl.BlockSpec((seq, 128), lambda b, hp, i: (b, hp)),
        ],
        out_specs=pl.BlockSpec((tq, 128), lambda b, hp, i: (b * nq + i, hp)),
        out_shape=jax.ShapeDtypeStruct((n, 256), F32),
        scratch_shapes=[pltpu.VMEM((2 * tq, 128), BF16), pltpu.VMEM((2 * tq, 128), F32)],
        compiler_params=_cparams("arbitrary", "arbitrary", "arbitrary"),
        name="prompt_attention",
    )(bias, q, kt, vb)


def _lower_incl(n):
    return jnp.where(_iota2((n, n), 1) <= _iota2((n, n), 0), 1.0, 0.0).astype(BF16)


def _group_rms(x, gain, width):
    outs = []
    for g in range(x.shape[1] // width):
        xg = x[:, g * width:(g + 1) * width]
        ms = jnp.mean(xg * xg, axis=1, keepdims=True)
        outs.append(xg * lax.rsqrt(ms + EPS))
    return jnp.concatenate(outs, axis=1) * gain


def _ssd_kernel(zx_ref, cw_ref, cb_ref, dtb_ref, alog_ref, d_ref, ng_ref,
                o_ref, conv_ref, h_ref, tail_ref, hs_ref, *, t, nb):
    cw_ref, cb_ref, dtb_ref, alog_ref, d_ref, ng_ref = _drop_layer_axis(
        cw_ref, cb_ref, dtb_ref, alog_ref, d_ref, ng_ref)
    i = pl.program_id(0)

    @pl.when(i == 0)
    def _():
        tail_ref[...] = jnp.zeros_like(tail_ref)
        hs_ref[...] = jnp.zeros_like(hs_ref)

    causal = _iota2((t, t), 1) <= _iota2((t, t), 0)
    lower = _lower_incl(t)
    outs, exts = [], []
    for s in range(nb):
        z = zx_ref[s, :, 0:256]
        xbc = zx_ref[s, :, 256:768]
        ext = jnp.concatenate([tail_ref[s], xbc], axis=0)
        exts.append(ext)
        conv = cb_ref[...]
        for j in range(4):
            conv = conv + ext[5 + j:5 + j + t] * cw_ref[j:j + 1, :]
        tail_ref[s] = xbc[t - 8:t]
        act = _silu(conv)
        xs, bm, cm = act[:, 0:256], act[:, 256:384], act[:, 384:512]
        dt = _softplus(zx_ref[s, :, 768:1024] + dtb_ref[...])
        da = dt * (-jnp.exp(alog_ref[...]))
        cum = _mm01_left(lower, da, 3)
        cum_t = cum.T
        xdt = xs * dt
        bmb = bm.astype(BF16)
        cmb = cm.astype(BF16)
        gmat = [_dot_nt(cmb[:, g * 64:(g + 1) * 64], bmb[:, g * 64:(g + 1) * 64]) for g in range(2)]
        ys = []
        for h in range(4):
            hs = slice(h * 64, (h + 1) * 64)
            gs = slice((h // 2) * 64, (h // 2 + 1) * 64)
            cum_h = cum[:, hs]
            seg = cum[:, h * 64:h * 64 + 1] - cum_t[h * 64:h * 64 + 1, :]
            decay = jnp.exp(jnp.where(causal, seg, NEG_BIG))
            y_intra = _dot((gmat[h // 2] * decay).astype(BF16), xdt[:, hs].astype(BF16))
            last = cum_h[t - 1:t, :]
            h_prev = hs_ref[s, h]
            y_inter = _dot_nt((cm[:, gs] * jnp.exp(cum_h)).astype(BF16), h_prev.astype(BF16))
            s_c = _dot_tn((xdt[:, hs] * jnp.exp(last - cum_h)).astype(BF16), bmb[:, gs])
            hs_ref[s, h] = h_prev * jnp.exp(last) + s_c
            ys.append(y_intra + y_inter)
        y = jnp.concatenate(ys, axis=1) + d_ref[...] * xs
        outs.append(_group_rms(y * _silu(z), ng_ref[...], 128))
    o_ref[...] = jnp.stack(outs, axis=0)

    @pl.when(i == pl.num_programs(0) - 1)
    def _():
        for s in range(nb):
            conv_ref[s] = exts[s][t + 5:t + 8]
        h_ref[...] = hs_ref[...]


def _prompt_ssd(zx, cw, cb, dtb, alog, d_full, ng, *, batch, seq, t, layer):
    n = zx.shape[0]
    o, conv, h = pl.pallas_call(
        functools.partial(_ssd_kernel, t=t, nb=batch),
        grid=(seq // t,),
        in_specs=[pl.BlockSpec((batch, t, 1024), lambda i: (0, i, 0))]
        + [_layer_spec(a, layer) for a in (cw, cb, dtb, alog, d_full, ng)],
        out_specs=[
            pl.BlockSpec((batch, t, 256), lambda i: (0, i, 0)),
            pl.BlockSpec((batch, 3, 512), lambda i: (0, 0, 0)),
            pl.BlockSpec((batch, 4, 64, 64), lambda i: (0, 0, 0, 0)),
        ],
        out_shape=[
            jax.ShapeDtypeStruct((batch, seq, 256), F32),
            jax.ShapeDtypeStruct((batch, 3, 512), F32),
            jax.ShapeDtypeStruct((batch, 4, 64, 64), F32),
        ],
        scratch_shapes=[pltpu.VMEM((batch, 8, 512), F32), pltpu.VMEM((batch, 4, 64, 64), F32)],
        compiler_params=_cparams("arbitrary"),
        name="prompt_ssd",
    )(zx.reshape(batch, seq, 1024), cw, cb, dtb, alog, d_full, ng)
    return o.reshape(n, 256), conv, h


HG_CHUNK = 32


def _head_mean_matrix():
    same = (_iota2((256, 256), 0) // 64) == (_iota2((256, 256), 1) // 64)
    return jnp.where(same, 1.0, 0.0).astype(BF16)


def _head_rms(y, gain):
    ms = _mm01_right(y * y, _head_mean_matrix(), 2) * (1.0 / 64.0)
    return y * lax.rsqrt(ms + EPS) * gain


def _hgrn_kernel(hg_ref, lb_ref, ng_ref, o_ref, h_ref, st_ref, *, t, nb):
    lb_ref, ng_ref = _drop_layer_axis(lb_ref, ng_ref)
    i = pl.program_id(0)
    c = HG_CHUNK

    @pl.when(i == 0)
    def _():
        st_ref[...] = jnp.zeros_like(st_ref)

    lb = lb_ref[...]
    same_chunk = (_iota2((t, t), 0) // c) == (_iota2((t, t), 1) // c)
    lmat = jnp.where(same_chunk & (_iota2((t, t), 1) <= _iota2((t, t), 0)), 1.0, 0.0).astype(BF16)
    lane_head = _iota2((1, 256), 1) // 64
    head_lane = [lane_head == h for h in range(4)]
    stack_causal = _iota2((4 * c, c), 1) <= (_iota2((4 * c, c), 0) % c)
    block_diag = (_iota2((256, 256), 0) // 64) == (_iota2((256, 256), 1) // 64)
    pre = []
    for s in range(nb):
        q = _silu(hg_ref[s, :, 0:256])
        fr = hg_ref[s, :, 256:512]
        log_f = jnp.log(lb + (1.0 - lb) * _sigmoid(fr))
        k = (1.0 - lb) * _sigmoid(-fr)
        v = hg_ref[s, :, 512:768]
        pre.append((q, k, v, _mm01_left(lmat, log_f, 3)))
    ys = [[] for _ in range(nb)]
    for n in range(t // c):
        rows = slice(n * c, (n + 1) * c)
        for s in range(nb):
            q, k, v, b_all = pre[s]
            b = b_all[rows]
            ref = b[c // 2 - 1:c // 2]
            last = b[c - 1:c]
            q_c, k_c, v_c = q[rows], k[rows], v[rows]
            qe = q_c * jnp.exp(b - ref)
            ke = (k_c * jnp.exp(ref - b)).astype(BF16)
            kl = (k_c * jnp.exp(last - b)).astype(BF16)
            qb = (q_c * jnp.exp(b)).astype(BF16)
            vb = v_c.astype(BF16)
            q_stack = jnp.concatenate([jnp.where(m, qe, 0.0) for m in head_lane], axis=0).astype(BF16)
            scores = jnp.where(stack_causal, _dot_nt(q_stack, ke), 0.0)
            y4 = _dot(scores.astype(BF16), vb)
            y_intra = jnp.where(head_lane[0], y4[0:c], 0.0)
            for h in range(1, 4):
                y_intra = y_intra + jnp.where(head_lane[h], y4[h * c:(h + 1) * c], 0.0)
            st = st_ref[s]
            y_inter = _dot_nt(qb, st.astype(BF16))
            st_ref[s] = st * jnp.exp(last) + jnp.where(block_diag, _dot_tn(vb, kl), 0.0)
            ys[s].append(y_intra + y_inter)
    outs = []
    for s in range(nb):
        y = jnp.concatenate(ys[s], axis=0)
        outs.append(_head_rms(y, ng_ref[...]) * _silu(hg_ref[s, :, 768:1024]))
    o_ref[...] = jnp.stack(outs, axis=0)

    @pl.when(i == pl.num_programs(0) - 1)
    def _():
        for s in range(nb):
            st_t = st_ref[s].T
            for h in range(4):
                h_ref[s, h] = st_t[h * 64:(h + 1) * 64, h * 64:(h + 1) * 64]


def _prompt_hgrn(hg4, lb, ng, *, batch, seq, t, layer):
    n = hg4.shape[0]
    o, h = pl.pallas_call(
        functools.partial(_hgrn_kernel, t=t, nb=batch),
        grid=(seq // t,),
        in_specs=[pl.BlockSpec((batch, t, 1024), lambda i: (0, i, 0)), _layer_spec(lb, layer), _layer_spec(ng, layer)],
        out_specs=[
            pl.BlockSpec((batch, t, 256), lambda i: (0, i, 0)),
            pl.BlockSpec((batch, 4, 64, 64), lambda i: (0, 0, 0, 0)),
        ],
        out_shape=[jax.ShapeDtypeStruct((batch, seq, 256), F32), jax.ShapeDtypeStruct((batch, 4, 64, 64), F32)],
        scratch_shapes=[pltpu.VMEM((batch, 256, 256), F32)],
        compiler_params=_cparams("arbitrary"),
        name="prompt_hgrn",
    )(hg4.reshape(batch, seq, 1024), lb, ng)
    return o.reshape(n, 256), h


S5_ROW = 16
S5_GROUPS = 16
S5_STATE = 64
S5_SLAB = 128
S5_SEG = 4096
S5_LAM_ROWS = 32


def _s5_discretize(a_re, a_im, b_re, b_im, log_dt):
    dt = jnp.exp(log_dt)[:, None]
    mag = jnp.exp(a_re * dt)
    ab_re = mag * jnp.cos(a_im * dt)
    ab_im = mag * jnp.sin(a_im * dt)
    den = a_re * a_re + a_im * a_im
    coef_re = ((ab_re - 1.0) * a_re + ab_im * a_im) / den
    coef_im = (ab_im * a_re - (ab_re - 1.0) * a_im) / den
    bb_re = coef_re[..., None] * b_re - coef_im[..., None] * b_im
    bb_im = coef_re[..., None] * b_im + coef_im[..., None] * b_re
    return a_re * dt, a_im * dt, bb_re, bb_im


def _lam_pow(log_mag, arg, m):
    mag = jnp.exp(m * log_mag)
    return mag * jnp.cos(m * arg), mag * jnp.sin(m * arg)


def _block_diag_expand(a, rows_per_group, cols_per_group):
    g = S5_GROUPS
    tiled = jnp.dot(a, jnp.tile(jnp.eye(cols_per_group, dtype=F32), (1, g)), precision=lax.Precision.HIGHEST)
    same = (jnp.arange(g * rows_per_group)[:, None] // rows_per_group
            == jnp.arange(g * cols_per_group)[None, :] // cols_per_group)
    return jnp.where(same, tiled, 0.0)


def _s5_expand_b(bb):
    return _block_diag_expand(jnp.swapaxes(bb, 1, 2).reshape(S5_GROUPS * 16, S5_STATE), 16, S5_STATE)


def _s5_expand_c(cc):
    return _block_diag_expand(jnp.swapaxes(cc, 1, 2).reshape(S5_GROUPS * S5_STATE, 16), S5_STATE, 16)


def _s5_prompt_matrices(log_mag, arg, bb_re, bb_im, c_re, c_im):
    r, g, n = S5_ROW, S5_GROUPS, S5_STATE
    hp = lax.Precision.HIGHEST
    taus = jnp.arange(r, dtype=F32)[:, None, None]
    pw_re, pw_im = _lam_pow(log_mag[None], arg[None], taus)
    p_re = pw_re[..., None] * bb_re[None] - pw_im[..., None] * bb_im[None]
    p_im = pw_re[..., None] * bb_im[None] + pw_im[..., None] * bb_re[None]
    kern = (jnp.einsum("ghn,tgnk->tghk", c_re, p_re, precision=hp)
            - jnp.einsum("ghn,tgnk->tghk", c_im, p_im, precision=hp))
    kern_rows = jnp.swapaxes(kern, 2, 3).reshape(r, g * 16, 16)
    kf = jax.vmap(lambda a: _block_diag_expand(a, 16, 16))(kern_rows)
    ns = g * n // S5_SLAB
    slab_cols = lambda m: m.reshape(g * 16, ns, S5_SLAB).transpose(1, 0, 2)
    slab_rows = lambda m: m.reshape(ns, S5_SLAB, g * 16)
    bf = jnp.concatenate([slab_cols(_s5_expand_b(bb_re)), slab_cols(_s5_expand_b(bb_im))], axis=2)
    cf = jnp.concatenate([slab_rows(_s5_expand_c(c_re)), -slab_rows(_s5_expand_c(c_im))], axis=1)
    ks = jnp.arange((S5_LAM_ROWS - 2) // 2, dtype=F32)
    mults = jnp.concatenate([jnp.ones((1,), F32), r * 2.0 ** ks])[:, None, None]
    l_re, l_im = _lam_pow(log_mag[None], arg[None], mults)
    tab = jnp.stack([l_re, l_im], axis=1).reshape(-1, g * n)
    lam = tab.reshape(tab.shape[0], ns, S5_SLAB).transpose(1, 0, 2)
    return kf.astype(BF16), bf.astype(BF16), cf.astype(BF16), lam


def _shift_rows(x, d):
    return jnp.concatenate([jnp.zeros((d, x.shape[1]), x.dtype), x[:x.shape[0] - d]], axis=0)


def _s5_kernel(u_ref, kf_ref, bf_ref, cf_ref, lam_ref, y_ref, xre_ref, xim_ref,
               ubf_ref, xprev_ref, xin_ref, *, seg):
    kf_ref, bf_ref, cf_ref, lam_ref = _drop_layer_axis(kf_ref, bf_ref, cf_ref, lam_ref)
    r = S5_ROW
    rows = seg // r
    ns = bf_ref.shape[0]
    sg = pl.program_id(1)

    @pl.when(sg == 0)
    def _():
        xin_ref[...] = jnp.zeros_like(xin_ref)

    for s in range(r):
        ubf_ref[s] = jnp.concatenate(
            [u_ref[0, pl.ds(s, rows, stride=r), :], u_ref[1, pl.ds(s, rows, stride=r), :]], axis=1).astype(BF16)

    first_row = _iota2((rows, S5_SLAB), 0) == 0
    for j in range(ns):
        l_re, l_im = lam_ref[j, 0:1, :], lam_ref[j, 1:2, :]
        x_re = jnp.zeros((rows, S5_SLAB), F32)
        x_im = jnp.zeros((rows, S5_SLAB), F32)
        for s in range(r):
            z = _dot(ubf_ref[s], bf_ref[j])
            x_re, x_im = (l_re * x_re - l_im * x_im + z[:, :S5_SLAB],
                          l_re * x_im + l_im * x_re + z[:, S5_SLAB:])
        in_re, in_im = xin_ref[0, j, 0:1, :], xin_ref[1, j, 0:1, :]
        m_re, m_im = lam_ref[j, 2:3, :], lam_ref[j, 3:4, :]
        x_re = x_re + jnp.where(first_row, m_re * in_re - m_im * in_im, 0.0)
        x_im = x_im + jnp.where(first_row, m_re * in_im + m_im * in_re, 0.0)
        d, k = 1, 0
        while d < rows:
            m_re, m_im = lam_ref[j, 2 + 2 * k:3 + 2 * k, :], lam_ref[j, 3 + 2 * k:4 + 2 * k, :]
            s_re, s_im = _shift_rows(x_re, d), _shift_rows(x_im, d)
            x_re, x_im = x_re + m_re * s_re - m_im * s_im, x_im + m_re * s_im + m_im * s_re
            d, k = 2 * d, k + 1
        xprev_ref[0, j] = jnp.concatenate([in_re, x_re[:rows - 1]], axis=0)
        xprev_ref[1, j] = jnp.concatenate([in_im, x_im[:rows - 1]], axis=0)
        xin_ref[0, j, 0:1, :] = x_re[rows - 1:rows]
        xin_ref[1, j, 0:1, :] = x_im[rows - 1:rows]

    for p in range(r):
        acc = _dot(ubf_ref[0], kf_ref[p])
        for s in range(1, p + 1):
            acc = acc + _dot(ubf_ref[s], kf_ref[p - s])
        for j in range(ns):
            l_re, l_im = lam_ref[j, 0:1, :], lam_ref[j, 1:2, :]
            x_re, x_im = xprev_ref[0, j], xprev_ref[1, j]
            x_re, x_im = l_re * x_re - l_im * x_im, l_re * x_im + l_im * x_re
            xprev_ref[0, j] = x_re
            xprev_ref[1, j] = x_im
            acc = acc + _dot(jnp.concatenate([x_re, x_im], axis=1).astype(BF16), cf_ref[j])
        y_ref[0, pl.ds(p, rows, stride=r), :] = acc[:, 0:128]
        y_ref[1, pl.ds(p, rows, stride=r), :] = acc[:, 128:256]

    @pl.when(sg == pl.num_programs(1) - 1)
    def _():
        xre_ref[0] = jnp.concatenate([xin_ref[0, j, 0:1, :] for j in range(ns)], axis=1)
        xim_ref[0] = jnp.concatenate([xin_ref[1, j, 0:1, :] for j in range(ns)], axis=1)


def _s5_prompt_scan(u_halves, mats, *, batch, seq, layer):
    g, n = S5_GROUPS, S5_STATE
    seg = min(S5_SEG, seq)
    nseg = seq // seg
    rows = seg // S5_ROW
    assert (rows - 1).bit_length() <= (S5_LAM_ROWS - 2) // 2
    ns = g * n // S5_SLAB
    full = lambda a: _layer_spec(a, layer)
    halves = pl.BlockSpec((2, seg, 128), lambda b, s: (0, b * nseg + s, 0))
    state = pl.BlockSpec((1, 1, g * n), lambda b, s: (b, 0, 0))
    y, x_re, x_im = pl.pallas_call(
        functools.partial(_s5_kernel, seg=seg),
        grid=(batch, nseg),
        in_specs=[halves] + [full(a) for a in mats],
        out_specs=[halves, state, state],
        out_shape=[jax.ShapeDtypeStruct((2, batch * seq, 128), F32),
                   jax.ShapeDtypeStruct((batch, 1, g * n), F32),
                   jax.ShapeDtypeStruct((batch, 1, g * n), F32)],
        scratch_shapes=[pltpu.VMEM((S5_ROW, rows, 256), BF16),
                        pltpu.VMEM((2, ns, rows, S5_SLAB), F32),
                        pltpu.VMEM((2, ns, 8, S5_SLAB), F32)],
        compiler_params=_cparams("arbitrary", "arbitrary"),
        name="prompt_s5",
    )(u_halves, *mats)
    return y, x_re.reshape(batch, g, n), x_im.reshape(batch, g, n)


D_FF = 2816
FFN_SPLIT = 2
POST_SUB = 2


def _layer_norm(x, g, b):
    mu = jnp.mean(x, axis=-1, keepdims=True)
    xc = x - mu
    var = jnp.mean(xc * xc, axis=-1, keepdims=True)
    return xc * lax.rsqrt(var + EPS) * g + b


def _post_kernel(*refs, t, decode, alpha):
    x_ref, oa_ref, ob_ref, yc_ref, u_ref, od_ref = refs[:6]
    (s5d_ref, wglu_ref, bglu_ref, wout_ref, ln1g_ref, ln1b_ref,
     wup_ref, fcw_ref, fcb_ref, wdown_ref, ln2g_ref, ln2b_ref) = _drop_layer_axis(*refs[6:18])
    if decode:
        st_ref, xo_ref, sto_ref = refs[18:]
    else:
        xo_ref, sto_ref, tail_ref = refs[18:]
        i = pl.program_id(1)

        @pl.when(i == 0)
        def _():
            tail_ref[...] = jnp.zeros_like(tail_ref)

    nsub = 1 if decode else POST_SUB
    ts = t // nsub
    width = D_FF // FFN_SPLIT
    prev_tail = [None] * FFN_SPLIT
    outs = []
    for sub in range(nsub):
        r0, r1 = sub * ts, (sub + 1) * ts
        if decode:
            yc_pre, u = yc_ref[...], u_ref[...]
        else:
            yc_pre = jnp.concatenate([yc_ref[0, r0:r1], yc_ref[1, r0:r1]], axis=1)
            u = jnp.concatenate([u_ref[0, r0:r1], u_ref[1, r0:r1]], axis=1)
        yc = _gelu_tanh(yc_pre + s5d_ref[...] * u)
        oc = yc * _sigmoid(_dot(yc.astype(BF16), wglu_ref[...]) + bglu_ref[...])
        mix = _dot(oa_ref[r0:r1, :].astype(BF16), wout_ref[0:256, :])
        mix = mix + _dot(ob_ref[r0:r1, :].astype(BF16), wout_ref[256:512, :])
        mix = mix + _dot(oc.astype(BF16), wout_ref[512:768, :])
        mix = mix + _dot(od_ref[r0:r1, :].astype(BF16), wout_ref[768:1024, :])
        x1 = _layer_norm(alpha * x_ref[r0:r1, :] + mix, ln1g_ref[...], ln1b_ref[...])
        x1b = x1.astype(BF16)
        ffn = None
        for c in range(FFN_SPLIT):
            lo, hi = c * width, (c + 1) * width
            uh = _dot(x1b, wup_ref[:, lo:hi])
            gh = _dot(x1b, wup_ref[:, D_FF + lo:D_FF + hi])
            if decode:
                s0 = st_ref[:, lo:hi]
                s1 = st_ref[:, D_FF + lo:D_FF + hi]
                gconv = (fcb_ref[:, lo:hi] + fcw_ref[0:1, lo:hi] * s0 + fcw_ref[1:2, lo:hi] * s1
                         + fcw_ref[2:3, lo:hi] * gh)
                sto_ref[:, lo:hi] = s1
                sto_ref[:, D_FF + lo:D_FF + hi] = gh
            else:
                tail = tail_ref[:, lo:hi] if sub == 0 else prev_tail[c]
                ext = jnp.concatenate([tail, gh], axis=0)
                gconv = fcb_ref[:, lo:hi]
                for j in range(3):
                    gconv = gconv + ext[6 + j:6 + j + ts] * fcw_ref[j:j + 1, lo:hi]
                prev_tail[c] = gh[ts - 8:ts]
                if sub == nsub - 1:
                    tail_ref[:, lo:hi] = gh[ts - 8:ts]

                    @pl.when(i == pl.num_programs(1) - 1)
                    def _():
                        sto_ref[0, :, lo:hi] = ext[ts + 6:ts + 8]

            hmid = (_gelu_tanh(gconv) * uh).astype(BF16)
            part = _dot(hmid, wdown_ref[lo:hi, :])
            ffn = part if ffn is None else ffn + part
        outs.append(_layer_norm(alpha * x1 + ffn, ln2g_ref[...], ln2b_ref[...]))
    xo_ref[...] = outs[0] if nsub == 1 else jnp.concatenate(outs, axis=0)


def _post(x, oa, ob, yc, u, od, weights, *, alpha, layer, batch=None, seq=None, t=None, ffn_state=None):
    decode = ffn_state is not None
    n = x.shape[0]
    if decode:
        t = n
        grid = (1,)
        row = lambda i: (0, 0)
        extra_in = [pl.BlockSpec(ffn_state.shape, row)]
        extra_args = [ffn_state]
        out_specs = [pl.BlockSpec((t, 1024), row), pl.BlockSpec((t, 2 * D_FF), row)]
        out_shape = [jax.ShapeDtypeStruct((n, 1024), F32), jax.ShapeDtypeStruct((n, 2 * D_FF), F32)]
        scratch = []
        sem = ("arbitrary",)
    else:
        nt = seq // t
        grid = (batch, nt)
        row = lambda b, i: (b * nt + i, 0)
        extra_in, extra_args = [], []
        out_specs = [pl.BlockSpec((t, 1024), row), pl.BlockSpec((1, 2, D_FF), lambda b, i: (b, 0, 0))]
        out_shape = [jax.ShapeDtypeStruct((n, 1024), F32), jax.ShapeDtypeStruct((batch, 2, D_FF), F32)]
        scratch = [pltpu.VMEM((8, D_FF), F32)]
        sem = ("arbitrary", "arbitrary")
    acts = (x, oa, ob, yc, u, od)

    def act_spec(a):
        if a.ndim == 3:
            return pl.BlockSpec((2, t, 128), lambda b, i: (0, b * nt + i, 0))
        return pl.BlockSpec((t, a.shape[1]), row)

    return pl.pallas_call(
        functools.partial(_post_kernel, t=t, decode=decode, alpha=alpha),
        grid=grid,
        in_specs=[act_spec(a) for a in acts]
        + [_layer_spec(w, layer, pipeline_mode=pl.Buffered(1)) for w in weights] + extra_in,
        out_specs=out_specs,
        out_shape=out_shape,
        scratch_shapes=scratch,
        compiler_params=_cparams(*sem),
        name="post_decode" if decode else "post_prompt",
    )(*acts, *weights, *extra_args)


PAGE = 128
PAGES_PER_STEP = 64


def _decode_attn_kernel(pt_ref, q_ref, bias_ref, *refs, pp):
    del pt_ref
    (bias_ref,) = _drop_layer_axis(bias_ref)
    k_refs, v_refs = refs[:pp], refs[pp:2 * pp]
    o_ref, qb_ref, c_ref, acc_ref = refs[2 * pp:]
    s = pl.program_id(1)
    eye = _iota2((256, 256), 0) == _iota2((256, 256), 1)

    @pl.when(s == 0)
    def _():
        q_col = jnp.sum(jnp.where(eye, q_ref[0], 0.0), axis=1, keepdims=True)
        qb_ref[...] = jnp.broadcast_to(q_col, (256, PAGE))
        c_ref[...] = jnp.zeros_like(c_ref)
        acc_ref[...] = jnp.zeros_like(acc_ref)

    ones_sfx = _suffix_ones(PAGE)
    qb = qb_ref[...]
    c = c_ref[...]
    acc = acc_ref[...]
    for r in reversed(range(pp)):
        prod = k_refs[r][0] * qb
        z = jnp.concatenate(
            [jnp.sum(prod[h * HEAD_DIM:(h + 1) * HEAD_DIM], axis=0, keepdims=True) for h in range(4)]
            + [jnp.zeros((4, PAGE), F32)], axis=0) + bias_ref[...]
        sp = _softplus2(z)
        rest = _dot(sp.astype(BF16), ones_sfx)
        w = jnp.exp2(z - rest - c)
        v_t = v_refs[r][0]
        acc = acc + jnp.concatenate(
            [v_t[h * HEAD_DIM:(h + 1) * HEAD_DIM] * w[h:h + 1, :] for h in range(4)], axis=0)
        c = c + rest[:, 0:1]
    c_ref[...] = c
    acc_ref[...] = acc

    @pl.when(s == pl.num_programs(1) - 1)
    def _():
        o_col = jnp.sum(acc, axis=1, keepdims=True)
        o_ref[0] = jnp.sum(jnp.where(eye, o_col, 0.0), axis=0, keepdims=True)


def _decode_attention(q, pool_kt, pool_vt, page_table, bias8, *, base, layer):
    nseq, npages = page_table.shape
    pp = PAGES_PER_STEP
    nsteps = npages // pp

    def page_map(r):
        return lambda b, s, pt: (base + pt[b, (nsteps - 1 - s) * pp + r], 0, 0)

    page_specs = [pl.BlockSpec((1, 256, PAGE), page_map(r)) for r in range(pp)]
    out = pl.pallas_call(
        functools.partial(_decode_attn_kernel, pp=pp),
        grid_spec=pltpu.PrefetchScalarGridSpec(
            num_scalar_prefetch=1,
            grid=(nseq, nsteps),
            in_specs=[pl.BlockSpec((1, 1, 256), lambda b, s, pt: (b, 0, 0)),
                      _layer_spec(bias8, layer)] + page_specs + page_specs,
            out_specs=pl.BlockSpec((1, 1, 256), lambda b, s, pt: (b, 0, 0)),
            scratch_shapes=[pltpu.VMEM((256, PAGE), F32), pltpu.VMEM((8, PAGE), F32), pltpu.VMEM((256, PAGE), F32)],
        ),
        out_shape=jax.ShapeDtypeStruct((nseq, 1, 256), F32),
        compiler_params=_cparams("arbitrary", "arbitrary"),
        name="decode_attention",
    )(page_table, q.reshape(nseq, 1, 256), bias8, *([pool_kt] * pp), *([pool_vt] * pp))
    return out.reshape(nseq, 256)


def _dot_f32(a, b):
    a0, a1, a2 = _split3(a)
    b0, b1, b2 = _split3(b)
    return (_dot(a0, b0) + (_dot(a0, b1) + _dot(a1, b0))
            + (_dot(a0, b2) + _dot(a2, b0) + _dot(a1, b1)))


_R_DEC, _R_XDT, _R_BM, _R_CM, _R_F, _R_K, _R_Q, _R_V, _R_END = 0, 256, 512, 640, 768, 1024, 1280, 1536, 1792


def _decode_mixers_kernel(zx_ref, u_ref, hg_ref, cs_ref, hssm_ref, x0re_ref, x0im_ref, hhg_ref,
                          cw_ref, cb_ref, dtb_ref, alog_ref, d_ref, ngs_ref, lb_ref, ngh_ref,
                          lre_ref, lim_ref, bre_ref, bim_ref, cre_ref, cim_ref,
                          ob_ref, cso_ref, hssmo_ref, yc_ref, xre_ref, xim_ref, od_ref, hhgo_ref,
                          rows_ref, yssm_ref, yhg_ref, *, nseq):
    (cw_ref, cb_ref, dtb_ref, alog_ref, d_ref, ngs_ref, lb_ref, ngh_ref,
     lre_ref, lim_ref, bre_ref, bim_ref, cre_ref, cim_ref) = _drop_layer_axis(
        cw_ref, cb_ref, dtb_ref, alog_ref, d_ref, ngs_ref, lb_ref, ngh_ref,
        lre_ref, lim_ref, bre_ref, bim_ref, cre_ref, cim_ref)
    z = zx_ref[:, 0:256]
    xbc = zx_ref[:, 256:768]
    s0, s1, s2 = cs_ref[:, 0:512], cs_ref[:, 512:1024], cs_ref[:, 1024:1536]
    conv = cb_ref[...] + cw_ref[0:1, :] * s0 + cw_ref[1:2, :] * s1 + cw_ref[2:3, :] * s2 + cw_ref[3:4, :] * xbc
    cso_ref[:, 0:512] = s1
    cso_ref[:, 512:1024] = s2
    cso_ref[:, 1024:1536] = xbc
    act = _silu(conv)
    xs = act[:, 0:256]
    dt = _softplus(zx_ref[:, 768:1024] + dtb_ref[...])
    rows_ref[:, _R_DEC:_R_XDT] = jnp.exp(dt * (-jnp.exp(alog_ref[...])))
    rows_ref[:, _R_XDT:_R_BM] = xs * dt
    rows_ref[:, _R_BM:_R_F] = act[:, 256:512]
    lb = lb_ref[...]
    fr = hg_ref[:, 256:512]
    rows_ref[:, _R_F:_R_K] = lb + (1.0 - lb) * _sigmoid(fr)
    rows_ref[:, _R_K:_R_Q] = (1.0 - lb) * _sigmoid(-fr)
    rows_ref[:, _R_Q:_R_V] = _silu(hg_ref[:, 0:256])
    rows_ref[:, _R_V:_R_END] = hg_ref[:, 512:768]

    eye = _iota2((256, 256), 0) == _iota2((256, 256), 1)
    row_id = _iota2((256, 64), 0)

    def to_col(r):
        return jnp.sum(jnp.where(eye, r, 0.0), axis=1, keepdims=True)

    def to_row(c):
        return jnp.sum(jnp.where(eye, c, 0.0), axis=0, keepdims=True)

    def per_seq(b, get):
        bm = get(_R_BM, _R_CM)
        cm = get(_R_CM, _R_F)
        bm_rows = jnp.where(row_id < 128, bm[:, 0:64], bm[:, 64:128])
        cm_rows = jnp.where(row_id < 128, cm[:, 0:64], cm[:, 64:128])
        hn = hssm_ref[b] * to_col(get(_R_DEC, _R_XDT)) + to_col(get(_R_XDT, _R_BM)) * bm_rows
        hssmo_ref[b] = hn
        y_ssm = to_row(jnp.sum(hn * cm_rows, axis=1, keepdims=True))
        v = get(_R_V, _R_END)
        v_rows = jnp.where(row_id < 64, v[:, 0:64],
                           jnp.where(row_id < 128, v[:, 64:128],
                                     jnp.where(row_id < 192, v[:, 128:192], v[:, 192:256])))
        gn = hhg_ref[b] * to_col(get(_R_F, _R_K)) + to_col(get(_R_K, _R_Q)) * v_rows
        hhgo_ref[b] = gn
        qg = to_col(get(_R_Q, _R_V)) * gn
        y_hg = jnp.concatenate(
            [jnp.sum(qg[h * 64:(h + 1) * 64], axis=0, keepdims=True) for h in range(4)], axis=1)
        return y_ssm, y_hg

    def per_octet(o, carry):
        base = pl.multiple_of(o * 8, 8)
        blk = rows_ref[pl.ds(base, 8), :]
        ys = [per_seq(base + r, lambda lo, hi, r=r: blk[r:r + 1, lo:hi]) for r in range(8)]
        yssm_ref[pl.ds(base, 8), :] = jnp.concatenate([y[0] for y in ys], axis=0)
        yhg_ref[pl.ds(base, 8), :] = jnp.concatenate([y[1] for y in ys], axis=0)
        return carry

    lax.fori_loop(0, nseq // 8, per_octet, 0)

    y = yssm_ref[...] + d_ref[...] * xs
    ob_ref[...] = _group_rms(y * _silu(z), ngs_ref[...], 128)
    od_ref[...] = _head_rms(yhg_ref[...], ngh_ref[...]) * _silu(hg_ref[:, 768:1024])
    u = u_ref[...]
    x0r, x0i = x0re_ref[...], x0im_ref[...]
    lr, li = lre_ref[...], lim_ref[...]
    xr = lr * x0r - li * x0i + _dot_f32(u, bre_ref[...])
    xi = lr * x0i + li * x0r + _dot_f32(u, bim_ref[...])
    xre_ref[...] = xr
    xim_ref[...] = xi
    yc_ref[...] = _dot_f32(xr, cre_ref[...]) - _dot_f32(xi, cim_ref[...])


def _decode_mixers(zx, u, hg4, conv_state, h_ssm, x0_re, x0_im, h_hg, params, *, layer):
    nseq = zx.shape[0]
    acts = (zx, u, hg4, conv_state, h_ssm, x0_re, x0_im, h_hg)
    args = (*acts, *params)
    full = lambda a: pl.BlockSpec(a.shape, lambda i: (0,) * a.ndim)
    out_shape = [
        jax.ShapeDtypeStruct((nseq, 256), F32),
        jax.ShapeDtypeStruct((nseq, 1536), F32),
        jax.ShapeDtypeStruct((nseq, 256, 64), F32),
        jax.ShapeDtypeStruct((nseq, 256), F32),
        jax.ShapeDtypeStruct((nseq, 1024), F32),
        jax.ShapeDtypeStruct((nseq, 1024), F32),
        jax.ShapeDtypeStruct((nseq, 256), F32),
        jax.ShapeDtypeStruct((nseq, 256, 64), F32),
    ]
    return pl.pallas_call(
        functools.partial(_decode_mixers_kernel, nseq=nseq),
        grid=(1,),
        in_specs=[full(a) for a in acts] + [_layer_spec(p, layer) for p in params],
        out_specs=[full(s) for s in out_shape],
        out_shape=out_shape,
        scratch_shapes=[pltpu.VMEM((nseq, _R_END), F32), pltpu.VMEM((nseq, 256), F32), pltpu.VMEM((nseq, 256), F32)],
        compiler_params=_cparams("arbitrary"),
        name="decode_mixers",
    )(*args)


def _s5_decode_matrices(log_mag, arg, bb_re, bb_im, c_re, c_im):
    g, n = S5_GROUPS, S5_STATE
    lam_re, lam_im = _lam_pow(log_mag, arg, 1.0)
    return (lam_re.reshape(1, g * n), lam_im.reshape(1, g * n),
            _s5_expand_b(bb_re), _s5_expand_b(bb_im), _s5_expand_c(c_re), _s5_expand_c(c_im))


ATTN_TQ = 1024
ATTN_TK = 256
PROJ_TILE = 512
SSD_TILE = 256
HGRN_TILE = 256
POST_TILE = 512


def _rearranged_w_in(w_in):
    q, k, v, z, xbc, dt, u, hq, hf, hi, hg = jnp.split(
        w_in, [256, 512, 768, 1024, 1536, 1540, 1796, 2052, 2308, 2564], axis=1)
    dt_full = jnp.repeat(dt, HEAD_DIM, axis=1)
    q_scale = LOG2E * HEAD_DIM ** -0.5
    return jnp.concatenate([q * q_scale, k, v, z, xbc, dt_full, u, hq, hf, hi, hg], axis=1).astype(BF16)


def kernel(x_prompt, x_sample, cache_k, cache_v, state_ssm_conv, state_ssm, state_s5_re, state_s5_im, state_hgrn, state_ffn_conv, page_table, ln1_g, ln1_b, ln2_g, ln2_b, w_in, w_out, sb_logit_bias, ssm_conv_w, ssm_conv_b, ssm_dt_bias, ssm_a_log, ssm_d, ssm_norm_g, s5_a_re, s5_a_im, s5_b_re, s5_b_im, s5_c_re, s5_c_im, s5_d, s5_log_dt, s5_w_glu, s5_b_glu, hg_lb_logits, hg_norm_g, w_up, ffn_conv_w, ffn_conv_b, w_down):
    depth = w_in.shape[0]
    bp, seq, dm = x_prompt.shape
    ns = x_sample.shape[0]
    n_phys = cache_k.shape[1]
    alpha = (2 * depth) ** 0.25
    row = lambda a: a[:, None, :]
    rep = lambda a: jnp.repeat(a, HEAD_DIM, axis=1)[:, None, :]
    pr = jax.nn.softmax(hg_lb_logits.astype(F32), axis=0)
    lb = row(jnp.cumsum(pr, axis=0) - pr[0:1])
    ngh = row(hg_norm_g)
    w_p = jax.vmap(_rearranged_w_in)(w_in)
    disc = jax.vmap(_s5_discretize)(s5_a_re, s5_a_im, s5_b_re, s5_b_im, s5_log_dt)
    s5_mats = jax.vmap(_s5_prompt_matrices)(*disc, s5_c_re, s5_c_im)
    ssd_w = (ssm_conv_w, row(ssm_conv_b), rep(ssm_dt_bias), rep(ssm_a_log), rep(ssm_d), row(ssm_norm_g))
    post_w = (row(s5_d), s5_w_glu.astype(BF16), row(s5_b_glu), w_out.astype(BF16), row(ln1_g), row(ln1_b),
              w_up.astype(BF16), ffn_conv_w, row(ffn_conv_b), w_down.astype(BF16), row(ln2_g), row(ln2_b))
    dec_w = (*ssd_w, lb, ngh, *jax.vmap(_s5_decode_matrices)(*disc, s5_c_re, s5_c_im))
    bias8 = jnp.broadcast_to(jnp.pad(sb_logit_bias * LOG2E, ((0, 0), (0, 4)))[:, :, None], (depth, 8, PAGE))
    pool_kt = cache_k.transpose(0, 1, 3, 4, 2).reshape(depth * n_phys, 256, PAGE)
    pool_vt = cache_v.transpose(0, 1, 3, 4, 2).reshape(depth * n_phys, 256, PAGE)

    xp = x_prompt.reshape(bp * seq, dm)
    xs = x_sample.reshape(ns, dm)
    k_all = jnp.zeros((depth, bp, 256, seq), F32)
    v_all = jnp.zeros((depth, bp, 256, seq), F32)
    outs_p, outs_s = [], []
    for l in range(depth):
        q, vb, zx, u, hg4, kt, k_all, v_all = _projection(
            xp, w_p, tm=PROJ_TILE, tk=ATTN_TK, seq=seq, layer=l, k_all=k_all, v_all=v_all)
        oa = _prompt_attention(q, kt, vb, sb_logit_bias, batch=bp, seq=seq, tq=ATTN_TQ, tk=ATTN_TK, layer=l)
        ob, conv_p, ssm_p = _prompt_ssd(zx, *ssd_w, batch=bp, seq=seq, t=SSD_TILE, layer=l)
        yc, re_p, im_p = _s5_prompt_scan(u, s5_mats, batch=bp, seq=seq, layer=l)
        od, hg_p = _prompt_hgrn(hg4, lb, ngh, batch=bp, seq=seq, t=HGRN_TILE, layer=l)
        xp, ffn_p = _post(xp, oa, ob, yc, u, od, post_w, alpha=alpha, layer=l, batch=bp, seq=seq, t=POST_TILE)
        outs_p.append((conv_p, ssm_p, re_p, im_p, hg_p, ffn_p))

        q, k, v, zx, u, hg4 = _projection(xs, w_p, tm=ns, layer=l)
        oa = _decode_attention(q, pool_kt, pool_vt, page_table, bias8, base=l * n_phys, layer=l)
        ob, conv_s, ssm_s, yc, re_s, im_s, od, hg_s = _decode_mixers(
            zx, u, hg4, state_ssm_conv[l].reshape(ns, 1536), state_ssm[l].reshape(ns, 256, 64),
            state_s5_re[l].reshape(ns, 1024), state_s5_im[l].reshape(ns, 1024),
            state_hgrn[l].reshape(ns, 256, 64), dec_w, layer=l)
        xs, ffn_s = _post(xs, oa, ob, yc, u, od, post_w, alpha=alpha, layer=l,
                          ffn_state=state_ffn_conv[l].reshape(ns, 2 * D_FF))
        outs_s.append((k.reshape(ns, 1, 4, HEAD_DIM), v.reshape(ns, 1, 4, HEAD_DIM),
                       conv_s.reshape(ns, 3, 512), ssm_s.reshape(ns, 4, 64, 64),
                       re_s.reshape(ns, S5_GROUPS, S5_STATE), im_s.reshape(ns, S5_GROUPS, S5_STATE),
                       hg_s.reshape(ns, 4, 64, 64), ffn_s.reshape(ns, 2, D_FF)))

    sp = [jnp.stack(col, axis=0) for col in zip(*outs_p)]
    ss = [jnp.stack(col, axis=0) for col in zip(*outs_s)]
    k_prompt = k_all.reshape(depth, bp, 4, HEAD_DIM, seq).transpose(0, 1, 4, 2, 3)
    v_prompt = v_all.reshape(depth, bp, 4, HEAD_DIM, seq).transpose(0, 1, 4, 2, 3)
    return (xp.reshape(bp, seq, dm), xs.reshape(ns, 1, dm), k_prompt, v_prompt, ss[0], ss[1], sp[0], ss[2],
            sp[1], ss[3], sp[2], ss[4], sp[3], ss[5], sp[4], ss[6], sp[5], ss[7])
```
